```python
import math, functools
import jax, jax.numpy as jnp
from jax import lax
import numpy as np

D_MODEL = 2048
BATCH = 8
SEQ = 2048
DEPTH = 2
DEC_BATCH = 32
DEC_SEQ = 1
PAST_LEN = 8192
PAGE_SIZE = 128

N_BRANCH = 4
MIX_W = D_MODEL // N_BRANCH
FOX_HD = 128
FOX_H = MIX_W // FOX_HD
FOX_BLOCK = 128
FOX_BF_INIT = 5.0
S5_GROUP = 16
S5_G = MIX_W // S5_GROUP
S5_P = 64
HG_DK = 128
HG_DV = 128
HG_H = MIX_W // HG_DV
HG_CHUNK = 64
GM_CHUNK = 128
GM_GROUPS = 4
GM_GW = MIX_W // GM_GROUPS
D_FF = ((8 * D_MODEL // 3 + 255) // 256) * 256
DEEPNORM_ALPHA = (2 * DEPTH) ** 0.25
DEEPNORM_BETA = (8 * DEPTH) ** -0.25
LN_EPS = 1e-5
RMS_EPS = 1e-6
IN_SPLITS = (MIX_W, MIX_W, MIX_W, FOX_H, MIX_W, HG_H * HG_DK, HG_H * HG_DK, HG_H * HG_DV, HG_H * HG_DV, MIX_W, MIX_W, N_BRANCH * D_MODEL)
N_IN = sum(IN_SPLITS)

kernel_name = 'hybrid_fox_s5_hgrn2_gmlp_decode_step'


def layer_norm(x, g, b):
    xf = x.astype(jnp.float32)
    mu = jnp.mean(xf, axis=-1, keepdims=True)
    var = jnp.mean(jnp.square(xf - mu), axis=-1, keepdims=True)
    return ((xf - mu) * lax.rsqrt(var + LN_EPS) * g + b).astype(x.dtype)


def split_cols(z):
    bounds = np.cumsum(IN_SPLITS)[:-1].tolist()
    return jnp.split(z, bounds, axis=-1)


def fox_prompt(q, k, v, logf):
    B, L, H, hd = q.shape
    scale = hd ** -0.5
    c = jnp.cumsum(logf, axis=1)
    c_keys = c.transpose(0, 2, 1)
    nb = L // FOX_BLOCK
    q_blocks = q.reshape(B, nb, FOX_BLOCK, H, hd).transpose(1, 0, 2, 3, 4)
    c_blocks = c.reshape(B, nb, FOX_BLOCK, H).transpose(1, 0, 3, 2)
    k_pos = jnp.arange(L)

    def block(args):
        qi, ci, i = args
        q_pos = i * FOX_BLOCK + jnp.arange(FOX_BLOCK)
        s = jnp.einsum('bqhd,bkhd->bhqk', qi, k).astype(jnp.float32) * scale
        s = s + ci[..., None] - c_keys[:, :, None, :]
        s = jnp.where(k_pos[None, :] <= q_pos[:, None], s, -jnp.inf)
        p = jax.nn.softmax(s, axis=-1).astype(v.dtype)
        return jnp.einsum('bhqk,bkhd->bqhd', p, v)

    o = lax.map(block, (q_blocks, c_blocks, jnp.arange(nb)))
    return o.transpose(1, 0, 2, 3, 4).reshape(B, L, H * hd)


def fox_sample(k_past, v_past, logf_past, q, k, v, logf):
    B, L, H, hd = q.shape
    P = k_past.shape[1]
    scale = hd ** -0.5
    lp = logf_past.astype(jnp.float32)
    suffix = lax.cumsum(lp, axis=1, reverse=True) - lp
    c_new = jnp.cumsum(logf, axis=1).transpose(0, 2, 1)
    s_past = jnp.einsum('bqhd,bkhd->bhqk', q, k_past).astype(jnp.float32) * scale
    s_past = s_past + suffix.transpose(0, 2, 1)[:, :, None, :] + c_new[..., None]
    s_new = jnp.einsum('bqhd,bkhd->bhqk', q, k).astype(jnp.float32) * scale
    s_new = s_new + c_new[..., None] - c_new[:, :, None, :]
    causal = jnp.tril(jnp.ones((L, L), dtype=bool))
    s_new = jnp.where(causal, s_new, -jnp.inf)
    p = jax.nn.softmax(jnp.concatenate([s_past, s_new], axis=-1), axis=-1).astype(v.dtype)
    o = jnp.einsum('bhqk,bkhd->bqhd', p[..., :P], v_past) + jnp.einsum('bhqk,bkhd->bqhd', p[..., P:], v)
    return o.reshape(B, L, H * hd)


def s5_mixer(u, lam_re, lam_im, log_dt, b_re, b_im, c_re, c_im, d, w_glu, b_glu, h0_re, h0_im):
    B, L, _ = u.shape
    f32 = jnp.float32
    uf = u.astype(f32).reshape(B, L, S5_G, S5_GROUP)
    dt = jnp.exp(log_dt.astype(f32))[:, None]
    lr, li = lam_re.astype(f32), lam_im.astype(f32)
    mag = jnp.exp(lr * dt)
    ab_re, ab_im = mag * jnp.cos(li * dt), mag * jnp.sin(li * dt)
    den = lr * lr + li * li
    nr, ni = ab_re - 1.0, ab_im
    coef_re = (nr * lr + ni * li) / den
    coef_im = (ni * lr - nr * li) / den
    br, bi = b_re.astype(f32), b_im.astype(f32)
    bb_re = coef_re[..., None] * br - coef_im[..., None] * bi
    bb_im = coef_re[..., None] * bi + coef_im[..., None] * br
    bu_re = jnp.einsum('gph,blgh->blgp', bb_re, uf)
    bu_im = jnp.einsum('gph,blgh->blgp', bb_im, uf)
    a_re = jnp.broadcast_to(ab_re, bu_re.shape)
    a_im = jnp.broadcast_to(ab_im, bu_im.shape)

    def combine(e1, e2):
        a1r, a1i, b1r, b1i = e1
        a2r, a2i, b2r, b2i = e2
        return (a2r * a1r - a2i * a1i, a2r * a1i + a2i * a1r,
                a2r * b1r - a2i * b1i + b2r, a2r * b1i + a2i * b1r + b2i)

    pr, pi_, hr, hi = lax.associative_scan(combine, (a_re, a_im, bu_re, bu_im), axis=1)
    h0r = h0_re.astype(f32)[:, None]
    h0i = h0_im.astype(f32)[:, None]
    hr = hr + pr * h0r - pi_ * h0i
    hi = hi + pr * h0i + pi_ * h0r
    y = jnp.einsum('ghp,blgp->blgh', c_re.astype(f32), hr) - jnp.einsum('ghp,blgp->blgh', c_im.astype(f32), hi)
    y = y.reshape(B, L, -1) + d.astype(f32) * uf.reshape(B, L, -1)
    g = jax.nn.gelu(y)
    out = g * jax.nn.sigmoid(g @ w_glu.astype(f32) + b_glu.astype(f32))
    return out.astype(u.dtype), hr[:, -1], hi[:, -1]


def hgrn2_mixer(q, f_logit, i_in, g, lb, norm_g, s0):
    B, L, H, dk = q.shape
    dv = i_in.shape[-1]
    f32 = jnp.float32
    lbr = lb.astype(f32).reshape(H, dk)
    log_f = jnp.logaddexp(jnp.log(lbr), jnp.log1p(-lbr) + jax.nn.log_sigmoid(f_logit.astype(f32)))
    c = min(HG_CHUNK, L)
    pad = (-L) % c
    n = (L + pad) // c

    def prep(t):
        t = jnp.pad(t, ((0, 0), (0, pad), (0, 0), (0, 0)))
        return t.reshape(B, n, c, H, -1).transpose(1, 0, 3, 2, 4)

    qc, lfc, vc = prep(q.astype(f32)), prep(log_f), prep(i_in.astype(f32))
    mask = jnp.tril(jnp.ones((c, c), dtype=bool))

    def step(S, args):
        qt, lft, vt = args
        kt = -jnp.expm1(lft)
        bc = jnp.cumsum(lft, axis=2)
        diff = bc[:, :, :, None, :] - bc[:, :, None, :, :]
        decay = jnp.exp(jnp.where(mask[:, :, None], diff, -jnp.inf))
        att = jnp.einsum('bhtd,bhtsd,bhsd->bhts', qt, decay, kt)
        o = att @ vt + jnp.einsum('bhtd,bhdv->bhtv', qt * jnp.exp(bc), S)
        bl = bc[:, :, -1]
        S = jnp.exp(bl)[..., None] * S + jnp.einsum('bhsd,bhsv->bhdv', kt * jnp.exp(bl[:, :, None, :] - bc), vt)
        return S, o

    S, o = lax.scan(step, s0.astype(f32), (qc, lfc, vc))
    o = o.transpose(1, 0, 3, 2, 4).reshape(B, n * c, H, dv)[:, :L]
    o = o * lax.rsqrt(jnp.mean(o * o, axis=-1, keepdims=True) + RMS_EPS) * norm_g.astype(f32).reshape(H, dv)
    o = o.reshape(B, L, H * dv) * jax.nn.silu(g.astype(f32))
    return o.astype(q.dtype), S


def gmlp_mixer(u, v, ln_g, ln_b, w_s, b_s):
    B, L, W = u.shape
    vn = layer_norm(v, ln_g, ln_b)
    pad = (-L) % GM_CHUNK
    n = (L + pad) // GM_CHUNK
    vp = jnp.pad(vn, ((0, 0), (0, pad), (0, 0))).reshape(B, n, GM_CHUNK, GM_GROUPS, GM_GW)
    ws = w_s * jnp.tril(jnp.ones((GM_CHUNK, GM_CHUNK), dtype=w_s.dtype))
    mixed = jnp.einsum('gts,bnsgc->bntgc', ws, vp) + b_s.T[None, None, :, :, None]
    mixed = mixed.reshape(B, n * GM_CHUNK, W)[:, :L]
    return u * mixed, vn


def trunk_layer(x, l, W, fox_fn, s5_h0_re, s5_h0_im, hg_s0):
    B, L, _ = x.shape
    z = x @ W['w_in'][l]
    fq, fk, fv, ff, su, hq, hf, hi, hg, gu, gv, gt = split_cols(z)
    fq = fq.reshape(B, L, FOX_H, FOX_HD)
    fk = fk.reshape(B, L, FOX_H, FOX_HD)
    fv = fv.reshape(B, L, FOX_H, FOX_HD)
    logf = jax.nn.log_sigmoid(ff.astype(jnp.float32) + W['fox_bf'][l].astype(jnp.float32))
    o_a = fox_fn(fq, fk, fv, logf)
    o_b, s5_re, s5_im = s5_mixer(su, W['s5_lambda_re'][l], W['s5_lambda_im'][l], W['s5_log_dt'][l],
                                 W['s5_b_re'][l], W['s5_b_im'][l], W['s5_c_re'][l], W['s5_c_im'][l],
                                 W['s5_d'][l], W['s5_w_glu'][l], W['s5_b_glu'][l], s5_h0_re, s5_h0_im)
    lb_all = jnp.cumsum(jax.nn.softmax(W['hgrn_lb'].astype(jnp.float32), axis=0), axis=0)
    lb = lb_all[l] - lb_all[0]
    o_c, hg_state = hgrn2_mixer(hq.reshape(B, L, HG_H, HG_DK), hf.reshape(B, L, HG_H, HG_DK),
                                hi.reshape(B, L, HG_H, HG_DV), hg, lb, W['hgrn_norm_g'][l], hg_s0)
    o_d, v_rows = gmlp_mixer(jax.nn.gelu(gu), jax.nn.gelu(gv), W['gmlp_ln_g'][l], W['gmlp_ln_b'][l],
                             W['gmlp_w_s'][l], W['gmlp_b_s'][l])
    branches = jnp.stack([o_a, o_b, o_c, o_d], axis=2)
    proj = jnp.einsum('blkc,kcd->blkd', branches, W['w_branch'][l])
    gates = jax.nn.sigmoid(gt.reshape(B, L, N_BRANCH, D_MODEL))
    merged = jnp.einsum('blkd,blkd->bld', gates, proj)
    x = layer_norm(DEEPNORM_ALPHA * x + merged @ W['w_out'][l], W['ln1_g'][l], W['ln1_b'][l])
    gate_ff, up_ff = jnp.split(x @ W['w_ffn_up'][l], 2, axis=-1)
    x = layer_norm(DEEPNORM_ALPHA * x + (jax.nn.silu(gate_ff) * up_ff) @ W['w_ffn_down'][l], W['ln2_g'][l], W['ln2_b'][l])
    return x, (fk, fv, logf, s5_re, s5_im, hg_state, v_rows)


def setup_inputs(seed: int = 0) -> dict:
    key = jax.random.key(seed)
    ks = iter(jax.random.split(key, 48))
    f32 = jnp.float32
    nrm = lambda shape, s=1.0: jax.random.normal(next(ks), shape, f32) * s
    n_pages = PAST_LEN // PAGE_SIZE
    n_phys = (DEC_BATCH * n_pages * 5) // 4
    perm = jax.random.permutation(next(ks), n_phys)
    page_table = perm[:DEC_BATCH * n_pages].reshape(DEC_BATCH, n_pages).astype(jnp.int32)
    inp = {}
    inp['x_prompt'] = nrm((BATCH, SEQ, D_MODEL))
    inp['x_sample'] = nrm((DEC_BATCH, DEC_SEQ, D_MODEL))
    inp['cache_k'] = nrm((DEPTH, n_phys, PAGE_SIZE, FOX_H, FOX_HD))
    inp['cache_v'] = nrm((DEPTH, n_phys, PAGE_SIZE, FOX_H, FOX_HD))
    inp['cache_logf'] = jax.nn.log_sigmoid(nrm((DEPTH, n_phys, PAGE_SIZE, FOX_H), 0.5) + 6.0)
    inp['page_table'] = page_table
    inp['state_s5_re'] = nrm((DEPTH, DEC_BATCH, S5_G, S5_P), 0.5)
    inp['state_s5_im'] = nrm((DEPTH, DEC_BATCH, S5_G, S5_P), 0.5)
    inp['state_hgrn'] = nrm((DEPTH, DEC_BATCH, HG_H, HG_DK, HG_DV), 0.5)
    inp['w_in'] = nrm((DEPTH, D_MODEL, N_IN), D_MODEL ** -0.5)
    inp['fox_bf'] = FOX_BF_INIT + nrm((DEPTH, FOX_H), 0.1)
    inp['s5_lambda_re'] = -0.5 + nrm((DEPTH, S5_G, S5_P), 0.01)
    inp['s5_lambda_im'] = jnp.pi * jnp.arange(S5_P, dtype=f32) + nrm((DEPTH, S5_G, S5_P), 0.01)
    inp['s5_log_dt'] = jax.random.uniform(next(ks), (DEPTH, S5_G), f32, math.log(1e-3), math.log(1e-1))
    inp['s5_b_re'] = nrm((DEPTH, S5_G, S5_P, S5_GROUP), (2 * S5_GROUP) ** -0.5)
    inp['s5_b_im'] = nrm((DEPTH, S5_G, S5_P, S5_GROUP), (2 * S5_GROUP) ** -0.5)
    inp['s5_c_re'] = nrm((DEPTH, S5_G, S5_GROUP, S5_P), S5_P ** -0.5)
    inp['s5_c_im'] = nrm((DEPTH, S5_G, S5_GROUP, S5_P), S5_P ** -0.5)
    inp['s5_d'] = nrm((DEPTH, MIX_W))
    inp['s5_w_glu'] = nrm((DEPTH, MIX_W, MIX_W), MIX_W ** -0.5)
    inp['s5_b_glu'] = nrm((DEPTH, MIX_W), 0.01)
    inp['hgrn_lb'] = nrm((DEPTH, HG_H * HG_DK))
    inp['hgrn_norm_g'] = 1.0 + nrm((DEPTH, HG_H * HG_DV), 0.1)
    inp['gmlp_ln_g'] = 1.0 + nrm((DEPTH, MIX_W), 0.1)
    inp['gmlp_ln_b'] = nrm((DEPTH, MIX_W), 0.01)
    inp['gmlp_w_s'] = nrm((DEPTH, GM_GROUPS, GM_CHUNK, GM_CHUNK), GM_CHUNK ** -0.5)
    inp['gmlp_b_s'] = 1.0 + nrm((DEPTH, GM_GROUPS, GM_CHUNK), 0.1)
    inp['w_branch'] = nrm((DEPTH, N_BRANCH, MIX_W, D_MODEL), MIX_W ** -0.5)
    inp['w_out'] = nrm((DEPTH, D_MODEL, D_MODEL), D_MODEL ** -0.5 * DEEPNORM_BETA)
    inp['ln1_g'] = 1.0 + nrm((DEPTH, D_MODEL), 0.1)
    inp['ln1_b'] = nrm((DEPTH, D_MODEL), 0.01)
    inp['w_ffn_up'] = nrm((DEPTH, D_MODEL, 2 * D_FF), D_MODEL ** -0.5)
    inp['w_ffn_down'] = nrm((DEPTH, D_FF, D_MODEL), D_FF ** -0.5 * DEEPNORM_BETA)
    inp['ln2_g'] = 1.0 + nrm((DEPTH, D_MODEL), 0.1)
    inp['ln2_b'] = nrm((DEPTH, D_MODEL), 0.01)
    return inp


def reference(x_prompt, x_sample, cache_k, cache_v, cache_logf, page_table, state_s5_re, state_s5_im, state_hgrn,
              w_in, fox_bf, s5_lambda_re, s5_lambda_im, s5_log_dt, s5_b_re, s5_b_im, s5_c_re, s5_c_im, s5_d,
              s5_w_glu, s5_b_glu, hgrn_lb, hgrn_norm_g, gmlp_ln_g, gmlp_ln_b, gmlp_w_s, gmlp_b_s, w_branch, w_out,
              ln1_g, ln1_b, w_ffn_up, w_ffn_down, ln2_g, ln2_b):
    W = {'w_in': w_in, 'fox_bf': fox_bf, 's5_lambda_re': s5_lambda_re, 's5_lambda_im': s5_lambda_im,
         's5_log_dt': s5_log_dt, 's5_b_re': s5_b_re, 's5_b_im': s5_b_im, 's5_c_re': s5_c_re, 's5_c_im': s5_c_im,
         's5_d': s5_d, 's5_w_glu': s5_w_glu, 's5_b_glu': s5_b_glu, 'hgrn_lb': hgrn_lb, 'hgrn_norm_g': hgrn_norm_g,
         'gmlp_ln_g': gmlp_ln_g, 'gmlp_ln_b': gmlp_ln_b, 'gmlp_w_s': gmlp_w_s, 'gmlp_b_s': gmlp_b_s,
         'w_branch': w_branch, 'w_out': w_out, 'ln1_g': ln1_g, 'ln1_b': ln1_b, 'w_ffn_up': w_ffn_up,
         'w_ffn_down': w_ffn_down, 'ln2_g': ln2_g, 'ln2_b': ln2_b}
    b_p = x_prompt.shape[0]
    b_s = x_sample.shape[0]
    n_past = page_table.shape[1] * cache_k.shape[2]
    s5_zero = jnp.zeros((b_p, S5_G, S5_P), jnp.float32)
    hg_zero = jnp.zeros((b_p, HG_H, HG_DK, HG_DV), jnp.float32)
    y_prompt, y_sample = x_prompt, x_sample
    p_states, s_states = [], []
    for l in range(DEPTH):
        y_prompt, sp = trunk_layer(y_prompt, l, W, fox_prompt, s5_zero, s5_zero, hg_zero)
        k_past = cache_k[l][page_table].reshape(b_s, n_past, FOX_H, FOX_HD)
        v_past = cache_v[l][page_table].reshape(b_s, n_past, FOX_H, FOX_HD)
        lf_past = cache_logf[l][page_table].reshape(b_s, n_past, FOX_H)
        y_sample, ss = trunk_layer(y_sample, l, W, functools.partial(fox_sample, k_past, v_past, lf_past),
                                   state_s5_re[l], state_s5_im[l], state_hgrn[l])
        p_states.append(sp)
        s_states.append(ss)
    fox_k_prompt = jnp.stack([s[0] for s in p_states])
    fox_v_prompt = jnp.stack([s[1] for s in p_states])
    fox_logf_prompt = jnp.stack([s[2] for s in p_states])
    fox_k_sample = jnp.stack([s[0] for s in s_states])
    fox_v_sample = jnp.stack([s[1] for s in s_states])
    fox_logf_sample = jnp.stack([s[2] for s in s_states])
    s5_re_prompt = jnp.stack([s[3] for s in p_states])
    s5_im_prompt = jnp.stack([s[4] for s in p_states])
    s5_re_sample = jnp.stack([s[3] for s in s_states])
    s5_im_sample = jnp.stack([s[4] for s in s_states])
    hgrn_prompt = jnp.stack([s[5] for s in p_states])
    hgrn_sample = jnp.stack([s[5] for s in s_states])
    gmlp_v_sample = jnp.stack([s[6] for s in s_states])
    return (y_prompt, y_sample, fox_k_prompt, fox_v_prompt, fox_logf_prompt, fox_k_sample, fox_v_sample, fox_logf_sample, s5_re_prompt, s5_im_prompt, s5_re_sample, s5_im_sample, hgrn_prompt, hgrn_sample, gmlp_v_sample)
```

```python
import functools
import math

import numpy as np
import jax
import jax.numpy as jnp
from jax import lax
from jax.experimental import pallas as pl
from jax.experimental.pallas import tpu as pltpu

F32 = jnp.float32
BF16 = jnp.bfloat16

D_MODEL = 2048
DEPTH = 2
N_BRANCH = 4
MIX_W = D_MODEL // N_BRANCH
FOX_HD = 128
FOX_H = MIX_W // FOX_HD
S5_GROUP = 16
S5_G = MIX_W // S5_GROUP
S5_P = 64
S5_STATE = S5_G * S5_P
HG_DK = 128
HG_DV = 128
HG_H = MIX_W // HG_DV
HG_CHUNK = 64
HG_SUB = 16
GM_CHUNK = 128
GM_GROUPS = 4
GM_GW = MIX_W // GM_GROUPS
D_FF = ((8 * D_MODEL // 3 + 255) // 256) * 256
DEEPNORM_ALPHA = (2 * DEPTH) ** 0.25
LN_EPS = 1e-5
RMS_EPS = 1e-6
LANES = 128
SUBLANES = 8
BF16_ROWS = 16
VMEM_LIMIT = 48 * 1024 * 1024


def _cparams(*sem):
    return pltpu.CompilerParams(dimension_semantics=sem, vmem_limit_bytes=VMEM_LIMIT)


def _log_sigmoid(x):
    return jnp.minimum(x, 0.0) - jnp.log1p(jnp.exp(-jnp.abs(x)))


def _layer_norm(x, g, b):
    mu = jnp.mean(x, axis=-1, keepdims=True)
    xc = x - mu
    var = jnp.mean(xc * xc, axis=-1, keepdims=True)
    return xc * lax.rsqrt(var + LN_EPS) * g + b


def _split3(x):
    hi = x.astype(BF16)
    r1 = x - hi.astype(F32)
    mid = r1.astype(BF16)
    lo = (r1 - mid.astype(F32)).astype(BF16)
    return hi, mid, lo


def _dot(a, b):
    return jnp.dot(a, b, preferred_element_type=F32)


def _dot_nt(a, b):
    return lax.dot_general(a, b, (((1,), (1,)), ((), ())), preferred_element_type=F32)


def _dot_tn(a, b):
    return lax.dot_general(a, b, (((0,), (0,)), ((), ())), preferred_element_type=F32)


def _proj_kernel(x_ref, w_ref, b_ref, *out_refs, act):
    z = _dot(x_ref[...], w_ref[...])
    if act == "gelu":
        z = jax.nn.gelu(z)
    elif act == "sigmoid":
        z = jax.nn.sigmoid(z)
    elif act == "log_sigmoid_bias":
        z = _log_sigmoid(z + b_ref[...])
    for o in out_refs:
        o[...] = z.astype(o.dtype)


def _proj(x, w, *, act=None, bias=None, out_dtypes=(F32,), tn=512):
    m, k = x.shape
    n = w.shape[1]
    tm = min(m, 512)
    tn = min(n, tn)
    if bias is None:
        bias = jnp.zeros((1, n), F32)
    return pl.pallas_call(
        functools.partial(_proj_kernel, act=act),
        grid=(n // tn, m // tm),
        in_specs=[pl.BlockSpec((tm, k), lambda j, i: (i, 0)),
                  pl.BlockSpec((k, tn), lambda j, i: (0, j)),
                  pl.BlockSpec((1, tn), lambda j, i: (0, j))],
        out_specs=[pl.BlockSpec((tm, tn), lambda j, i: (i, j)) for _ in out_dtypes],
        out_shape=[jax.ShapeDtypeStruct((m, n), d) for d in out_dtypes],
        compiler_params=_cparams("parallel", "arbitrary"),
        name="proj_" + (act or "id"),
    )(x, w, bias)


def _merge_kernel(oa_ref, ob_ref, oc_ref, od_ref, g0_ref, g1_ref, g2_ref, g3_ref, wb_ref, o_ref):
    acc = None
    for br, gr, k in ((oa_ref, g0_ref, 0), (ob_ref, g1_ref, 1), (oc_ref, g2_ref, 2), (od_ref, g3_ref, 3)):
        t = gr[...].astype(F32) * _dot(br[...], wb_ref[k])
        acc = t if acc is None else acc + t
    o_ref[...] = acc.astype(o_ref.dtype)


def _merge(branches, gates, w_branch):
    m = branches[0].shape[0]
    tm = min(m, 512)
    tn = 512
    nb = D_MODEL // tn
    gate_specs = [pl.BlockSpec((tm, tn), functools.partial(lambda j, i, k: (i, k * nb + j), k=k))
                  for k in range(N_BRANCH)]
    return pl.pallas_call(
        _merge_kernel,
        grid=(nb, m // tm),
        in_specs=[pl.BlockSpec((tm, MIX_W), lambda j, i: (i, 0)) for _ in range(N_BRANCH)] + gate_specs
                 + [pl.BlockSpec((N_BRANCH, MIX_W, tn), lambda j, i: (0, 0, j))],
        out_specs=pl.BlockSpec((tm, tn), lambda j, i: (i, j)),
        out_shape=jax.ShapeDtypeStruct((m, D_MODEL), BF16),
        compiler_params=_cparams("parallel", "arbitrary"),
        name="merge",
    )(*branches, gates, gates, gates, gates, w_branch)


def _res_ln_kernel(a_ref, w_ref, x_ref, g_ref, b_ref, y_ref, ybf_ref, acc_ref):
    kk = pl.program_id(1)

    @pl.when(kk == 0)
    def _():
        acc_ref[...] = jnp.zeros_like(acc_ref)

    acc_ref[...] += _dot(a_ref[...], w_ref[...])

    @pl.when(kk == pl.num_programs(1) - 1)
    def _():
        y = _layer_norm(DEEPNORM_ALPHA * x_ref[...] + acc_ref[...], g_ref[...], b_ref[...])
        y_ref[...] = y
        ybf_ref[...] = y.astype(BF16)


def _matmul_res_ln(a, w, x, g, b, *, tk, tm):
    m, k = a.shape
    n = w.shape[1]
    tm = min(m, tm)
    return pl.pallas_call(
        _res_ln_kernel,
        grid=(m // tm, k // tk),
        in_specs=[pl.BlockSpec((tm, tk), lambda i, kk: (i, kk)),
                  pl.BlockSpec((tk, n), lambda i, kk: (kk, 0)),
                  pl.BlockSpec((tm, n), lambda i, kk: (i, 0)),
                  pl.BlockSpec((1, n), lambda i, kk: (0, 0)),
                  pl.BlockSpec((1, n), lambda i, kk: (0, 0))],
        out_specs=[pl.BlockSpec((tm, n), lambda i, kk: (i, 0)),
                   pl.BlockSpec((tm, n), lambda i, kk: (i, 0))],
        out_shape=[jax.ShapeDtypeStruct((m, n), F32), jax.ShapeDtypeStruct((m, n), BF16)],
        scratch_shapes=[pltpu.VMEM((tm, n), F32)],
        compiler_params=_cparams("parallel", "arbitrary"),
        name="matmul_res_ln",
    )(a, w, x, g.reshape(1, n), b.reshape(1, n))


def _ffn_up_kernel(x_ref, wg_ref, wu_ref, o_ref):
    x = x_ref[...]
    o_ref[...] = (jax.nn.silu(_dot(x, wg_ref[...])) * _dot(x, wu_ref[...])).astype(o_ref.dtype)


def _ffn_up(x, w_up):
    m, k = x.shape
    tm = min(m, 512)
    tn = 512
    nb = D_FF // tn
    return pl.pallas_call(
        _ffn_up_kernel,
        grid=(nb, m // tm),
        in_specs=[pl.BlockSpec((tm, k), lambda j, i: (i, 0)),
                  pl.BlockSpec((k, tn), lambda j, i: (0, j)),
                  pl.BlockSpec((k, tn), lambda j, i: (0, nb + j))],
        out_specs=pl.BlockSpec((tm, tn), lambda j, i: (i, j)),
        out_shape=jax.ShapeDtypeStruct((m, D_FF), BF16),
        compiler_params=_cparams("parallel", "arbitrary"),
        name="ffn_up",
    )(x, w_up, w_up)


def _cumsum_lanes_kernel(x_ref, o_ref):
    x = x_ref[...]
    n = x.shape[-1]
    lane = lax.broadcasted_iota(jnp.int32, x.shape, 1)
    sh = 1
    while sh < n:
        x = x + jnp.where(lane >= sh, pltpu.roll(x, sh, axis=1), 0.0)
        sh *= 2
    o_ref[...] = x


def _cumsum_lanes(x):
    return pl.pallas_call(_cumsum_lanes_kernel, out_shape=jax.ShapeDtypeStruct(x.shape, F32),
                          name="cumsum_lanes")(x)


def _flash_kernel(q_ref, k_ref, v_ref, cq_ref, ck_ref, o_ref, m_sc, l_sc, acc_sc, *, tq, tk, scale):
    qi = pl.program_id(2)
    ki = pl.program_id(3)

    @pl.when(ki == 0)
    def _():
        m_sc[...] = jnp.full_like(m_sc, -jnp.inf)
        l_sc[...] = jnp.zeros_like(l_sc)
        acc_sc[...] = jnp.zeros_like(acc_sc)

    @pl.when(ki * tk <= qi * tq + (tq - 1))
    def _():
        s = _dot_nt(q_ref[...], k_ref[...]) * scale
        s = s + cq_ref[0, 0] - ck_ref[0, 0]
        q_pos = qi * tq + lax.broadcasted_iota(jnp.int32, (tq, tk), 0)
        k_pos = ki * tk + lax.broadcasted_iota(jnp.int32, (tq, tk), 1)
        s = jnp.where(k_pos <= q_pos, s, -jnp.inf)
        m_prev = m_sc[...]
        m_new = jnp.maximum(m_prev, jnp.max(s, axis=-1, keepdims=True))
        alpha = jnp.exp(m_prev - m_new)
        p = jnp.exp(s - m_new)
        l_sc[...] = alpha * l_sc[...] + jnp.sum(p, axis=-1, keepdims=True)
        acc_sc[...] = alpha * acc_sc[...] + _dot(p.astype(BF16), v_ref[...])
        m_sc[...] = m_new

    @pl.when(ki == pl.num_programs(3) - 1)
    def _():
        o_ref[...] = (acc_sc[...] / l_sc[...]).astype(o_ref.dtype)


def _fox_prompt(qkv, c, bsz, seq):
    tq = tk = min(seq, 512)
    nq = seq // tq
    nk = seq // tk
    cq = c.reshape(bsz, FOX_H, seq, 1)
    ck = c.reshape(bsz, FOX_H, 1, seq)

    def kv_blk(qi, ki):
        return jnp.minimum(ki, (qi * tq + tq - 1) // tk)

    return pl.pallas_call(
        functools.partial(_flash_kernel, tq=tq, tk=tk, scale=FOX_HD ** -0.5),
        grid=(bsz, FOX_H, nq, nk),
        in_specs=[pl.BlockSpec((tq, FOX_HD), lambda b, h, qi, ki: (b * nq + qi, h)),
                  pl.BlockSpec((tk, FOX_HD), lambda b, h, qi, ki: (b * nk + kv_blk(qi, ki), FOX_H + h)),
                  pl.BlockSpec((tk, FOX_HD), lambda b, h, qi, ki: (b * nk + kv_blk(qi, ki), 2 * FOX_H + h)),
                  pl.BlockSpec((1, 1, tq, 1), lambda b, h, qi, ki: (b, h, qi, 0)),
                  pl.BlockSpec((1, 1, 1, tk), lambda b, h, qi, ki: (b, h, 0, kv_blk(qi, ki)))],
        out_specs=pl.BlockSpec((tq, FOX_HD), lambda b, h, qi, ki: (b * nq + qi, h)),
        out_shape=jax.ShapeDtypeStruct((bsz * seq, MIX_W), BF16),
        scratch_shapes=[pltpu.VMEM((tq, 1), F32), pltpu.VMEM((tq, 1), F32), pltpu.VMEM((tq, FOX_HD), F32)],
        compiler_params=_cparams("parallel", "parallel", "parallel", "arbitrary"),
        name="fox_prompt",
    )(qkv, qkv, qkv, cq, ck)


def _fox_decode_kernel(pt_ref, q_ref, k_ref, v_ref, lf_ref, msuf_ref, cnew_ref, knew_ref, vnew_ref, o_ref,
                       m_sc, l_sc, acc_sc, carry_sc, *, scale, page):
    p = pl.program_id(1)
    hrow = lax.broadcasted_iota(jnp.int32, (BF16_ROWS, MIX_W), 0)
    lane = lax.broadcasted_iota(jnp.int32, (BF16_ROWS, MIX_W), 1)

    @pl.when(p == 0)
    def _():
        m_sc[...] = jnp.full_like(m_sc, -jnp.inf)
        l_sc[...] = jnp.zeros_like(l_sc)
        acc_sc[...] = jnp.zeros_like(acc_sc)
        carry_sc[...] = jnp.zeros_like(carry_sc)

    q = q_ref[0]
    s = _dot_nt(q, k_ref[0].astype(BF16)) * scale
    lfd = jnp.where((lane & (FOX_H - 1)) == hrow, lf_ref[0], 0.0)
    hi, mid, lo = _split3(lfd)
    msuf = msuf_ref[...]
    suffix = _dot(hi, msuf) + _dot(mid, msuf) + _dot(lo, msuf)
    carry = carry_sc[...]
    s = s + (suffix + carry + cnew_ref[0])
    carry_sc[...] = carry + jnp.sum(lfd, axis=-1, keepdims=True)

    m_prev = m_sc[...]
    m_new = jnp.maximum(m_prev, jnp.max(s, axis=-1, keepdims=True))
    alpha = jnp.exp(m_prev - m_new)
    pr = jnp.exp(s - m_new)
    l_sc[...] = alpha * l_sc[...] + jnp.sum(pr, axis=-1, keepdims=True)
    acc_sc[...] = alpha * acc_sc[...] + _dot(pr.astype(BF16), v_ref[0].astype(BF16))
    m_sc[...] = m_new

    @pl.when(p == pl.num_programs(1) - 1)
    def _():
        kn = knew_ref[0].astype(BF16).astype(F32)
        vn = vnew_ref[0].astype(BF16).astype(F32)
        s_new = jnp.sum(q.astype(F32) * kn, axis=-1, keepdims=True) * scale
        m_prev = m_sc[...]
        m_fin = jnp.maximum(m_prev, s_new)
        alpha = jnp.exp(m_prev - m_fin)
        p_new = jnp.exp(s_new - m_fin)
        l_fin = alpha * l_sc[...] + p_new
        acc = alpha * acc_sc[...] + p_new.astype(BF16).astype(F32) * vn
        o = acc / l_fin
        o_ref[0] = jnp.sum(jnp.where((lane >> 7) == hrow, o, 0.0), axis=0, keepdims=True)


def _fox_decode(page_table, q, k_new, v_new, c_new, cache_k, cache_v, cache_lf):
    bsz, n_pages = page_table.shape
    page = cache_k.shape[1]
    hmask = (np.arange(MIX_W)[None, :] // FOX_HD) == np.arange(BF16_ROWS)[:, None]
    q_rows = jnp.where(hmask[None], q[:, None, :], 0.0).astype(BF16)
    c_rows = jnp.pad(c_new, ((0, 0), (0, BF16_ROWS - FOX_H)))[:, :, None]
    msuf = jnp.asarray((np.arange(page * FOX_H)[:, None] // FOX_H) > np.arange(page)[None, :], BF16)
    pt = page_table.reshape(-1)

    def pg(b, p, pt_ref):
        return pt_ref[b * n_pages + (n_pages - 1 - p)]

    grid_spec = pltpu.PrefetchScalarGridSpec(
        num_scalar_prefetch=1,
        grid=(bsz, n_pages),
        in_specs=[pl.BlockSpec((1, BF16_ROWS, MIX_W), lambda b, p, pt_ref: (b, 0, 0)),
                  pl.BlockSpec((1, page, MIX_W), lambda b, p, pt_ref: (pg(b, p, pt_ref), 0, 0)),
                  pl.BlockSpec((1, page, MIX_W), lambda b, p, pt_ref: (pg(b, p, pt_ref), 0, 0)),
                  pl.BlockSpec((1, 1, page * FOX_H), lambda b, p, pt_ref: (pg(b, p, pt_ref), 0, 0)),
                  pl.BlockSpec((page * FOX_H, page), lambda b, p, pt_ref: (0, 0)),
                  pl.BlockSpec((1, BF16_ROWS, 1), lambda b, p, pt_ref: (b, 0, 0)),
                  pl.BlockSpec((1, 1, MIX_W), lambda b, p, pt_ref: (b, 0, 0)),
                  pl.BlockSpec((1, 1, MIX_W), lambda b, p, pt_ref: (b, 0, 0))],
        out_specs=pl.BlockSpec((1, 1, MIX_W), lambda b, p, pt_ref: (b, 0, 0)),
        scratch_shapes=[pltpu.VMEM((BF16_ROWS, 1), F32), pltpu.VMEM((BF16_ROWS, 1), F32),
                        pltpu.VMEM((BF16_ROWS, MIX_W), F32), pltpu.VMEM((BF16_ROWS, 1), F32)],
    )
    out = pl.pallas_call(
        functools.partial(_fox_decode_kernel, scale=FOX_HD ** -0.5, page=page),
        grid_spec=grid_spec,
        out_shape=jax.ShapeDtypeStruct((bsz, 1, MIX_W), F32),
        compiler_params=_cparams("parallel", "arbitrary"),
        name="fox_decode",
    )(pt, q_rows, cache_k, cache_v, cache_lf, msuf, c_rows, k_new[:, None, :], v_new[:, None, :])
    return out.reshape(bsz, MIX_W)


def _s5_kernel(u_ref, bblk_ref, cblk_ref, are_ref, aim_ref, d_ref, wglu_ref, bglu_ref, h0re_ref, h0im_ref,
               o_ref, hre_ref, him_ref, hs_sc, *, tc, bk, slab):
    c = pl.program_id(0)

    @pl.when(c == 0)
    def _():
        hre_ref[...] = h0re_ref[...]
        him_ref[...] = h0im_ref[...]

    u = u_ref[...]
    hs_sc[...] = _dot(u.astype(BF16), bblk_ref[...])
    for s0 in range(0, S5_STATE, slab):
        re_sl = slice(s0, s0 + slab)
        im_sl = slice(S5_STATE + s0, S5_STATE + s0 + slab)
        ar = jnp.broadcast_to(are_ref[:, re_sl], (bk, slab))
        ai = jnp.broadcast_to(aim_ref[:, re_sl], (bk, slab))

        def step(t, carry, re_sl=re_sl, im_sl=im_sl, ar=ar, ai=ai):
            hr, hi = carry
            rows = pl.ds(pl.multiple_of(t * bk, bk), bk)
            nr = ar * hr - ai * hi + hs_sc[rows, re_sl]
            ni = ar * hi + ai * hr + hs_sc[rows, im_sl]
            hs_sc[rows, re_sl] = nr
            hs_sc[rows, im_sl] = ni
            return nr, ni

        hr, hi = lax.fori_loop(0, tc, step, (hre_ref[:, re_sl], him_ref[:, re_sl]))
        hre_ref[:, re_sl] = hr
        him_ref[:, re_sl] = hi
    y = _dot(hs_sc[...].astype(BF16), cblk_ref[...]) + d_ref[...] * u
    g = jax.nn.gelu(y)
    gate = jax.nn.sigmoid(_dot(g.astype(BF16), wglu_ref[...]) + bglu_ref[...])
    o_ref[...] = (g * gate).astype(o_ref.dtype)


def _s5_params(lam_re, lam_im, log_dt, b_re, b_im, c_re, c_im):
    dt = jnp.exp(log_dt)[:, None]
    mag = jnp.exp(lam_re * dt)
    ab_re, ab_im = mag * jnp.cos(lam_im * dt), mag * jnp.sin(lam_im * dt)
    den = lam_re * lam_re + lam_im * lam_im
    nr, ni = ab_re - 1.0, ab_im
    coef_re = (nr * lam_re + ni * lam_im) / den
    coef_im = (ni * lam_re - nr * lam_im) / den
    bb_re = coef_re[..., None] * b_re - coef_im[..., None] * b_im
    bb_im = coef_re[..., None] * b_im + coef_im[..., None] * b_re
    eye = jnp.eye(S5_G, dtype=F32)
    blk_b = lambda t: jnp.einsum("gph,gk->ghkp", t, eye).reshape(MIX_W, S5_STATE)
    blk_c = lambda t: jnp.einsum("ghp,gk->gpkh", t, eye).reshape(S5_STATE, MIX_W)
    bblk = jnp.concatenate([blk_b(bb_re), blk_b(bb_im)], axis=1).astype(BF16)
    cblk = jnp.concatenate([blk_c(c_re), blk_c(-c_im)], axis=0).astype(BF16)
    return ab_re.reshape(1, S5_STATE), ab_im.reshape(1, S5_STATE), bblk, cblk


def _s5(u_tb, params, d, w_glu, b_glu, h0_re, h0_im, *, bk, steps):
    a_re, a_im, bblk, cblk = params
    tc = min(steps, 64)
    rows = tc * bk
    const = lambda shape: pl.BlockSpec(shape, lambda c: (0,) * len(shape))
    return pl.pallas_call(
        functools.partial(_s5_kernel, tc=tc, bk=bk, slab=512),
        grid=(steps // tc,),
        in_specs=[pl.BlockSpec((rows, MIX_W), lambda c: (c, 0)),
                  const((MIX_W, 2 * S5_STATE)), const((2 * S5_STATE, MIX_W)),
                  const((1, S5_STATE)), const((1, S5_STATE)), const((1, MIX_W)),
                  const((MIX_W, MIX_W)), const((1, MIX_W)),
                  const((bk, S5_STATE)), const((bk, S5_STATE))],
        out_specs=[pl.BlockSpec((rows, MIX_W), lambda c: (c, 0)),
                   const((bk, S5_STATE)), const((bk, S5_STATE))],
        out_shape=[jax.ShapeDtypeStruct((steps * bk, MIX_W), BF16),
                   jax.ShapeDtypeStruct((bk, S5_STATE), F32), jax.ShapeDtypeStruct((bk, S5_STATE), F32)],
        scratch_shapes=[pltpu.VMEM((rows, 2 * S5_STATE), F32)],
        compiler_params=_cparams("arbitrary"),
        name="s5",
    )(u_tb, bblk, cblk, a_re, a_im, d.reshape(1, MIX_W), w_glu, b_glu.reshape(1, MIX_W), h0_re, h0_im)


def _hgrn_kernel(q_ref, f_ref, i_ref, g_ref, lb_ref, ng_ref, tri_ref, s0_ref, o_ref, st_ref, *, c, sub, valid):
    n = pl.program_id(2)

    @pl.when(n == 0)
    def _():
        st_ref[...] = s0_ref[...]

    q = q_ref[...]
    v = i_ref[...]
    lb = lb_ref[...]
    la = jnp.log(lb)
    lbb = jnp.log1p(-lb) + _log_sigmoid(f_ref[...])
    lf = jnp.maximum(la, lbb) + jnp.log1p(jnp.exp(-jnp.abs(la - lbb)))
    if valid < c:
        lf = jnp.where(lax.broadcasted_iota(jnp.int32, lf.shape, 0) < valid, lf, 0.0)
    kt = 1.0 - jnp.exp(lf)
    tri = tri_ref[...]
    hi, mid, lo = _split3(lf)
    bc = _dot(tri, hi) + _dot(tri, mid) + _dot(tri, lo)
    st = st_ref[0, 0]
    o = _dot_nt((q * jnp.exp(bc)).astype(BF16), st.astype(BF16))
    nb = c // sub
    rows = lax.broadcasted_iota(jnp.int32, (sub, HG_DK), 0)
    blocks = []
    for i in range(nb):
        lo_r, hi_r = i * sub, (i + 1) * sub
        bci = bc[lo_r:hi_r]
        qi = q[lo_r:hi_r]
        oi = o[lo_r:hi_r]
        if i > 0:
            e = bc[lo_r - 1:lo_r]
            qs = (qi * jnp.exp(bci - e)).astype(BF16)
            ks = (kt[:lo_r] * jnp.exp(e - bc[:lo_r])).astype(BF16)
            att = _dot_nt(qs, ks)
            oi = oi + _dot(att.astype(BF16), v[:lo_r].astype(BF16))
        for s in range(sub):
            r = lo_r + s
            dec = jnp.exp(jnp.where(rows >= s, bci - bc[r:r + 1], -jnp.inf))
            a = jnp.sum(qi * dec * kt[r:r + 1], axis=-1, keepdims=True)
            oi = oi + a * v[r:r + 1]
        blocks.append(oi)
    o = jnp.concatenate(blocks, axis=0) if nb > 1 else blocks[0]
    bl = bc[c - 1:c]
    st_ref[0, 0] = jnp.exp(bl) * st + _dot_tn(v.astype(BF16), (kt * jnp.exp(bl - bc)).astype(BF16))
    o = o * lax.rsqrt(jnp.mean(o * o, axis=-1, keepdims=True) + RMS_EPS) * ng_ref[...]
    o_ref[...] = (o * jax.nn.silu(g_ref[...])).astype(o_ref.dtype)


def _hgrn(z, lb, norm_g, s0_t, *, bsz, seq_pad, valid):
    c = min(HG_CHUNK, seq_pad)
    sub = min(HG_SUB, c)
    nc = seq_pad // c
    tri = jnp.asarray(np.tril(np.ones((c, c))), BF16)
    col = lambda k: pl.BlockSpec((c, HG_DK), functools.partial(lambda b, h, n, k: (b * nc + n, k * HG_H + h), k=k))
    return pl.pallas_call(
        functools.partial(_hgrn_kernel, c=c, sub=sub, valid=valid),
        grid=(bsz, HG_H, nc),
        in_specs=[col(0), col(1), col(2), col(3),
                  pl.BlockSpec((1, HG_DK), lambda b, h, n: (0, h)),
                  pl.BlockSpec((1, HG_DV), lambda b, h, n: (0, h)),
                  pl.BlockSpec((c, c), lambda b, h, n: (0, 0)),
                  pl.BlockSpec((1, 1, HG_DV, HG_DK), lambda b, h, n: (b, h, 0, 0))],
        out_specs=[pl.BlockSpec((c, HG_DV), lambda b, h, n: (b * nc + n, h)),
                   pl.BlockSpec((1, 1, HG_DV, HG_DK), lambda b, h, n: (b, h, 0, 0))],
        out_shape=[jax.ShapeDtypeStruct((bsz * seq_pad, MIX_W), BF16),
                   jax.ShapeDtypeStruct((bsz, HG_H, HG_DV, HG_DK), F32)],
        compiler_params=_cparams("parallel", "parallel", "arbitrary"),
        name="hgrn2",
    )(z, z, z, z, lb.reshape(1, MIX_W), norm_g.reshape(1, MIX_W), tri, s0_t)


def _gmlp_kernel(u_ref, v_ref, lg_ref, lbias_ref, ws_ref, bs_ref, o_ref):
    vn = _layer_norm(v_ref[...], lg_ref[...], lbias_ref[...]).astype(BF16)
    u = u_ref[...]
    r = lax.broadcasted_iota(jnp.int32, (GM_CHUNK, GM_CHUNK), 0)
    cc = lax.broadcasted_iota(jnp.int32, (GM_CHUNK, GM_CHUNK), 1)
    for g in range(GM_GROUPS):
        sl = slice(g * GM_GW, (g + 1) * GM_GW)
        ws = jnp.where(r >= cc, ws_ref[g], 0.0).astype(BF16)
        mixed = _dot(ws, vn[:, sl]) + bs_ref[g]
        o_ref[:, sl] = (u[:, sl] * mixed).astype(o_ref.dtype)


def _gmlp(uv, ln_g, ln_b, w_s, b_s):
    m = uv.shape[0]
    return pl.pallas_call(
        _gmlp_kernel,
        grid=(m // GM_CHUNK,),
        in_specs=[pl.BlockSpec((GM_CHUNK, MIX_W), lambda i: (i, 0)),
                  pl.BlockSpec((GM_CHUNK, MIX_W), lambda i: (i, 1)),
                  pl.BlockSpec((1, MIX_W), lambda i: (0, 0)),
                  pl.BlockSpec((1, MIX_W), lambda i: (0, 0)),
                  pl.BlockSpec((GM_GROUPS, GM_CHUNK, GM_CHUNK), lambda i: (0, 0, 0)),
                  pl.BlockSpec((GM_GROUPS, GM_CHUNK, 1), lambda i: (0, 0, 0))],
        out_specs=pl.BlockSpec((GM_CHUNK, MIX_W), lambda i: (i, 0)),
        out_shape=jax.ShapeDtypeStruct((m, MIX_W), BF16),
        compiler_params=_cparams("parallel"),
        name="gmlp",
    )(uv, uv, ln_g.reshape(1, MIX_W), ln_b.reshape(1, MIX_W), w_s, b_s[:, :, None])


def _gmlp_first_kernel(u_ref, v_ref, lg_ref, lbias_ref, w00_ref, b0_ref, o_ref, vn_ref):
    vn = _layer_norm(v_ref[...], lg_ref[...], lbias_ref[...])
    vn_ref[...] = vn
    o_ref[...] = (u_ref[...] * (vn * w00_ref[...] + b0_ref[...])).astype(o_ref.dtype)


def _gmlp_first(uv, ln_g, ln_b, w_s, b_s):
    m = uv.shape[0]
    w00 = jnp.repeat(w_s[:, 0, 0], GM_GW).reshape(1, MIX_W)
    b0 = jnp.repeat(b_s[:, 0], GM_GW).reshape(1, MIX_W)
    vec = pl.BlockSpec((1, MIX_W), lambda i: (0, 0))
    return pl.pallas_call(
        _gmlp_first_kernel,
        grid=(1,),
        in_specs=[pl.BlockSpec((m, MIX_W), lambda i: (0, 0)), pl.BlockSpec((m, MIX_W), lambda i: (0, 1)),
                  vec, vec, vec, vec],
        out_specs=[pl.BlockSpec((m, MIX_W), lambda i: (0, 0)), pl.BlockSpec((m, MIX_W), lambda i: (0, 0))],
        out_shape=[jax.ShapeDtypeStruct((m, MIX_W), BF16), jax.ShapeDtypeStruct((m, MIX_W), F32)],
        name="gmlp_first",
    )(uv, uv, ln_g.reshape(1, MIX_W), ln_b.reshape(1, MIX_W), w00, b0)


def _layer_weights(l, w_in, fox_bf, s5_lambda_re, s5_lambda_im, s5_log_dt, s5_b_re, s5_b_im, s5_c_re, s5_c_im,
                   s5_w_glu, hgrn_lb, w_branch, w_out, w_ffn_up, w_ffn_down):
    wi = w_in[l]
    o_ff = 3 * MIX_W
    o_s5 = o_ff + FOX_H
    o_hg = o_s5 + MIX_W
    o_gm = o_hg + 4 * MIX_W
    o_gt = o_gm + 2 * MIX_W
    lb_all = jnp.cumsum(jax.nn.softmax(hgrn_lb.astype(F32), axis=0), axis=0)
    return dict(
        w_qkv=wi[:, :o_ff].astype(BF16),
        w_ff=jnp.pad(wi[:, o_ff:o_s5], ((0, 0), (0, LANES - FOX_H))).astype(BF16),
        bf=jnp.pad(fox_bf[l], (0, LANES - FOX_H)).reshape(1, LANES),
        w_s5=wi[:, o_s5:o_hg].astype(BF16),
        w_hg=wi[:, o_hg:o_gm].astype(BF16),
        w_gm=wi[:, o_gm:o_gt].astype(BF16),
        w_gate=wi[:, o_gt:].astype(BF16),
        s5=_s5_params(s5_lambda_re[l], s5_lambda_im[l], s5_log_dt[l], s5_b_re[l], s5_b_im[l],
                      s5_c_re[l], s5_c_im[l]),
        w_glu=s5_w_glu[l].astype(BF16),
        lb=lb_all[l] - lb_all[0],
        w_branch=w_branch[l].astype(BF16),
        w_out=w_out[l].astype(BF16),
        w_up=w_ffn_up[l].astype(BF16),
        w_down=w_ffn_down[l].astype(BF16),
    )


def _trunk_layer(x, x_bf, l, lw, P, *, bsz, seq, fox_fn, s5_h0, hg_s0):
    m = bsz * seq
    qkv_f32, qkv_bf = _proj(x_bf, lw["w_qkv"], out_dtypes=(F32, BF16))
    logf = _proj(x_bf, lw["w_ff"], act="log_sigmoid_bias", bias=lw["bf"])[0][:, :FOX_H]
    o_a = fox_fn(qkv_f32, qkv_bf, logf)

    su = _proj(x_bf, lw["w_s5"])[0]
    su_tb = su.reshape(bsz, seq, MIX_W).transpose(1, 0, 2).reshape(m, MIX_W)
    o_b_tb, s5_re, s5_im = _s5(su_tb, lw["s5"], P["s5_d"][l], lw["w_glu"], P["s5_b_glu"][l],
                               s5_h0[0].reshape(bsz, S5_STATE), s5_h0[1].reshape(bsz, S5_STATE),
                               bk=bsz, steps=seq)
    o_b = o_b_tb.reshape(seq, bsz, MIX_W).transpose(1, 0, 2).reshape(m, MIX_W)

    zh = _proj(x_bf, lw["w_hg"])[0]
    seq_pad = seq if seq % HG_CHUNK == 0 else -(-seq // BF16_ROWS) * BF16_ROWS
    if seq_pad != seq:
        zh = jnp.pad(zh.reshape(bsz, seq, 4 * MIX_W), ((0, 0), (0, seq_pad - seq), (0, 0))).reshape(-1, 4 * MIX_W)
    o_c, hg_t = _hgrn(zh, lw["lb"], P["hgrn_norm_g"][l], hg_s0.transpose(0, 1, 3, 2),
                      bsz=bsz, seq_pad=seq_pad, valid=min(seq, HG_CHUNK))
    if seq_pad != seq:
        o_c = o_c.reshape(bsz, seq_pad, MIX_W)[:, :seq].reshape(m, MIX_W)
    hg_state = hg_t.transpose(0, 1, 3, 2)

    uv = _proj(x_bf, lw["w_gm"], act="gelu")[0]
    if seq == 1:
        o_d, v_rows = _gmlp_first(uv, P["gmlp_ln_g"][l], P["gmlp_ln_b"][l], P["gmlp_w_s"][l], P["gmlp_b_s"][l])
    else:
        o_d = _gmlp(uv, P["gmlp_ln_g"][l], P["gmlp_ln_b"][l], P["gmlp_w_s"][l], P["gmlp_b_s"][l])
        v_rows = None

    gates = _proj(x_bf, lw["w_gate"], act="sigmoid", tn=1024)[0]
    merged = _merge((o_a, o_b, o_c, o_d), gates, lw["w_branch"])
    x1, x1_bf = _matmul_res_ln(merged, lw["w_out"], x, P["ln1_g"][l], P["ln1_b"][l], tk=D_MODEL, tm=256)
    hff = _ffn_up(x1_bf, lw["w_up"])
    x2, x2_bf = _matmul_res_ln(hff, lw["w_down"], x1, P["ln2_g"][l], P["ln2_b"][l], tk=512, tm=512)

    fk = qkv_f32[:, MIX_W:2 * MIX_W].reshape(bsz, seq, FOX_H, FOX_HD)
    fv = qkv_f32[:, 2 * MIX_W:].reshape(bsz, seq, FOX_H, FOX_HD)
    state = (fk, fv, logf.reshape(bsz, seq, FOX_H), s5_re.reshape(bsz, S5_G, S5_P),
             s5_im.reshape(bsz, S5_G, S5_P), hg_state, v_rows)
    return x2, x2_bf, state


def kernel(x_prompt, x_sample, cache_k, cache_v, cache_logf, page_table, state_s5_re, state_s5_im, state_hgrn,
           w_in, fox_bf, s5_lambda_re, s5_lambda_im, s5_log_dt, s5_b_re, s5_b_im, s5_c_re, s5_c_im, s5_d,
           s5_w_glu, s5_b_glu, hgrn_lb, hgrn_norm_g, gmlp_ln_g, gmlp_ln_b, gmlp_w_s, gmlp_b_s, w_branch, w_out,
           ln1_g, ln1_b, w_ffn_up, w_ffn_down, ln2_g, ln2_b):
    b_p, l_p, _ = x_prompt.shape
    b_s, l_s, _ = x_sample.shape
    assert l_s == 1, "the sample group decodes one token per sequence"
    n_phys, page = cache_k.shape[1], cache_k.shape[2]
    P = dict(s5_d=s5_d, s5_b_glu=s5_b_glu, hgrn_norm_g=hgrn_norm_g, gmlp_ln_g=gmlp_ln_g, gmlp_ln_b=gmlp_ln_b,
             gmlp_w_s=gmlp_w_s, gmlp_b_s=gmlp_b_s, ln1_g=ln1_g, ln1_b=ln1_b, ln2_g=ln2_g, ln2_b=ln2_b)

    xp = x_prompt.reshape(b_p * l_p, D_MODEL)
    xs = x_sample.reshape(b_s * l_s, D_MODEL)
    xp_bf, xs_bf = xp.astype(BF16), xs.astype(BF16)
    s5_zero = jnp.zeros((b_p, S5_G, S5_P), F32)
    hg_zero = jnp.zeros((b_p, HG_H, HG_DK, HG_DV), F32)
    p_states, s_states = [], []
    for l in range(DEPTH):
        lw = _layer_weights(l, w_in, fox_bf, s5_lambda_re, s5_lambda_im, s5_log_dt, s5_b_re, s5_b_im,
                            s5_c_re, s5_c_im, s5_w_glu, hgrn_lb, w_branch, w_out, w_ffn_up, w_ffn_down)

        def fox_p(qkv_f32, qkv_bf, logf):
            c = _cumsum_lanes(logf.reshape(b_p, l_p, FOX_H).transpose(0, 2, 1).reshape(b_p * FOX_H, l_p))
            return _fox_prompt(qkv_bf, c.reshape(b_p, FOX_H, l_p), b_p, l_p)

        def fox_s(qkv_f32, qkv_bf, logf, l=l):
            o = _fox_decode(page_table, qkv_f32[:, :MIX_W], qkv_f32[:, MIX_W:2 * MIX_W], qkv_f32[:, 2 * MIX_W:],
                            logf, cache_k[l].reshape(n_phys, page, MIX_W), cache_v[l].reshape(n_phys, page, MIX_W),
                            cache_logf[l].reshape(n_phys, 1, page * FOX_H))
            return o.astype(BF16)

        xp, xp_bf, sp = _trunk_layer(xp, xp_bf, l, lw, P, bsz=b_p, seq=l_p, fox_fn=fox_p,
                                     s5_h0=(s5_zero, s5_zero), hg_s0=hg_zero)
        xs, xs_bf, ss = _trunk_layer(xs, xs_bf, l, lw, P, bsz=b_s, seq=l_s, fox_fn=fox_s,
                                     s5_h0=(state_s5_re[l], state_s5_im[l]), hg_s0=state_hgrn[l])
        p_states.append(sp)
        s_states.append(ss)

    stack = lambda states, i: jnp.stack([s[i] for s in states])
    return (xp.reshape(b_p, l_p, D_MODEL), xs.reshape(b_s, l_s, D_MODEL),
            stack(p_states, 0), stack(p_states, 1), stack(p_states, 2),
            stack(s_states, 0), stack(s_states, 1), stack(s_states, 2),
            stack(p_states, 3), stack(p_states, 4), stack(s_states, 3), stack(s_states, 4),
            stack(p_states, 5), stack(s_states, 5),
            jnp.stack([s[6].reshape(b_s, l_s, MIX_W) for s in s_states]))
```

```python
import functools
import math

import numpy as np
import jax
import jax.numpy as jnp
from jax import lax
from jax.experimental import pallas as pl
from jax.experimental.pallas import tpu as pltpu

F32 = jnp.float32
BF16 = jnp.bfloat16

D_MODEL = 2048
DEPTH = 2
N_BRANCH = 4
MIX_W = D_MODEL // N_BRANCH
FOX_HD = 128
FOX_H = MIX_W // FOX_HD
S5_GROUP = 16
S5_G = MIX_W // S5_GROUP
S5_P = 64
S5_STATE = S5_G * S5_P
HG_DK = 128
HG_DV = 128
HG_H = MIX_W // HG_DV
HG_CHUNK = 64
HG_SUB = 16
GM_CHUNK = 128
GM_GROUPS = 4
GM_GW = MIX_W // GM_GROUPS
D_FF = ((8 * D_MODEL // 3 + 255) // 256) * 256
DEEPNORM_ALPHA = (2 * DEPTH) ** 0.25
LN_EPS = 1e-5
RMS_EPS = 1e-6
LANES = 128
SUBLANES = 8
BF16_ROWS = 16
FOX_DECODE_PAGES = 8
Z_HG = MIX_W
Z_GM = Z_HG + 4 * MIX_W
Z_GATE = Z_GM + 2 * MIX_W
VMEM_LIMIT = 48 * 1024 * 1024


def _cparams(*sem):
    return pltpu.CompilerParams(dimension_semantics=sem, vmem_limit_bytes=VMEM_LIMIT)


def _log_sigmoid(x):
    return jnp.minimum(x, 0.0) - jnp.log1p(jnp.exp(-jnp.abs(x)))


def _layer_norm(x, g, b):
    mu = jnp.mean(x, axis=-1, keepdims=True)
    xc = x - mu
    var = jnp.mean(xc * xc, axis=-1, keepdims=True)
    return xc * lax.rsqrt(var + LN_EPS) * g + b


def _split3(x):
    hi = x.astype(BF16)
    r1 = x - hi.astype(F32)
    mid = r1.astype(BF16)
    lo = (r1 - mid.astype(F32)).astype(BF16)
    return hi, mid, lo


def _dot(a, b):
    return jnp.dot(a, b, preferred_element_type=F32)


def _dot_nt(a, b):
    return lax.dot_general(a, b, (((1,), (1,)), ((), ())), preferred_element_type=F32)


def _dot_tn(a, b):
    return lax.dot_general(a, b, (((0,), (0,)), ((), ())), preferred_element_type=F32)


def _cast_cols_kernel(*refs, shift):
    o_ref = refs[-1]
    a = refs[0][...]
    if shift:
        a = jnp.concatenate([a, refs[1][...]], axis=1)[:, shift:shift + LANES]
    o_ref[...] = a.astype(o_ref.dtype)


def _cast_cols(w, l, col0, n):
    k = w.shape[1]
    shift = col0 % LANES
    j0 = col0 // LANES
    in_specs = [pl.BlockSpec((None, k, LANES), lambda j: (l, 0, j0 + j))]
    if shift:
        in_specs.append(pl.BlockSpec((None, k, LANES), lambda j: (l, 0, j0 + j + 1)))
    return pl.pallas_call(
        functools.partial(_cast_cols_kernel, shift=shift),
        grid=(n // LANES,),
        in_specs=in_specs,
        out_specs=pl.BlockSpec((k, LANES), lambda j: (0, j)),
        out_shape=jax.ShapeDtypeStruct((k, n), BF16),
        compiler_params=_cparams("parallel"),
        name="cast_cols",
    )(*([w] * len(in_specs)))


def _proj_kernel(x_ref, w_ref, b_ref, *out_refs, act):
    z = _dot(x_ref[...], w_ref[...])
    if act == "gelu":
        z = jax.nn.gelu(z)
    elif act == "sigmoid":
        z = jax.nn.sigmoid(z)
    elif act == "log_sigmoid_bias":
        z = _log_sigmoid(z + b_ref[...])
    for o in out_refs:
        o[...] = z.astype(o.dtype)


def _proj(x, w, *, col0=0, n=None, act=None, bias=None, out_dtypes=(F32,), tn=512):
    m, k = x.shape
    n = w.shape[1] if n is None else n
    tm = min(m, 512)
    tn = min(n, tn)
    assert col0 % tn == 0 and n % tn == 0
    j0 = col0 // tn
    if bias is None:
        bias = jnp.zeros((1, n), F32)
    return pl.pallas_call(
        functools.partial(_proj_kernel, act=act),
        grid=(n // tn, m // tm),
        in_specs=[pl.BlockSpec((tm, k), lambda j, i: (i, 0)),
                  pl.BlockSpec((k, tn), lambda j, i: (0, j0 + j)),
                  pl.BlockSpec((1, tn), lambda j, i: (0, j))],
        out_specs=[pl.BlockSpec((tm, tn), lambda j, i: (i, j)) for _ in out_dtypes],
        out_shape=[jax.ShapeDtypeStruct((m, n), d) for d in out_dtypes],
        compiler_params=_cparams("parallel", "arbitrary"),
        name="proj_" + (act or "id"),
    )(x, w, bias)


def _merge_kernel(x_ref, oa_ref, ob_ref, oc_ref, od_ref, g0_ref, g1_ref, g2_ref, g3_ref, wb_ref, o_ref):
    x = x_ref[...]
    acc = None
    for br, gr, k in ((oa_ref, g0_ref, 0), (ob_ref, g1_ref, 1), (oc_ref, g2_ref, 2), (od_ref, g3_ref, 3)):
        t = jax.nn.sigmoid(_dot(x, gr[...])) * _dot(br[...], wb_ref[k])
        acc = t if acc is None else acc + t
    o_ref[...] = acc.astype(o_ref.dtype)


def _merge(x, branches, w, gate_col0, w_branch):
    m, k_in = x.shape
    tm = min(m, 512)
    tn = 512
    nb = D_MODEL // tn
    assert gate_col0 % tn == 0
    j0 = gate_col0 // tn
    gate_specs = [pl.BlockSpec((k_in, tn), functools.partial(lambda j, i, k: (0, j0 + k * nb + j), k=k))
                  for k in range(N_BRANCH)]
    return pl.pallas_call(
        _merge_kernel,
        grid=(nb, m // tm),
        in_specs=[pl.BlockSpec((tm, k_in), lambda j, i: (i, 0))]
                 + [pl.BlockSpec((tm, MIX_W), lambda j, i: (i, 0)) for _ in range(N_BRANCH)] + gate_specs
                 + [pl.BlockSpec((N_BRANCH, MIX_W, tn), lambda j, i: (0, 0, j))],
        out_specs=pl.BlockSpec((tm, tn), lambda j, i: (i, j)),
        out_shape=jax.ShapeDtypeStruct((m, D_MODEL), BF16),
        compiler_params=_cparams("parallel", "arbitrary"),
        name="merge",
    )(x, *branches, w, w, w, w, w_branch)


def _res_ln_kernel(a_ref, w_ref, x_ref, g_ref, b_ref, y_ref, ybf_ref, acc_ref):
    kk = pl.program_id(1)

    @pl.when(kk == 0)
    def _():
        acc_ref[...] = jnp.zeros_like(acc_ref)

    acc_ref[...] += _dot(a_ref[...], w_ref[...])

    @pl.when(kk == pl.num_programs(1) - 1)
    def _():
        y = _layer_norm(DEEPNORM_ALPHA * x_ref[...] + acc_ref[...], g_ref[...], b_ref[...])
        y_ref[...] = y
        ybf_ref[...] = y.astype(BF16)


def _matmul_res_ln(a, w, x, g, b, *, tk, tm):
    m, k = a.shape
    n = w.shape[1]
    tm = min(m, tm)
    return pl.pallas_call(
        _res_ln_kernel,
        grid=(m // tm, k // tk),
        in_specs=[pl.BlockSpec((tm, tk), lambda i, kk: (i, kk)),
                  pl.BlockSpec((tk, n), lambda i, kk: (kk, 0)),
                  pl.BlockSpec((tm, n), lambda i, kk: (i, 0)),
                  pl.BlockSpec((1, n), lambda i, kk: (0, 0)),
                  pl.BlockSpec((1, n), lambda i, kk: (0, 0))],
        out_specs=[pl.BlockSpec((tm, n), lambda i, kk: (i, 0)),
                   pl.BlockSpec((tm, n), lambda i, kk: (i, 0))],
        out_shape=[jax.ShapeDtypeStruct((m, n), F32), jax.ShapeDtypeStruct((m, n), BF16)],
        scratch_shapes=[pltpu.VMEM((tm, n), F32)],
        compiler_params=_cparams("parallel", "arbitrary"),
        name="matmul_res_ln",
    )(a, w, x, g.reshape(1, n), b.reshape(1, n))


def _ffn_up_kernel(x_ref, wg_ref, wu_ref, o_ref):
    x = x_ref[...]
    o_ref[...] = (jax.nn.silu(_dot(x, wg_ref[...])) * _dot(x, wu_ref[...])).astype(o_ref.dtype)


def _ffn_up(x, w_up):
    m, k = x.shape
    tm = min(m, 512)
    tn = 512
    nb = D_FF // tn
    return pl.pallas_call(
        _ffn_up_kernel,
        grid=(nb, m // tm),
        in_specs=[pl.BlockSpec((tm, k), lambda j, i: (i, 0)),
                  pl.BlockSpec((k, tn), lambda j, i: (0, j)),
                  pl.BlockSpec((k, tn), lambda j, i: (0, nb + j))],
        out_specs=pl.BlockSpec((tm, tn), lambda j, i: (i, j)),
        out_shape=jax.ShapeDtypeStruct((m, D_FF), BF16),
        compiler_params=_cparams("parallel", "arbitrary"),
        name="ffn_up",
    )(x, w_up, w_up)


def _cumsum_lanes_kernel(x_ref, o_ref):
    x = x_ref[...]
    n = x.shape[-1]
    lane = lax.broadcasted_iota(jnp.int32, x.shape, 1)
    sh = 1
    while sh < n:
        x = x + jnp.where(lane >= sh, pltpu.roll(x, sh, axis=1), 0.0)
        sh *= 2
    o_ref[...] = x


def _cumsum_lanes(x):
    return pl.pallas_call(_cumsum_lanes_kernel, out_shape=jax.ShapeDtypeStruct(x.shape, F32),
                          name="cumsum_lanes")(x)


def _flash_kernel(q_ref, k_ref, v_ref, cq_ref, ck_ref, o_ref, m_sc, l_sc, acc_sc, *, tq, tk, scale):
    qi = pl.program_id(2)
    ki = pl.program_id(3)

    @pl.when(ki == 0)
    def _():
        m_sc[...] = jnp.full_like(m_sc, -jnp.inf)
        l_sc[...] = jnp.zeros_like(l_sc)
        acc_sc[...] = jnp.zeros_like(acc_sc)

    def update(masked):
        s = _dot_nt(q_ref[...], k_ref[...]) * scale
        s = s + cq_ref[0, 0] - ck_ref[0, 0]
        if masked:
            q_pos = lax.broadcasted_iota(jnp.int32, (tq, tk), 0)
            k_pos = lax.broadcasted_iota(jnp.int32, (tq, tk), 1)
            s = jnp.where(k_pos <= q_pos, s, -jnp.inf)
        m_prev = m_sc[...]
        m_new = jnp.maximum(m_prev, jnp.max(s, axis=-1, keepdims=True))
        alpha = jnp.exp(m_prev - m_new)
        p = jnp.exp(s - m_new)
        l_sc[...] = alpha * l_sc[...] + jnp.sum(p, axis=-1, keepdims=True)
        acc_sc[...] = alpha * acc_sc[...] + _dot(p.astype(BF16), v_ref[...])
        m_sc[...] = m_new

    pl.when(ki < qi)(functools.partial(update, False))
    pl.when(ki == qi)(functools.partial(update, True))

    @pl.when(ki == pl.num_programs(3) - 1)
    def _():
        o_ref[...] = (acc_sc[...] / l_sc[...]).astype(o_ref.dtype)


def _fox_prompt(qkv, c, bsz, seq):
    tq = tk = min(seq, 512)
    nq = seq // tq
    nk = seq // tk
    cq = c.reshape(bsz, FOX_H, seq, 1)
    ck = c.reshape(bsz, FOX_H, 1, seq)

    def kv_blk(qi, ki):
        return jnp.minimum(ki, (qi * tq + tq - 1) // tk)

    return pl.pallas_call(
        functools.partial(_flash_kernel, tq=tq, tk=tk, scale=FOX_HD ** -0.5),
        grid=(bsz, FOX_H, nq, nk),
        in_specs=[pl.BlockSpec((tq, FOX_HD), lambda b, h, qi, ki: (b * nq + qi, h)),
                  pl.BlockSpec((tk, FOX_HD), lambda b, h, qi, ki: (b * nk + kv_blk(qi, ki), FOX_H + h)),
                  pl.BlockSpec((tk, FOX_HD), lambda b, h, qi, ki: (b * nk + kv_blk(qi, ki), 2 * FOX_H + h)),
                  pl.BlockSpec((1, 1, tq, 1), lambda b, h, qi, ki: (b, h, qi, 0)),
                  pl.BlockSpec((1, 1, 1, tk), lambda b, h, qi, ki: (b, h, 0, kv_blk(qi, ki)))],
        out_specs=pl.BlockSpec((tq, FOX_HD), lambda b, h, qi, ki: (b * nq + qi, h)),
        out_shape=jax.ShapeDtypeStruct((bsz * seq, MIX_W), BF16),
        scratch_shapes=[pltpu.VMEM((tq, 1), F32), pltpu.VMEM((tq, 1), F32), pltpu.VMEM((tq, FOX_HD), F32)],
        compiler_params=_cparams("parallel", "parallel", "parallel", "arbitrary"),
        name="fox_prompt",
    )(qkv, qkv, qkv, cq, ck)


def _fox_decode_kernel(pt_ref, q_ref, cnew_ref, knew_ref, vnew_ref, *rest, scale, page, group):
    k_refs, v_refs, lf_refs = rest[:group], rest[group:2 * group], rest[2 * group:3 * group]
    o_ref, m_sc, l_sc, acc_sc, carry_sc = rest[3 * group:]
    p = pl.program_id(1)
    hrow = lax.broadcasted_iota(jnp.int32, (BF16_ROWS, MIX_W), 0)
    lane = lax.broadcasted_iota(jnp.int32, (BF16_ROWS, MIX_W), 1)
    hcol = lax.broadcasted_iota(jnp.int32, (BF16_ROWS, 1), 0)
    later = jnp.where(lax.broadcasted_iota(jnp.int32, (page, page), 0)
                      > lax.broadcasted_iota(jnp.int32, (page, page), 1), 1.0, 0.0).astype(BF16)

    @pl.when(p == 0)
    def _():
        m_sc[...] = jnp.full_like(m_sc, -jnp.inf)
        l_sc[...] = jnp.zeros_like(l_sc)
        acc_sc[...] = jnp.zeros_like(acc_sc)
        carry_sc[...] = jnp.zeros_like(carry_sc)

    q = q_ref[0]
    carry = carry_sc[...]
    cnew = cnew_ref[0]
    parts = []
    for j in range(group):
        lf = lf_refs[j][...]
        lf16 = jnp.zeros((BF16_ROWS, page), F32)
        for h in range(FOX_H):
            lf16 = jnp.where(hcol == h, lf[h:h + 1, :], lf16)
        hi, mid, lo = _split3(lf16)
        suffix = _dot(hi, later) + _dot(mid, later) + _dot(lo, later)
        s = None
        for h in range(FOX_H):
            kh = k_refs[j][pl.ds(h, page, stride=FOX_H), :]
            t = _dot_nt(q[:, h * FOX_HD:(h + 1) * FOX_HD], kh.astype(BF16))
            s = t if s is None else s + t
        parts.append(s * scale + (suffix + carry + cnew))
        carry = carry + jnp.sum(lf16, axis=-1, keepdims=True)
    carry_sc[...] = carry
    s = jnp.concatenate(parts, axis=1)

    m_prev = m_sc[...]
    m_new = jnp.maximum(m_prev, jnp.max(s, axis=-1, keepdims=True))
    alpha = jnp.exp(m_prev - m_new)
    pr = jnp.exp(s - m_new)
    l_sc[...] = alpha * l_sc[...] + jnp.sum(pr, axis=-1, keepdims=True)
    pr = pr.astype(BF16)
    for h in range(FOX_H):
        cols = slice(h * FOX_HD, (h + 1) * FOX_HD)
        upd = None
        for j in range(group):
            vh = v_refs[j][pl.ds(h, page, stride=FOX_H), :]
            t = _dot(pr[:, j * page:(j + 1) * page], vh.astype(BF16))
            upd = t if upd is None else upd + t
        acc_sc[:, cols] = alpha * acc_sc[:, cols] + upd
    m_sc[...] = m_new

    @pl.when(p == pl.num_programs(1) - 1)
    def _():
        kn = knew_ref[0].astype(BF16).astype(F32)
        vn = vnew_ref[0].astype(BF16).astype(F32)
        s_new = jnp.sum(q.astype(F32) * kn, axis=-1, keepdims=True) * scale
        m_prev = m_sc[...]
        m_fin = jnp.maximum(m_prev, s_new)
        alpha = jnp.exp(m_prev - m_fin)
        p_new = jnp.exp(s_new - m_fin)
        l_fin = alpha * l_sc[...] + p_new
        acc = alpha * acc_sc[...] + p_new.astype(BF16).astype(F32) * vn
        o = acc / l_fin
        o_ref[0] = jnp.sum(jnp.where((lane >> 7) == hrow, o, 0.0), axis=0, keepdims=True)


def _fox_decode(page_table, q, k_new, v_new, c_new, cache_k, cache_v, cache_lf, l):
    bsz, n_pages = page_table.shape
    depth, n_phys, page = cache_k.shape[:3]
    group = math.gcd(n_pages, FOX_DECODE_PAGES)
    cache_k = cache_k.reshape(depth, n_phys, page * FOX_H, FOX_HD)
    cache_v = cache_v.reshape(depth, n_phys, page * FOX_H, FOX_HD)
    cache_lf = cache_lf.transpose(0, 1, 3, 2)
    hmask = (np.arange(MIX_W)[None, :] // FOX_HD) == np.arange(BF16_ROWS)[:, None]
    q_rows = jnp.where(hmask[None], q[:, None, :], 0.0).astype(BF16)
    c_rows = jnp.pad(c_new, ((0, 0), (0, BF16_ROWS - FOX_H)))[:, :, None]
    pt = page_table.reshape(-1)

    def pg(j):
        return lambda b, p, pt_ref: pt_ref[b * n_pages + (n_pages - 1 - (p * group + j))]

    def kv_spec(j):
        return pl.BlockSpec((None, None, page * FOX_H, FOX_HD),
                            lambda b, p, pt_ref: (l, pg(j)(b, p, pt_ref), 0, 0))

    def lf_spec(j):
        return pl.BlockSpec((None, None, FOX_H, page), lambda b, p, pt_ref: (l, pg(j)(b, p, pt_ref), 0, 0))

    grid_spec = pltpu.PrefetchScalarGridSpec(
        num_scalar_prefetch=1,
        grid=(bsz, n_pages // group),
        in_specs=[pl.BlockSpec((1, BF16_ROWS, MIX_W), lambda b, p, pt_ref: (b, 0, 0)),
                  pl.BlockSpec((1, BF16_ROWS, 1), lambda b, p, pt_ref: (b, 0, 0)),
                  pl.BlockSpec((1, 1, MIX_W), lambda b, p, pt_ref: (b, 0, 0)),
                  pl.BlockSpec((1, 1, MIX_W), lambda b, p, pt_ref: (b, 0, 0))]
                 + [kv_spec(j) for j in range(group)] + [kv_spec(j) for j in range(group)]
                 + [lf_spec(j) for j in range(group)],
        out_specs=pl.BlockSpec((1, 1, MIX_W), lambda b, p, pt_ref: (b, 0, 0)),
        scratch_shapes=[pltpu.VMEM((BF16_ROWS, 1), F32), pltpu.VMEM((BF16_ROWS, 1), F32),
                        pltpu.VMEM((BF16_ROWS, MIX_W), F32), pltpu.VMEM((BF16_ROWS, 1), F32)],
    )
    out = pl.pallas_call(
        functools.partial(_fox_decode_kernel, scale=FOX_HD ** -0.5, page=page, group=group),
        grid_spec=grid_spec,
        out_shape=jax.ShapeDtypeStruct((bsz, 1, MIX_W), F32),
        compiler_params=_cparams("parallel", "arbitrary"),
        name="fox_decode",
    )(pt, q_rows, c_rows, k_new[:, None, :], v_new[:, None, :],
      *([cache_k] * group), *([cache_v] * group), *([cache_lf] * group))
    return out.reshape(bsz, MIX_W)


def _s5_kernel(u_ref, bblk_ref, cblk_ref, are_ref, aim_ref, d_ref, wglu_ref, bglu_ref, h0re_ref, h0im_ref,
               o_ref, hre_ref, him_ref, hs_sc, *, tc, bk, slab):
    c = pl.program_id(0)

    @pl.when(c == 0)
    def _():
        hre_ref[...] = h0re_ref[...]
        him_ref[...] = h0im_ref[...]

    u = u_ref[...]
    hs_sc[...] = _dot(u.astype(BF16), bblk_ref[...])
    for s0 in range(0, S5_STATE, slab):
        re_sl = slice(s0, s0 + slab)
        im_sl = slice(S5_STATE + s0, S5_STATE + s0 + slab)
        ar = jnp.broadcast_to(are_ref[:, re_sl], (bk, slab))
        ai = jnp.broadcast_to(aim_ref[:, re_sl], (bk, slab))

        def step(t, carry, re_sl=re_sl, im_sl=im_sl, ar=ar, ai=ai):
            hr, hi = carry
            rows = pl.ds(pl.multiple_of(t * bk, bk), bk)
            nr = ar * hr - ai * hi + hs_sc[rows, re_sl]
            ni = ar * hi + ai * hr + hs_sc[rows, im_sl]
            hs_sc[rows, re_sl] = nr
            hs_sc[rows, im_sl] = ni
            return nr, ni

        hr, hi = lax.fori_loop(0, tc, step, (hre_ref[:, re_sl], him_ref[:, re_sl]))
        hre_ref[:, re_sl] = hr
        him_ref[:, re_sl] = hi
    y = _dot(hs_sc[...].astype(BF16), cblk_ref[...]) + d_ref[...] * u
    g = jax.nn.gelu(y)
    gate = jax.nn.sigmoid(_dot(g.astype(BF16), wglu_ref[...]) + bglu_ref[...])
    o_ref[...] = (g * gate).astype(o_ref.dtype)


def _s5_params(lam_re, lam_im, log_dt, b_re, b_im, c_re, c_im):
    dt = jnp.exp(log_dt)[:, None]
    mag = jnp.exp(lam_re * dt)
    ab_re, ab_im = mag * jnp.cos(lam_im * dt), mag * jnp.sin(lam_im * dt)
    den = lam_re * lam_re + lam_im * lam_im
    nr, ni = ab_re - 1.0, ab_im
    coef_re = (nr * lam_re + ni * lam_im) / den
    coef_im = (ni * lam_re - nr * lam_im) / den
    bb_re = coef_re[..., None] * b_re - coef_im[..., None] * b_im
    bb_im = coef_re[..., None] * b_im + coef_im[..., None] * b_re
    eye = jnp.eye(S5_G, dtype=F32)
    blk_b = lambda t: jnp.einsum("gph,gk->ghkp", t, eye).reshape(MIX_W, S5_STATE)
    blk_c = lambda t: jnp.einsum("ghp,gk->gpkh", t, eye).reshape(S5_STATE, MIX_W)
    bblk = jnp.concatenate([blk_b(bb_re), blk_b(bb_im)], axis=1).astype(BF16)
    cblk = jnp.concatenate([blk_c(c_re), blk_c(-c_im)], axis=0).astype(BF16)
    return ab_re.reshape(1, S5_STATE), ab_im.reshape(1, S5_STATE), bblk, cblk


def _s5(u_tb, params, d, w_glu, b_glu, h0_re, h0_im, *, bk, steps):
    a_re, a_im, bblk, cblk = params
    tc = min(steps, 64)
    rows = tc * bk
    const = lambda shape: pl.BlockSpec(shape, lambda c: (0,) * len(shape))
    return pl.pallas_call(
        functools.partial(_s5_kernel, tc=tc, bk=bk, slab=512),
        grid=(steps // tc,),
        in_specs=[pl.BlockSpec((rows, MIX_W), lambda c: (c, 0)),
                  const((MIX_W, 2 * S5_STATE)), const((2 * S5_STATE, MIX_W)),
                  const((1, S5_STATE)), const((1, S5_STATE)), const((1, MIX_W)),
                  const((MIX_W, MIX_W)), const((1, MIX_W)),
                  const((bk, S5_STATE)), const((bk, S5_STATE))],
        out_specs=[pl.BlockSpec((rows, MIX_W), lambda c: (c, 0)),
                   const((bk, S5_STATE)), const((bk, S5_STATE))],
        out_shape=[jax.ShapeDtypeStruct((steps * bk, MIX_W), BF16),
                   jax.ShapeDtypeStruct((bk, S5_STATE), F32), jax.ShapeDtypeStruct((bk, S5_STATE), F32)],
        scratch_shapes=[pltpu.VMEM((rows, 2 * S5_STATE), F32)],
        compiler_params=_cparams("arbitrary"),
        name="s5",
    )(u_tb, bblk, cblk, a_re, a_im, d.reshape(1, MIX_W), w_glu, b_glu.reshape(1, MIX_W), h0_re, h0_im)


def _hgrn_kernel(q_ref, f_ref, i_ref, g_ref, lb_ref, ng_ref, tri_ref, s0_ref, o_ref, st_ref, *, c, sub, valid):
    n = pl.program_id(1)

    @pl.when(n == 0)
    def _():
        st_ref[...] = s0_ref[...]

    tri = tri_ref[...]
    for h in range(HG_H):
        cols = slice(h * HG_DK, (h + 1) * HG_DK)
        o, st = _hgrn_head(q_ref[:, cols], f_ref[:, cols], i_ref[:, cols], lb_ref[:, cols], tri, st_ref[0, h],
                           c=c, sub=sub, valid=valid)
        st_ref[0, h] = st
        o = o * lax.rsqrt(jnp.mean(o * o, axis=-1, keepdims=True) + RMS_EPS) * ng_ref[:, cols]
        o_ref[:, cols] = (o * jax.nn.silu(g_ref[:, cols])).astype(o_ref.dtype)


def _hgrn_head(q, z, v, lb, tri, st, *, c, sub, valid):
    la = jnp.log(lb)
    lbb = jnp.log1p(-lb) + _log_sigmoid(z)
    lf = jnp.maximum(la, lbb) + jnp.log1p(jnp.exp(-jnp.abs(la - lbb)))
    if valid < c:
        lf = jnp.where(lax.broadcasted_iota(jnp.int32, lf.shape, 0) < valid, lf, 0.0)
    kt = 1.0 - jnp.exp(lf)
    hi, mid, lo = _split3(lf)
    bc = _dot(tri, hi) + _dot(tri, mid) + _dot(tri, lo)
    o = _dot_nt((q * jnp.exp(bc)).astype(BF16), st.astype(BF16))
    nb = c // sub
    rows = lax.broadcasted_iota(jnp.int32, (sub, HG_DK), 0)
    blocks = []
    for i in range(nb):
        lo_r, hi_r = i * sub, (i + 1) * sub
        bci = bc[lo_r:hi_r]
        qi = q[lo_r:hi_r]
        oi = o[lo_r:hi_r]
        if i > 0:
            e = bc[lo_r - 1:lo_r]
            qs = (qi * jnp.exp(bci - e)).astype(BF16)
            ks = (kt[:lo_r] * jnp.exp(e - bc[:lo_r])).astype(BF16)
            att = _dot_nt(qs, ks)
            oi = oi + _dot(att.astype(BF16), v[:lo_r].astype(BF16))
        for s in range(sub):
            r = lo_r + s
            dec = jnp.exp(jnp.where(rows >= s, bci - bc[r:r + 1], -jnp.inf))
            a = jnp.sum(qi * dec * kt[r:r + 1], axis=-1, keepdims=True)
            oi = oi + a * v[r:r + 1]
        blocks.append(oi)
    o = jnp.concatenate(blocks, axis=0) if nb > 1 else blocks[0]
    bl = bc[c - 1:c]
    st = jnp.exp(bl) * st + _dot_tn(v.astype(BF16), (kt * jnp.exp(bl - bc)).astype(BF16))
    return o, st


def _hgrn(z, col0, lb, norm_g, s0_t, *, bsz, seq_pad, valid):
    c = min(HG_CHUNK, seq_pad)
    sub = min(HG_SUB, c)
    nc = seq_pad // c
    assert col0 % MIX_W == 0
    j0 = col0 // MIX_W
    tri = jnp.asarray(np.tril(np.ones((c, c))), BF16)
    col = lambda k: pl.BlockSpec((c, MIX_W), lambda b, n: (b * nc + n, j0 + k))
    vec = pl.BlockSpec((1, MIX_W), lambda b, n: (0, 0))
    state = pl.BlockSpec((1, HG_H, HG_DV, HG_DK), lambda b, n: (b, 0, 0, 0))
    return pl.pallas_call(
        functools.partial(_hgrn_kernel, c=c, sub=sub, valid=valid),
        grid=(bsz, nc),
        in_specs=[col(0), col(1), col(2), col(3), vec, vec, pl.BlockSpec((c, c), lambda b, n: (0, 0)), state],
        out_specs=[pl.BlockSpec((c, MIX_W), lambda b, n: (b * nc + n, 0)), state],
        out_shape=[jax.ShapeDtypeStruct((bsz * seq_pad, MIX_W), BF16),
                   jax.ShapeDtypeStruct((bsz, HG_H, HG_DV, HG_DK), F32)],
        compiler_params=_cparams("parallel", "arbitrary"),
        name="hgrn2",
    )(z, z, z, z, lb.reshape(1, MIX_W), norm_g.reshape(1, MIX_W), tri, s0_t)


def _gmlp_kernel(u_ref, v_ref, lg_ref, lbias_ref, ws_ref, bs_ref, o_ref):
    vn = _layer_norm(jax.nn.gelu(v_ref[...]), lg_ref[...], lbias_ref[...]).astype(BF16)
    u = jax.nn.gelu(u_ref[...])
    r = lax.broadcasted_iota(jnp.int32, (GM_CHUNK, GM_CHUNK), 0)
    cc = lax.broadcasted_iota(jnp.int32, (GM_CHUNK, GM_CHUNK), 1)
    for g in range(GM_GROUPS):
        sl = slice(g * GM_GW, (g + 1) * GM_GW)
        ws = jnp.where(r >= cc, ws_ref[g], 0.0).astype(BF16)
        mixed = _dot(ws, vn[:, sl]) + bs_ref[g]
        o_ref[:, sl] = (u[:, sl] * mixed).astype(o_ref.dtype)


def _gmlp(z, col0, ln_g, ln_b, w_s, b_s):
    m = z.shape[0]
    assert col0 % MIX_W == 0
    j0 = col0 // MIX_W
    return pl.pallas_call(
        _gmlp_kernel,
        grid=(m // GM_CHUNK,),
        in_specs=[pl.BlockSpec((GM_CHUNK, MIX_W), lambda i: (i, j0)),
                  pl.BlockSpec((GM_CHUNK, MIX_W), lambda i: (i, j0 + 1)),
                  pl.BlockSpec((1, MIX_W), lambda i: (0, 0)),
                  pl.BlockSpec((1, MIX_W), lambda i: (0, 0)),
                  pl.BlockSpec((GM_GROUPS, GM_CHUNK, GM_CHUNK), lambda i: (0, 0, 0)),
                  pl.BlockSpec((GM_GROUPS, GM_CHUNK, 1), lambda i: (0, 0, 0))],
        out_specs=pl.BlockSpec((GM_CHUNK, MIX_W), lambda i: (i, 0)),
        out_shape=jax.ShapeDtypeStruct((m, MIX_W), BF16),
        compiler_params=_cparams("parallel"),
        name="gmlp",
    )(z, z, ln_g.reshape(1, MIX_W), ln_b.reshape(1, MIX_W), w_s, b_s[:, :, None])


def _gmlp_first_kernel(u_ref, v_ref, lg_ref, lbias_ref, w00_ref, b0_ref, o_ref, vn_ref):
    vn = _layer_norm(jax.nn.gelu(v_ref[...]), lg_ref[...], lbias_ref[...])
    vn_ref[...] = vn
    o_ref[...] = (jax.nn.gelu(u_ref[...]) * (vn * w00_ref[...] + b0_ref[...])).astype(o_ref.dtype)


def _gmlp_first(z, col0, ln_g, ln_b, w_s, b_s):
    m = z.shape[0]
    assert col0 % MIX_W == 0
    j0 = col0 // MIX_W
    w00 = jnp.repeat(w_s[:, 0, 0], GM_GW).reshape(1, MIX_W)
    b0 = jnp.repeat(b_s[:, 0], GM_GW).reshape(1, MIX_W)
    vec = pl.BlockSpec((1, MIX_W), lambda i: (0, 0))
    return pl.pallas_call(
        _gmlp_first_kernel,
        grid=(1,),
        in_specs=[pl.BlockSpec((m, MIX_W), lambda i: (0, j0)), pl.BlockSpec((m, MIX_W), lambda i: (0, j0 + 1)),
                  vec, vec, vec, vec],
        out_specs=[pl.BlockSpec((m, MIX_W), lambda i: (0, 0)), pl.BlockSpec((m, MIX_W), lambda i: (0, 0))],
        out_shape=[jax.ShapeDtypeStruct((m, MIX_W), BF16), jax.ShapeDtypeStruct((m, MIX_W), F32)],
        name="gmlp_first",
    )(z, z, ln_g.reshape(1, MIX_W), ln_b.reshape(1, MIX_W), w00, b0)


def _layer_weights(l, w_in, fox_bf, s5_lambda_re, s5_lambda_im, s5_log_dt, s5_b_re, s5_b_im, s5_c_re, s5_c_im,
                   s5_w_glu, hgrn_lb, w_branch, w_out, w_ffn_up, w_ffn_down):
    n_a = 3 * MIX_W + LANES
    o_r = 3 * MIX_W + FOX_H
    lb_all = jnp.cumsum(jax.nn.softmax(hgrn_lb.astype(F32), axis=0), axis=0)
    return dict(
        w_a=_cast_cols(w_in, l, 0, n_a),
        w_r=_cast_cols(w_in, l, o_r, w_in.shape[2] - o_r),
        bf=jnp.pad(fox_bf[l], (0, LANES - FOX_H)).reshape(1, LANES),
        s5=_s5_params(s5_lambda_re[l], s5_lambda_im[l], s5_log_dt[l], s5_b_re[l], s5_b_im[l],
                      s5_c_re[l], s5_c_im[l]),
        w_glu=s5_w_glu[l].astype(BF16),
        lb=lb_all[l] - lb_all[0],
        w_branch=w_branch[l].astype(BF16),
        w_out=w_out[l].astype(BF16),
        w_up=w_ffn_up[l].astype(BF16),
        w_down=w_ffn_down[l].astype(BF16),
    )


def _trunk_layer(x, x_bf, l, lw, P, *, bsz, seq, fox_fn, s5_h0, hg_s0):
    m = bsz * seq
    qkv_f32, qkv_bf = _proj(x_bf, lw["w_a"], n=3 * MIX_W, out_dtypes=(F32, BF16))
    logf = _proj(x_bf, lw["w_a"], col0=3 * MIX_W, n=LANES, act="log_sigmoid_bias", bias=lw["bf"])[0][:, :FOX_H]
    o_a = fox_fn(qkv_f32, qkv_bf, logf)

    z = _proj(x_bf, lw["w_r"], n=Z_GATE)[0]
    su_tb = z[:, :MIX_W].reshape(bsz, seq, MIX_W).transpose(1, 0, 2).reshape(m, MIX_W)
    o_b_tb, s5_re, s5_im = _s5(su_tb, lw["s5"], P["s5_d"][l], lw["w_glu"], P["s5_b_glu"][l],
                               s5_h0[0].reshape(bsz, S5_STATE), s5_h0[1].reshape(bsz, S5_STATE),
                               bk=bsz, steps=seq)
    o_b = o_b_tb.reshape(seq, bsz, MIX_W).transpose(1, 0, 2).reshape(m, MIX_W)

    seq_pad = seq if seq % HG_CHUNK == 0 else -(-seq // BF16_ROWS) * BF16_ROWS
    zh, zh_col0 = z, Z_HG
    if seq_pad != seq:
        zh = z[:, Z_HG:Z_GM].reshape(bsz, seq, 4 * MIX_W)
        zh, zh_col0 = jnp.pad(zh, ((0, 0), (0, seq_pad - seq), (0, 0))).reshape(-1, 4 * MIX_W), 0
    o_c, hg_t = _hgrn(zh, zh_col0, lw["lb"], P["hgrn_norm_g"][l], hg_s0.transpose(0, 1, 3, 2),
                      bsz=bsz, seq_pad=seq_pad, valid=min(seq, HG_CHUNK))
    if seq_pad != seq:
        o_c = o_c.reshape(bsz, seq_pad, MIX_W)[:, :seq].reshape(m, MIX_W)
    hg_state = hg_t.transpose(0, 1, 3, 2)

    if seq == 1:
        o_d, v_rows = _gmlp_first(z, Z_GM, P["gmlp_ln_g"][l], P["gmlp_ln_b"][l], P["gmlp_w_s"][l], P["gmlp_b_s"][l])
    else:
        o_d = _gmlp(z, Z_GM, P["gmlp_ln_g"][l], P["gmlp_ln_b"][l], P["gmlp_w_s"][l], P["gmlp_b_s"][l])
        v_rows = None

    merged = _merge(x_bf, (o_a, o_b, o_c, o_d), lw["w_r"], Z_GATE, lw["w_branch"])
    x1, x1_bf = _matmul_res_ln(merged, lw["w_out"], x, P["ln1_g"][l], P["ln1_b"][l], tk=D_MODEL, tm=256)
    hff = _ffn_up(x1_bf, lw["w_up"])
    x2, x2_bf = _matmul_res_ln(hff, lw["w_down"], x1, P["ln2_g"][l], P["ln2_b"][l], tk=512, tm=512)

    fk = qkv_f32[:, MIX_W:2 * MIX_W].reshape(bsz, seq, FOX_H, FOX_HD)
    fv = qkv_f32[:, 2 * MIX_W:].reshape(bsz, seq, FOX_H, FOX_HD)
    state = (fk, fv, logf.reshape(bsz, seq, FOX_H), s5_re.reshape(bsz, S5_G, S5_P),
             s5_im.reshape(bsz, S5_G, S5_P), hg_state, v_rows)
    return x2, x2_bf, state


def kernel(x_prompt, x_sample, cache_k, cache_v, cache_logf, page_table, state_s5_re, state_s5_im, state_hgrn,
           w_in, fox_bf, s5_lambda_re, s5_lambda_im, s5_log_dt, s5_b_re, s5_b_im, s5_c_re, s5_c_im, s5_d,
           s5_w_glu, s5_b_glu, hgrn_lb, hgrn_norm_g, gmlp_ln_g, gmlp_ln_b, gmlp_w_s, gmlp_b_s, w_branch, w_out,
           ln1_g, ln1_b, w_ffn_up, w_ffn_down, ln2_g, ln2_b):
    b_p, l_p, _ = x_prompt.shape
    b_s, l_s, _ = x_sample.shape
    assert l_s == 1, "the sample group decodes one token per sequence"
    P = dict(s5_d=s5_d, s5_b_glu=s5_b_glu, hgrn_norm_g=hgrn_norm_g, gmlp_ln_g=gmlp_ln_g, gmlp_ln_b=gmlp_ln_b,
             gmlp_w_s=gmlp_w_s, gmlp_b_s=gmlp_b_s, ln1_g=ln1_g, ln1_b=ln1_b, ln2_g=ln2_g, ln2_b=ln2_b)

    xp = x_prompt.reshape(b_p * l_p, D_MODEL)
    xs = x_sample.reshape(b_s * l_s, D_MODEL)
    xp_bf, xs_bf = xp.astype(BF16), xs.astype(BF16)
    s5_zero = jnp.zeros((b_p, S5_G, S5_P), F32)
    hg_zero = jnp.zeros((b_p, HG_H, HG_DK, HG_DV), F32)
    p_states, s_states = [], []
    for l in range(DEPTH):
        lw = _layer_weights(l, w_in, fox_bf, s5_lambda_re, s5_lambda_im, s5_log_dt, s5_b_re, s5_b_im,
                            s5_c_re, s5_c_im, s5_w_glu, hgrn_lb, w_branch, w_out, w_ffn_up, w_ffn_down)

        def fox_p(qkv_f32, qkv_bf, logf):
            c = _cumsum_lanes(logf.reshape(b_p, l_p, FOX_H).transpose(0, 2, 1).reshape(b_p * FOX_H, l_p))
            return _fox_prompt(qkv_bf, c.reshape(b_p, FOX_H, l_p), b_p, l_p)

        def fox_s(qkv_f32, qkv_bf, logf, l=l):
            o = _fox_decode(page_table, qkv_f32[:, :MIX_W], qkv_f32[:, MIX_W:2 * MIX_W], qkv_f32[:, 2 * MIX_W:],
                            logf, cache_k, cache_v, cache_logf, l)
            return o.astype(BF16)

        xp, xp_bf, sp = _trunk_layer(xp, xp_bf, l, lw, P, bsz=b_p, seq=l_p, fox_fn=fox_p,
                                     s5_h0=(s5_zero, s5_zero), hg_s0=hg_zero)
        xs, xs_bf, ss = _trunk_layer(xs, xs_bf, l, lw, P, bsz=b_s, seq=l_s, fox_fn=fox_s,
                                     s5_h0=(state_s5_re[l], state_s5_im[l]), hg_s0=state_hgrn[l])
        p_states.append(sp)
        s_states.append(ss)

    stack = lambda states, i: jnp.stack([s[i] for s in states])
    return (xp.reshape(b_p, l_p, D_MODEL), xs.reshape(b_s, l_s, D_MODEL),
            stack(p_states, 0), stack(p_states, 1), stack(p_states, 2),
            stack(s_states, 0), stack(s_states, 1), stack(s_states, 2),
            stack(p_states, 3), stack(p_states, 4), stack(s_states, 3), stack(s_states, 4),
            stack(p_states, 5), stack(s_states, 5),
            jnp.stack([s[6].reshape(b_s, l_s, MIX_W) for s in s_states]))
```

```python
import functools
import math

import numpy as np
import jax
import jax.numpy as jnp
from jax import lax
from jax.experimental import pallas as pl
from jax.experimental.pallas import tpu as pltpu

F32 = jnp.float32
BF16 = jnp.bfloat16

D_MODEL = 2048
DEPTH = 2
N_BRANCH = 4
MIX_W = D_MODEL // N_BRANCH
FOX_HD = 128
FOX_H = MIX_W // FOX_HD
S5_GROUP = 16
S5_G = MIX_W // S5_GROUP
S5_P = 64
S5_STATE = S5_G * S5_P
HG_DK = 128
HG_DV = 128
HG_H = MIX_W // HG_DV
HG_CHUNK = 64
HG_SUB = 16
HG_SAFE_RANGE = 60.0
GM_CHUNK = 128
GM_GROUPS = 4
GM_GW = MIX_W // GM_GROUPS
D_FF = ((8 * D_MODEL // 3 + 255) // 256) * 256
DEEPNORM_ALPHA = (2 * DEPTH) ** 0.25
LN_EPS = 1e-5
RMS_EPS = 1e-6
LANES = 128
SUBLANES = 8
BF16_ROWS = 16
FOX_DECODE_PAGES = 8
Z_HG = MIX_W
Z_GM = Z_HG + 4 * MIX_W
Z_GATE = Z_GM + 2 * MIX_W
VMEM_LIMIT = 48 * 1024 * 1024
VMEM_LIMIT_BIG = 58 * 1024 * 1024
PROJ_TN_MAX = 1792


def _cparams(*sem):
    return pltpu.CompilerParams(dimension_semantics=sem, vmem_limit_bytes=VMEM_LIMIT)


def _log_sigmoid(x):
    return jnp.minimum(x, 0.0) - jnp.log1p(jnp.exp(-jnp.abs(x)))


def _layer_norm(x, g, b):
    mu = jnp.mean(x, axis=-1, keepdims=True)
    xc = x - mu
    var = jnp.mean(xc * xc, axis=-1, keepdims=True)
    return xc * lax.rsqrt(var + LN_EPS) * g + b


def _split3(x):
    hi = x.astype(BF16)
    r1 = x - hi.astype(F32)
    mid = r1.astype(BF16)
    lo = (r1 - mid.astype(F32)).astype(BF16)
    return hi, mid, lo


def _dot(a, b):
    return jnp.dot(a, b, preferred_element_type=F32)


def _dot_nt(a, b):
    return lax.dot_general(a, b, (((1,), (1,)), ((), ())), preferred_element_type=F32)


def _dot_tn(a, b):
    return lax.dot_general(a, b, (((0,), (0,)), ((), ())), preferred_element_type=F32)


def _cast_cols_kernel(*refs, shift):
    o_ref = refs[-1]
    a = refs[0][...]
    if shift:
        a = jnp.concatenate([a, refs[1][...]], axis=1)[:, shift:shift + LANES]
    o_ref[...] = a.astype(o_ref.dtype)


def _cast_cols(w, l, col0, n):
    k = w.shape[1]
    shift = col0 % LANES
    j0 = col0 // LANES
    in_specs = [pl.BlockSpec((None, k, LANES), lambda j: (l, 0, j0 + j))]
    if shift:
        in_specs.append(pl.BlockSpec((None, k, LANES), lambda j: (l, 0, j0 + j + 1)))
    return pl.pallas_call(
        functools.partial(_cast_cols_kernel, shift=shift),
        grid=(n // LANES,),
        in_specs=in_specs,
        out_specs=pl.BlockSpec((k, LANES), lambda j: (0, j)),
        out_shape=jax.ShapeDtypeStruct((k, n), BF16),
        compiler_params=_cparams("parallel"),
        name="cast_cols",
    )(*([w] * len(in_specs)))


def _proj_kernel(x_ref, w_ref, b_ref, *out_refs, act):
    z = _dot(x_ref[...], w_ref[...])
    if act == "gelu":
        z = jax.nn.gelu(z)
    elif act == "sigmoid":
        z = jax.nn.sigmoid(z)
    elif act == "log_sigmoid_bias":
        z = _log_sigmoid(z + b_ref[...])
    for o in out_refs:
        o[...] = z.astype(o.dtype)


def _proj(x, w, *, col0=0, n=None, act=None, bias=None, out_dtypes=(F32,)):
    m, k = x.shape
    n = w.shape[1] if n is None else n
    tm = min(m, 512)
    tn = max(t for t in range(LANES, min(n, PROJ_TN_MAX) + 1, LANES) if n % t == 0 and col0 % t == 0)
    j0 = col0 // tn
    if bias is None:
        bias = jnp.zeros((1, n), F32)
    return pl.pallas_call(
        functools.partial(_proj_kernel, act=act),
        grid=(n // tn, m // tm),
        in_specs=[pl.BlockSpec((tm, k), lambda j, i: (i, 0)),
                  pl.BlockSpec((k, tn), lambda j, i: (0, j0 + j)),
                  pl.BlockSpec((1, tn), lambda j, i: (0, j))],
        out_specs=[pl.BlockSpec((tm, tn), lambda j, i: (i, j)) for _ in out_dtypes],
        out_shape=[jax.ShapeDtypeStruct((m, n), d) for d in out_dtypes],
        compiler_params=_cparams("parallel", "arbitrary"),
        name="proj_" + (act or "id"),
    )(x, w, bias)


def _merge_kernel(x_ref, oa_ref, ob_ref, oc_ref, od_ref, g0_ref, g1_ref, g2_ref, g3_ref, wb_ref, o_ref):
    x = x_ref[...]
    acc = None
    for br, gr, k in ((oa_ref, g0_ref, 0), (ob_ref, g1_ref, 1), (oc_ref, g2_ref, 2), (od_ref, g3_ref, 3)):
        t = jax.nn.sigmoid(_dot(x, gr[...])) * _dot(br[...], wb_ref[k])
        acc = t if acc is None else acc + t
    o_ref[...] = acc.astype(o_ref.dtype)


def _merge(x, branches, w, gate_col0, w_branch):
    m, k_in = x.shape
    tm = min(m, 512)
    tn = 512
    nb = D_MODEL // tn
    assert gate_col0 % tn == 0
    j0 = gate_col0 // tn
    gate_specs = [pl.BlockSpec((k_in, tn), functools.partial(lambda j, i, k: (0, j0 + k * nb + j), k=k))
                  for k in range(N_BRANCH)]
    return pl.pallas_call(
        _merge_kernel,
        grid=(nb, m // tm),
        in_specs=[pl.BlockSpec((tm, k_in), lambda j, i: (i, 0))]
                 + [pl.BlockSpec((tm, MIX_W), lambda j, i: (i, 0)) for _ in range(N_BRANCH)] + gate_specs
                 + [pl.BlockSpec((N_BRANCH, MIX_W, tn), lambda j, i: (0, 0, j))],
        out_specs=pl.BlockSpec((tm, tn), lambda j, i: (i, j)),
        out_shape=jax.ShapeDtypeStruct((m, D_MODEL), BF16),
        compiler_params=_cparams("parallel", "arbitrary"),
        name="merge",
    )(x, *branches, w, w, w, w, w_branch)


def _res_ln_kernel(a_ref, w_ref, x_ref, g_ref, b_ref, y_ref, ybf_ref):
    y = _layer_norm(DEEPNORM_ALPHA * x_ref[...] + _dot(a_ref[...], w_ref[...]), g_ref[...], b_ref[...])
    y_ref[...] = y
    ybf_ref[...] = y.astype(BF16)


def _matmul_res_ln(a, w, x, g, b):
    m, k = a.shape
    n = w.shape[1]
    tm = min(m, 256)
    return pl.pallas_call(
        _res_ln_kernel,
        grid=(m // tm,),
        in_specs=[pl.BlockSpec((tm, k), lambda i: (i, 0)),
                  pl.BlockSpec((k, n), lambda i: (0, 0), pipeline_mode=pl.Buffered(1)),
                  pl.BlockSpec((tm, n), lambda i: (i, 0)),
                  pl.BlockSpec((1, n), lambda i: (0, 0)),
                  pl.BlockSpec((1, n), lambda i: (0, 0))],
        out_specs=[pl.BlockSpec((tm, n), lambda i: (i, 0)),
                   pl.BlockSpec((tm, n), lambda i: (i, 0))],
        out_shape=[jax.ShapeDtypeStruct((m, n), F32), jax.ShapeDtypeStruct((m, n), BF16)],
        compiler_params=pltpu.CompilerParams(dimension_semantics=("parallel",), vmem_limit_bytes=VMEM_LIMIT_BIG),
        name="matmul_res_ln",
    )(a, w, x, g.reshape(1, n), b.reshape(1, n))


def _ffn_up_kernel(x_ref, wg_ref, wu_ref, o_ref):
    x = x_ref[...]
    o_ref[...] = (jax.nn.silu(_dot(x, wg_ref[...])) * _dot(x, wu_ref[...])).astype(o_ref.dtype)


def _ffn_up(x, w_up):
    m, k = x.shape
    tm = min(m, 1024)
    tn = 512
    nb = D_FF // tn
    return pl.pallas_call(
        _ffn_up_kernel,
        grid=(nb, m // tm),
        in_specs=[pl.BlockSpec((tm, k), lambda j, i: (i, 0)),
                  pl.BlockSpec((k, tn), lambda j, i: (0, j)),
                  pl.BlockSpec((k, tn), lambda j, i: (0, nb + j))],
        out_specs=pl.BlockSpec((tm, tn), lambda j, i: (i, j)),
        out_shape=jax.ShapeDtypeStruct((m, D_FF), BF16),
        compiler_params=_cparams("parallel", "arbitrary"),
        name="ffn_up",
    )(x, w_up, w_up)


def _cumsum_lanes_kernel(x_ref, o_ref):
    x = x_ref[...]
    n = x.shape[-1]
    lane = lax.broadcasted_iota(jnp.int32, x.shape, 1)
    sh = 1
    while sh < n:
        x = x + jnp.where(lane >= sh, pltpu.roll(x, sh, axis=1), 0.0)
        sh *= 2
    o_ref[...] = x


def _cumsum_lanes(x):
    return pl.pallas_call(_cumsum_lanes_kernel, out_shape=jax.ShapeDtypeStruct(x.shape, F32),
                          name="cumsum_lanes")(x)


def _flash_kernel(q_ref, k_ref, v_ref, cq_ref, ck_ref, o_ref, qt_sc, m_sc, l_sc, acc_sc, *, tq, tk, scale):
    qi = pl.program_id(2)
    ki = pl.program_id(3)

    @pl.when(ki == 0)
    def _():
        qt_sc[...] = q_ref[...].astype(F32).T.astype(BF16)
        m_sc[...] = jnp.full_like(m_sc, -jnp.inf)
        l_sc[...] = jnp.zeros_like(l_sc)
        acc_sc[...] = jnp.zeros_like(acc_sc)

    def update(masked):
        s = _dot(k_ref[...], qt_sc[...]) * scale
        s = s + cq_ref[0, 0] - jnp.concatenate([ck_ref[0, 0]] * (tq // LANES), axis=1)
        if masked:
            k_pos = lax.broadcasted_iota(jnp.int32, (tk, tq), 0)
            q_pos = lax.broadcasted_iota(jnp.int32, (tk, tq), 1)
            s = jnp.where(k_pos <= q_pos, s, -jnp.inf)
        m_prev = m_sc[...]
        m_new = jnp.maximum(m_prev, jnp.max(s, axis=0, keepdims=True))
        alpha = jnp.exp(m_prev - m_new)
        p = jnp.exp(s - m_new)
        l_sc[...] = alpha * l_sc[...] + jnp.sum(p, axis=0, keepdims=True)
        acc_sc[...] = alpha * acc_sc[...] + _dot_tn(v_ref[...], p.astype(BF16))
        m_sc[...] = m_new

    pl.when(ki < qi)(functools.partial(update, False))
    pl.when(ki == qi)(functools.partial(update, True))

    @pl.when(ki == pl.num_programs(3) - 1)
    def _():
        o_ref[...] = (acc_sc[...] / l_sc[...]).T.astype(o_ref.dtype)


def _fox_prompt(qkv, c, bsz, seq):
    tq = tk = min(seq, 512)
    nq = seq // tq
    nk = seq // tk
    cq = c.reshape(bsz, FOX_H, 1, seq)
    ck = jnp.broadcast_to(c[..., None], (bsz, FOX_H, seq, LANES))

    def kv_blk(qi, ki):
        return jnp.minimum(ki, qi)

    return pl.pallas_call(
        functools.partial(_flash_kernel, tq=tq, tk=tk, scale=FOX_HD ** -0.5),
        grid=(bsz, FOX_H, nq, nk),
        in_specs=[pl.BlockSpec((tq, FOX_HD), lambda b, h, qi, ki: (b * nq + qi, h)),
                  pl.BlockSpec((tk, FOX_HD), lambda b, h, qi, ki: (b * nk + kv_blk(qi, ki), FOX_H + h)),
                  pl.BlockSpec((tk, FOX_HD), lambda b, h, qi, ki: (b * nk + kv_blk(qi, ki), 2 * FOX_H + h)),
                  pl.BlockSpec((1, 1, 1, tq), lambda b, h, qi, ki: (b, h, 0, qi)),
                  pl.BlockSpec((1, 1, tk, LANES), lambda b, h, qi, ki: (b, h, kv_blk(qi, ki), 0))],
        out_specs=pl.BlockSpec((tq, FOX_HD), lambda b, h, qi, ki: (b * nq + qi, h)),
        out_shape=jax.ShapeDtypeStruct((bsz * seq, MIX_W), BF16),
        scratch_shapes=[pltpu.VMEM((FOX_HD, tq), BF16), pltpu.VMEM((1, tq), F32), pltpu.VMEM((1, tq), F32),
                        pltpu.VMEM((FOX_HD, tq), F32)],
        compiler_params=_cparams("parallel", "parallel", "parallel", "arbitrary"),
        name="fox_prompt",
    )(qkv, qkv, qkv, cq, ck)


def _fox_decode_kernel(pt_ref, q_ref, cnew_ref, knew_ref, vnew_ref, *rest, scale, page, group):
    k_refs, v_refs, lf_refs = rest[:group], rest[group:2 * group], rest[2 * group:3 * group]
    o_ref, m_sc, l_sc, acc_sc, carry_sc = rest[3 * group:]
    p = pl.program_id(1)
    hrow = lax.broadcasted_iota(jnp.int32, (BF16_ROWS, MIX_W), 0)
    lane = lax.broadcasted_iota(jnp.int32, (BF16_ROWS, MIX_W), 1)
    hcol = lax.broadcasted_iota(jnp.int32, (BF16_ROWS, 1), 0)
    later = jnp.where(lax.broadcasted_iota(jnp.int32, (page, page), 0)
                      > lax.broadcasted_iota(jnp.int32, (page, page), 1), 1.0, 0.0).astype(BF16)

    @pl.when(p == 0)
    def _():
        m_sc[...] = jnp.full_like(m_sc, -jnp.inf)
        l_sc[...] = jnp.zeros_like(l_sc)
        acc_sc[...] = jnp.zeros_like(acc_sc)
        carry_sc[...] = jnp.zeros_like(carry_sc)

    q = q_ref[0]
    carry = carry_sc[...]
    cnew = cnew_ref[0]
    parts = []
    for j in range(group):
        lf = lf_refs[j][...]
        lf16 = jnp.zeros((BF16_ROWS, page), F32)
        for h in range(FOX_H):
            lf16 = jnp.where(hcol == h, lf[h:h + 1, :], lf16)
        hi, mid, lo = _split3(lf16)
        suffix = _dot(hi, later) + _dot(mid, later) + _dot(lo, later)
        s = None
        for h in range(FOX_H):
            kh = k_refs[j][pl.ds(h, page, stride=FOX_H), :]
            t = _dot_nt(q[:, h * FOX_HD:(h + 1) * FOX_HD], kh.astype(BF16))
            s = t if s is None else s + t
        parts.append(s * scale + (suffix + carry + cnew))
        carry = carry + jnp.sum(lf16, axis=-1, keepdims=True)
    carry_sc[...] = carry
    s = jnp.concatenate(parts, axis=1)

    m_prev = m_sc[...]
    m_new = jnp.maximum(m_prev, jnp.max(s, axis=-1, keepdims=True))
    alpha = jnp.exp(m_prev - m_new)
    pr = jnp.exp(s - m_new)
    l_sc[...] = alpha * l_sc[...] + jnp.sum(pr, axis=-1, keepdims=True)
    pr = pr.astype(BF16)
    for h in range(FOX_H):
        cols = slice(h * FOX_HD, (h + 1) * FOX_HD)
        upd = None
        for j in range(group):
            vh = v_refs[j][pl.ds(h, page, stride=FOX_H), :]
            t = _dot(pr[:, j * page:(j + 1) * page], vh.astype(BF16))
            upd = t if upd is None else upd + t
        acc_sc[:, cols] = alpha * acc_sc[:, cols] + upd
    m_sc[...] = m_new

    @pl.when(p == pl.num_programs(1) - 1)
    def _():
        kn = knew_ref[0].astype(BF16).astype(F32)
        vn = vnew_ref[0].astype(BF16).astype(F32)
        s_new = jnp.sum(q.astype(F32) * kn, axis=-1, keepdims=True) * scale
        m_prev = m_sc[...]
        m_fin = jnp.maximum(m_prev, s_new)
        alpha = jnp.exp(m_prev - m_fin)
        p_new = jnp.exp(s_new - m_fin)
        l_fin = alpha * l_sc[...] + p_new
        acc = alpha * acc_sc[...] + p_new.astype(BF16).astype(F32) * vn
        o = acc / l_fin
        o_ref[0] = jnp.sum(jnp.where((lane >> 7) == hrow, o, 0.0), axis=0, keepdims=True)


def _fox_decode(page_table, q, k_new, v_new, c_new, cache_k, cache_v, cache_lf, l):
    bsz, n_pages = page_table.shape
    depth, n_phys, page = cache_k.shape[:3]
    group = math.gcd(n_pages, FOX_DECODE_PAGES)
    cache_k = cache_k.reshape(depth, n_phys, page * FOX_H, FOX_HD)
    cache_v = cache_v.reshape(depth, n_phys, page * FOX_H, FOX_HD)
    cache_lf = cache_lf.transpose(0, 1, 3, 2)
    hmask = (np.arange(MIX_W)[None, :] // FOX_HD) == np.arange(BF16_ROWS)[:, None]
    q_rows = jnp.where(hmask[None], q[:, None, :], 0.0).astype(BF16)
    c_rows = jnp.pad(c_new, ((0, 0), (0, BF16_ROWS - FOX_H)))[:, :, None]
    pt = page_table.reshape(-1)

    def pg(j):
        return lambda b, p, pt_ref: pt_ref[b * n_pages + (n_pages - 1 - (p * group + j))]

    def kv_spec(j):
        return pl.BlockSpec((None, None, page * FOX_H, FOX_HD),
                            lambda b, p, pt_ref: (l, pg(j)(b, p, pt_ref), 0, 0))

    def lf_spec(j):
        return pl.BlockSpec((None, None, FOX_H, page), lambda b, p, pt_ref: (l, pg(j)(b, p, pt_ref), 0, 0))

    grid_spec = pltpu.PrefetchScalarGridSpec(
        num_scalar_prefetch=1,
        grid=(bsz, n_pages // group),
        in_specs=[pl.BlockSpec((1, BF16_ROWS, MIX_W), lambda b, p, pt_ref: (b, 0, 0)),
                  pl.BlockSpec((1, BF16_ROWS, 1), lambda b, p, pt_ref: (b, 0, 0)),
                  pl.BlockSpec((1, 1, MIX_W), lambda b, p, pt_ref: (b, 0, 0)),
                  pl.BlockSpec((1, 1, MIX_W), lambda b, p, pt_ref: (b, 0, 0))]
                 + [kv_spec(j) for j in range(group)] + [kv_spec(j) for j in range(group)]
                 + [lf_spec(j) for j in range(group)],
        out_specs=pl.BlockSpec((1, 1, MIX_W), lambda b, p, pt_ref: (b, 0, 0)),
        scratch_shapes=[pltpu.VMEM((BF16_ROWS, 1), F32), pltpu.VMEM((BF16_ROWS, 1), F32),
                        pltpu.VMEM((BF16_ROWS, MIX_W), F32), pltpu.VMEM((BF16_ROWS, 1), F32)],
    )
    out = pl.pallas_call(
        functools.partial(_fox_decode_kernel, scale=FOX_HD ** -0.5, page=page, group=group),
        grid_spec=grid_spec,
        out_shape=jax.ShapeDtypeStruct((bsz, 1, MIX_W), F32),
        compiler_params=_cparams("parallel", "arbitrary"),
        name="fox_decode",
    )(pt, q_rows, c_rows, k_new[:, None, :], v_new[:, None, :],
      *([cache_k] * group), *([cache_v] * group), *([cache_lf] * group))
    return out.reshape(bsz, MIX_W)


def _s5_kernel(u_ref, bblk_ref, cblk_ref, are_ref, aim_ref, d_ref, wglu_ref, bglu_ref, h0re_ref, h0im_ref,
               o_ref, hre_ref, him_ref, hs_sc, *, tc, bk, slab):
    c = pl.program_id(0)

    @pl.when(c == 0)
    def _():
        hre_ref[...] = h0re_ref[...]
        him_ref[...] = h0im_ref[...]

    u = u_ref[...]
    hs_sc[...] = _dot(u.astype(BF16), bblk_ref[...])
    for s0 in range(0, S5_STATE, slab):
        re_sl = slice(s0, s0 + slab)
        im_sl = slice(S5_STATE + s0, S5_STATE + s0 + slab)
        ar = jnp.broadcast_to(are_ref[:, re_sl], (bk, slab))
        ai = jnp.broadcast_to(aim_ref[:, re_sl], (bk, slab))

        def step(t, carry, re_sl=re_sl, im_sl=im_sl, ar=ar, ai=ai):
            hr, hi = carry
            rows = pl.ds(pl.multiple_of(t * bk, bk), bk)
            nr = ar * hr - ai * hi + hs_sc[rows, re_sl]
            ni = ar * hi + ai * hr + hs_sc[rows, im_sl]
            hs_sc[rows, re_sl] = nr
            hs_sc[rows, im_sl] = ni
            return nr, ni

        hr, hi = lax.fori_loop(0, tc, step, (hre_ref[:, re_sl], him_ref[:, re_sl]))
        hre_ref[:, re_sl] = hr
        him_ref[:, re_sl] = hi
    y = _dot(hs_sc[...].astype(BF16), cblk_ref[...]) + d_ref[...] * u
    g = jax.nn.gelu(y)
    gate = jax.nn.sigmoid(_dot(g.astype(BF16), wglu_ref[...]) + bglu_ref[...])
    o_ref[...] = (g * gate).astype(o_ref.dtype)


def _s5_params(lam_re, lam_im, log_dt, b_re, b_im, c_re, c_im):
    dt = jnp.exp(log_dt)[:, None]
    mag = jnp.exp(lam_re * dt)
    ab_re, ab_im = mag * jnp.cos(lam_im * dt), mag * jnp.sin(lam_im * dt)
    den = lam_re * lam_re + lam_im * lam_im
    nr, ni = ab_re - 1.0, ab_im
    coef_re = (nr * lam_re + ni * lam_im) / den
    coef_im = (ni * lam_re - nr * lam_im) / den
    bb_re = coef_re[..., None] * b_re - coef_im[..., None] * b_im
    bb_im = coef_re[..., None] * b_im + coef_im[..., None] * b_re
    eye = jnp.eye(S5_G, dtype=F32)
    blk_b = lambda t: jnp.einsum("gph,gk->ghkp", t, eye).reshape(MIX_W, S5_STATE)
    blk_c = lambda t: jnp.einsum("ghp,gk->gpkh", t, eye).reshape(S5_STATE, MIX_W)
    bblk = jnp.concatenate([blk_b(bb_re), blk_b(bb_im)], axis=1).astype(BF16)
    cblk = jnp.concatenate([blk_c(c_re), blk_c(-c_im)], axis=0).astype(BF16)
    return ab_re.reshape(1, S5_STATE), ab_im.reshape(1, S5_STATE), bblk, cblk


def _s5(u_tb, params, d, w_glu, b_glu, h0_re, h0_im, *, bk, steps):
    a_re, a_im, bblk, cblk = params
    tc = min(steps, 64)
    rows = tc * bk
    const = lambda shape: pl.BlockSpec(shape, lambda c: (0,) * len(shape))
    return pl.pallas_call(
        functools.partial(_s5_kernel, tc=tc, bk=bk, slab=512),
        grid=(steps // tc,),
        in_specs=[pl.BlockSpec((rows, MIX_W), lambda c: (c, 0)),
                  const((MIX_W, 2 * S5_STATE)), const((2 * S5_STATE, MIX_W)),
                  const((1, S5_STATE)), const((1, S5_STATE)), const((1, MIX_W)),
                  const((MIX_W, MIX_W)), const((1, MIX_W)),
                  const((bk, S5_STATE)), const((bk, S5_STATE))],
        out_specs=[pl.BlockSpec((rows, MIX_W), lambda c: (c, 0)),
                   const((bk, S5_STATE)), const((bk, S5_STATE))],
        out_shape=[jax.ShapeDtypeStruct((steps * bk, MIX_W), BF16),
                   jax.ShapeDtypeStruct((bk, S5_STATE), F32), jax.ShapeDtypeStruct((bk, S5_STATE), F32)],
        scratch_shapes=[pltpu.VMEM((rows, 2 * S5_STATE), F32)],
        compiler_params=_cparams("arbitrary"),
        name="s5",
    )(u_tb, bblk, cblk, a_re, a_im, d.reshape(1, MIX_W), w_glu, b_glu.reshape(1, MIX_W), h0_re, h0_im)


def _hgrn_kernel(q_ref, f_ref, i_ref, g_ref, lb_ref, ng_ref, tri_ref, s0_ref, o_ref, st_ref, *, c, sub, valid):
    n = pl.program_id(1)

    @pl.when(n == 0)
    def _():
        st_ref[...] = s0_ref[...]

    tri = tri_ref[...]
    nb = c // sub
    decay = []
    worst = jnp.zeros((1, HG_DK), F32)
    for h in range(HG_H):
        cols = slice(h * HG_DK, (h + 1) * HG_DK)
        lb = lb_ref[:, cols]
        la = jnp.log(lb)
        lbb = jnp.log1p(-lb) + _log_sigmoid(f_ref[:, cols])
        lf = jnp.maximum(la, lbb) + jnp.log1p(jnp.exp(-jnp.abs(la - lbb)))
        if valid < c:
            lf = jnp.where(lax.broadcasted_iota(jnp.int32, lf.shape, 0) < valid, lf, 0.0)
        hi, mid, lo = _split3(lf)
        bc = _dot(tri, hi) + _dot(tri, mid) + _dot(tri, lo)
        decay.append((1.0 - jnp.exp(lf), bc))
        for i in range(nb):
            top = bc[i * sub - 1:i * sub] if i else 0.0
            worst = jnp.maximum(worst, top - bc[(i + 1) * sub - 1:(i + 1) * sub])
    safe = jnp.max(worst) < HG_SAFE_RANGE

    def run(exact):
        for h in range(HG_H):
            cols = slice(h * HG_DK, (h + 1) * HG_DK)
            kt, bc = decay[h]
            o, st = _hgrn_head(q_ref[:, cols], kt, bc, i_ref[:, cols], st_ref[0, h], c=c, sub=sub, exact=exact)
            st_ref[0, h] = st
            o = o * lax.rsqrt(jnp.mean(o * o, axis=-1, keepdims=True) + RMS_EPS) * ng_ref[:, cols]
            o_ref[:, cols] = (o * jax.nn.silu(g_ref[:, cols])).astype(o_ref.dtype)

    pl.when(safe)(functools.partial(run, False))
    pl.when(jnp.logical_not(safe))(functools.partial(run, True))


def _hgrn_head(q, kt, bc, v, st, *, c, sub, exact):
    o = _dot_nt((q * jnp.exp(bc)).astype(BF16), st.astype(BF16))
    nb = c // sub
    rows = lax.broadcasted_iota(jnp.int32, (sub, HG_DK), 0)
    blocks = []
    for i in range(nb):
        lo_r, hi_r = i * sub, (i + 1) * sub
        bci = bc[lo_r:hi_r]
        qi = q[lo_r:hi_r]
        oi = o[lo_r:hi_r]
        e = bc[lo_r - 1:lo_r] if i else jnp.zeros((1, HG_DK), F32)
        n_keys = lo_r if exact else hi_r
        if n_keys:
            qs = (qi * jnp.exp(bci - e)).astype(BF16)
            ks = (kt[:n_keys] * jnp.exp(e - bc[:n_keys])).astype(BF16)
            att = _dot_nt(qs, ks)
            if not exact:
                t_pos = lo_r + lax.broadcasted_iota(jnp.int32, (sub, n_keys), 0)
                s_pos = lax.broadcasted_iota(jnp.int32, (sub, n_keys), 1)
                att = jnp.where(s_pos <= t_pos, att, 0.0)
            oi = oi + _dot(att.astype(BF16), v[:n_keys].astype(BF16))
        if exact:
            for s in range(sub):
                r = lo_r + s
                dec = jnp.exp(jnp.where(rows >= s, bci - bc[r:r + 1], -jnp.inf))
                a = jnp.sum(qi * dec * kt[r:r + 1], axis=-1, keepdims=True)
                oi = oi + a * v[r:r + 1]
        blocks.append(oi)
    o = jnp.concatenate(blocks, axis=0) if nb > 1 else blocks[0]
    bl = bc[c - 1:c]
    st = jnp.exp(bl) * st + _dot_tn(v.astype(BF16), (kt * jnp.exp(bl - bc)).astype(BF16))
    return o, st


def _hgrn(z, col0, lb, norm_g, s0_t, *, bsz, seq_pad, valid):
    c = min(HG_CHUNK, seq_pad)
    sub = min(HG_SUB, c)
    nc = seq_pad // c
    assert col0 % MIX_W == 0
    j0 = col0 // MIX_W
    tri = jnp.asarray(np.tril(np.ones((c, c))), BF16)
    col = lambda k: pl.BlockSpec((c, MIX_W), lambda b, n: (b * nc + n, j0 + k))
    vec = pl.BlockSpec((1, MIX_W), lambda b, n: (0, 0))
    state = pl.BlockSpec((1, HG_H, HG_DV, HG_DK), lambda b, n: (b, 0, 0, 0))
    return pl.pallas_call(
        functools.partial(_hgrn_kernel, c=c, sub=sub, valid=valid),
        grid=(bsz, nc),
        in_specs=[col(0), col(1), col(2), col(3), vec, vec, pl.BlockSpec((c, c), lambda b, n: (0, 0)), state],
        out_specs=[pl.BlockSpec((c, MIX_W), lambda b, n: (b * nc + n, 0)), state],
        out_shape=[jax.ShapeDtypeStruct((bsz * seq_pad, MIX_W), BF16),
                   jax.ShapeDtypeStruct((bsz, HG_H, HG_DV, HG_DK), F32)],
        compiler_params=_cparams("parallel", "arbitrary"),
        name="hgrn2",
    )(z, z, z, z, lb.reshape(1, MIX_W), norm_g.reshape(1, MIX_W), tri, s0_t)


def _gmlp_kernel(u_ref, v_ref, lg_ref, lbias_ref, ws_ref, bs_ref, o_ref):
    vn = _layer_norm(jax.nn.gelu(v_ref[...]), lg_ref[...], lbias_ref[...]).astype(BF16)
    u = jax.nn.gelu(u_ref[...])
    r = lax.broadcasted_iota(jnp.int32, (GM_CHUNK, GM_CHUNK), 0)
    cc = lax.broadcasted_iota(jnp.int32, (GM_CHUNK, GM_CHUNK), 1)
    for g in range(GM_GROUPS):
        sl = slice(g * GM_GW, (g + 1) * GM_GW)
        ws = jnp.where(r >= cc, ws_ref[g], 0.0).astype(BF16)
        mixed = _dot(ws, vn[:, sl]) + bs_ref[g]
        o_ref[:, sl] = (u[:, sl] * mixed).astype(o_ref.dtype)


def _gmlp(z, col0, ln_g, ln_b, w_s, b_s):
    m = z.shape[0]
    assert col0 % MIX_W == 0
    j0 = col0 // MIX_W
    return pl.pallas_call(
        _gmlp_kernel,
        grid=(m // GM_CHUNK,),
        in_specs=[pl.BlockSpec((GM_CHUNK, MIX_W), lambda i: (i, j0)),
                  pl.BlockSpec((GM_CHUNK, MIX_W), lambda i: (i, j0 + 1)),
                  pl.BlockSpec((1, MIX_W), lambda i: (0, 0)),
                  pl.BlockSpec((1, MIX_W), lambda i: (0, 0)),
                  pl.BlockSpec((GM_GROUPS, GM_CHUNK, GM_CHUNK), lambda i: (0, 0, 0)),
                  pl.BlockSpec((GM_GROUPS, GM_CHUNK, 1), lambda i: (0, 0, 0))],
        out_specs=pl.BlockSpec((GM_CHUNK, MIX_W), lambda i: (i, 0)),
        out_shape=jax.ShapeDtypeStruct((m, MIX_W), BF16),
        compiler_params=_cparams("parallel"),
        name="gmlp",
    )(z, z, ln_g.reshape(1, MIX_W), ln_b.reshape(1, MIX_W), w_s, b_s[:, :, None])


def _gmlp_first_kernel(u_ref, v_ref, lg_ref, lbias_ref, w00_ref, b0_ref, o_ref, vn_ref):
    vn = _layer_norm(jax.nn.gelu(v_ref[...]), lg_ref[...], lbias_ref[...])
    vn_ref[...] = vn
    o_ref[...] = (jax.nn.gelu(u_ref[...]) * (vn * w00_ref[...] + b0_ref[...])).astype(o_ref.dtype)


def _gmlp_first(z, col0, ln_g, ln_b, w_s, b_s):
    m = z.shape[0]
    assert col0 % MIX_W == 0
    j0 = col0 // MIX_W
    w00 = jnp.repeat(w_s[:, 0, 0], GM_GW).reshape(1, MIX_W)
    b0 = jnp.repeat(b_s[:, 0], GM_GW).reshape(1, MIX_W)
    vec = pl.BlockSpec((1, MIX_W), lambda i: (0, 0))
    return pl.pallas_call(
        _gmlp_first_kernel,
        grid=(1,),
        in_specs=[pl.BlockSpec((m, MIX_W), lambda i: (0, j0)), pl.BlockSpec((m, MIX_W), lambda i: (0, j0 + 1)),
                  vec, vec, vec, vec],
        out_specs=[pl.BlockSpec((m, MIX_W), lambda i: (0, 0)), pl.BlockSpec((m, MIX_W), lambda i: (0, 0))],
        out_shape=[jax.ShapeDtypeStruct((m, MIX_W), BF16), jax.ShapeDtypeStruct((m, MIX_W), F32)],
        name="gmlp_first",
    )(z, z, ln_g.reshape(1, MIX_W), ln_b.reshape(1, MIX_W), w00, b0)


def _layer_weights(l, w_in, fox_bf, s5_lambda_re, s5_lambda_im, s5_log_dt, s5_b_re, s5_b_im, s5_c_re, s5_c_im,
                   s5_w_glu, hgrn_lb, w_branch, w_out, w_ffn_up, w_ffn_down):
    n_a = 3 * MIX_W + LANES
    o_r = 3 * MIX_W + FOX_H
    lb_all = jnp.cumsum(jax.nn.softmax(hgrn_lb.astype(F32), axis=0), axis=0)
    return dict(
        w_a=_cast_cols(w_in, l, 0, n_a),
        w_r=_cast_cols(w_in, l, o_r, w_in.shape[2] - o_r),
        bf=jnp.pad(fox_bf[l], (0, LANES - FOX_H)).reshape(1, LANES),
        s5=_s5_params(s5_lambda_re[l], s5_lambda_im[l], s5_log_dt[l], s5_b_re[l], s5_b_im[l],
                      s5_c_re[l], s5_c_im[l]),
        w_glu=s5_w_glu[l].astype(BF16),
        lb=lb_all[l] - lb_all[0],
        w_branch=w_branch[l].astype(BF16),
        w_out=w_out[l].astype(BF16),
        w_up=w_ffn_up[l].astype(BF16),
        w_down=w_ffn_down[l].astype(BF16),
    )


def _trunk_layer(x, x_bf, l, lw, P, *, bsz, seq, fox_fn, s5_h0, hg_s0):
    m = bsz * seq
    qkv_f32, qkv_bf = _proj(x_bf, lw["w_a"], n=3 * MIX_W, out_dtypes=(F32, BF16))
    logf = _proj(x_bf, lw["w_a"], col0=3 * MIX_W, n=LANES, act="log_sigmoid_bias", bias=lw["bf"])[0][:, :FOX_H]
    o_a = fox_fn(qkv_f32, qkv_bf, logf)

    z = _proj(x_bf, lw["w_r"], n=Z_GATE)[0]
    su_tb = z[:, :MIX_W].reshape(bsz, seq, MIX_W).transpose(1, 0, 2).reshape(m, MIX_W)
    o_b_tb, s5_re, s5_im = _s5(su_tb, lw["s5"], P["s5_d"][l], lw["w_glu"], P["s5_b_glu"][l],
                               s5_h0[0].reshape(bsz, S5_STATE), s5_h0[1].reshape(bsz, S5_STATE),
                               bk=bsz, steps=seq)
    o_b = o_b_tb.reshape(seq, bsz, MIX_W).transpose(1, 0, 2).reshape(m, MIX_W)

    seq_pad = seq if seq % HG_CHUNK == 0 else -(-seq // BF16_ROWS) * BF16_ROWS
    zh, zh_col0 = z, Z_HG
    if seq_pad != seq:
        zh = z[:, Z_HG:Z_GM].reshape(bsz, seq, 4 * MIX_W)
        zh, zh_col0 = jnp.pad(zh, ((0, 0), (0, seq_pad - seq), (0, 0))).reshape(-1, 4 * MIX_W), 0
    o_c, hg_t = _hgrn(zh, zh_col0, lw["lb"], P["hgrn_norm_g"][l], hg_s0.transpose(0, 1, 3, 2),
                      bsz=bsz, seq_pad=seq_pad, valid=min(seq, HG_CHUNK))
    if seq_pad != seq:
        o_c = o_c.reshape(bsz, seq_pad, MIX_W)[:, :seq].reshape(m, MIX_W)
    hg_state = hg_t.transpose(0, 1, 3, 2)

    if seq == 1:
        o_d, v_rows = _gmlp_first(z, Z_GM, P["gmlp_ln_g"][l], P["gmlp_ln_b"][l], P["gmlp_w_s"][l], P["gmlp_b_s"][l])
    else:
        o_d = _gmlp(z, Z_GM, P["gmlp_ln_g"][l], P["gmlp_ln_b"][l], P["gmlp_w_s"][l], P["gmlp_b_s"][l])
        v_rows = None

    merged = _merge(x_bf, (o_a, o_b, o_c, o_d), lw["w_r"], Z_GATE, lw["w_branch"])
    x1, x1_bf = _matmul_res_ln(merged, lw["w_out"], x, P["ln1_g"][l], P["ln1_b"][l])
    hff = _ffn_up(x1_bf, lw["w_up"])
    x2, x2_bf = _matmul_res_ln(hff, lw["w_down"], x1, P["ln2_g"][l], P["ln2_b"][l])

    fk = qkv_f32[:, MIX_W:2 * MIX_W].reshape(bsz, seq, FOX_H, FOX_HD)
    fv = qkv_f32[:, 2 * MIX_W:].reshape(bsz, seq, FOX_H, FOX_HD)
    state = (fk, fv, logf.reshape(bsz, seq, FOX_H), s5_re.reshape(bsz, S5_G, S5_P),
             s5_im.reshape(bsz, S5_G, S5_P), hg_state, v_rows)
    return x2, x2_bf, state


def kernel(x_prompt, x_sample, cache_k, cache_v, cache_logf, page_table, state_s5_re, state_s5_im, state_hgrn,
           w_in, fox_bf, s5_lambda_re, s5_lambda_im, s5_log_dt, s5_b_re, s5_b_im, s5_c_re, s5_c_im, s5_d,
           s5_w_glu, s5_b_glu, hgrn_lb, hgrn_norm_g, gmlp_ln_g, gmlp_ln_b, gmlp_w_s, gmlp_b_s, w_branch, w_out,
           ln1_g, ln1_b, w_ffn_up, w_ffn_down, ln2_g, ln2_b):
    b_p, l_p, _ = x_prompt.shape
    b_s, l_s, _ = x_sample.shape
    assert l_s == 1, "the sample group decodes one token per sequence"
    P = dict(s5_d=s5_d, s5_b_glu=s5_b_glu, hgrn_norm_g=hgrn_norm_g, gmlp_ln_g=gmlp_ln_g, gmlp_ln_b=gmlp_ln_b,
             gmlp_w_s=gmlp_w_s, gmlp_b_s=gmlp_b_s, ln1_g=ln1_g, ln1_b=ln1_b, ln2_g=ln2_g, ln2_b=ln2_b)

    xp = x_prompt.reshape(b_p * l_p, D_MODEL)
    xs = x_sample.reshape(b_s * l_s, D_MODEL)
    xp_bf, xs_bf = xp.astype(BF16), xs.astype(BF16)
    s5_zero = jnp.zeros((b_p, S5_G, S5_P), F32)
    hg_zero = jnp.zeros((b_p, HG_H, HG_DK, HG_DV), F32)
    p_states, s_states = [], []
    for l in range(DEPTH):
        lw = _layer_weights(l, w_in, fox_bf, s5_lambda_re, s5_lambda_im, s5_log_dt, s5_b_re, s5_b_im,
                            s5_c_re, s5_c_im, s5_w_glu, hgrn_lb, w_branch, w_out, w_ffn_up, w_ffn_down)

        def fox_p(qkv_f32, qkv_bf, logf):
            c = _cumsum_lanes(logf.reshape(b_p, l_p, FOX_H).transpose(0, 2, 1).reshape(b_p * FOX_H, l_p))
            return _fox_prompt(qkv_bf, c.reshape(b_p, FOX_H, l_p), b_p, l_p)

        def fox_s(qkv_f32, qkv_bf, logf, l=l):
            o = _fox_decode(page_table, qkv_f32[:, :MIX_W], qkv_f32[:, MIX_W:2 * MIX_W], qkv_f32[:, 2 * MIX_W:],
                            logf, cache_k, cache_v, cache_logf, l)
            return o.astype(BF16)

        xp, xp_bf, sp = _trunk_layer(xp, xp_bf, l, lw, P, bsz=b_p, seq=l_p, fox_fn=fox_p,
                                     s5_h0=(s5_zero, s5_zero), hg_s0=hg_zero)
        xs, xs_bf, ss = _trunk_layer(xs, xs_bf, l, lw, P, bsz=b_s, seq=l_s, fox_fn=fox_s,
                                     s5_h0=(state_s5_re[l], state_s5_im[l]), hg_s0=state_hgrn[l])
        p_states.append(sp)
        s_states.append(ss)

    stack = lambda states, i: jnp.stack([s[i] for s in states])
    return (xp.reshape(b_p, l_p, D_MODEL), xs.reshape(b_s, l_s, D_MODEL),
            stack(p_states, 0), stack(p_states, 1), stack(p_states, 2),
            stack(s_states, 0), stack(s_states, 1), stack(s_states, 2),
            stack(p_states, 3), stack(p_states, 4), stack(s_states, 3), stack(s_states, 4),
            stack(p_states, 5), stack(s_states, 5),
            jnp.stack([s[6].reshape(b_s, l_s, MIX_W) for s in s_states]))
```

```python
import functools
import math

import numpy as np
import jax
import jax.numpy as jnp
from jax import lax
from jax.experimental import pallas as pl
from jax.experimental.pallas import tpu as pltpu

F32 = jnp.float32
BF16 = jnp.bfloat16

D_MODEL = 2048
DEPTH = 2
N_BRANCH = 4
MIX_W = D_MODEL // N_BRANCH
FOX_HD = 128
FOX_H = MIX_W // FOX_HD
S5_GROUP = 16
S5_G = MIX_W // S5_GROUP
S5_P = 64
S5_STATE = S5_G * S5_P
HG_DK = 128
HG_DV = 128
HG_H = MIX_W // HG_DV
HG_CHUNK = 64
HG_SUB = 16
HG_SAFE_RANGE = 44.0
GM_CHUNK = 128
GM_GROUPS = 4
GM_GW = MIX_W // GM_GROUPS
D_FF = ((8 * D_MODEL // 3 + 255) // 256) * 256
DEEPNORM_ALPHA = (2 * DEPTH) ** 0.25
LN_EPS = 1e-5
RMS_EPS = 1e-6
LANES = 128
SUBLANES = 8
BF16_ROWS = 16
FOX_DECODE_PAGES = 16
Z_HG = MIX_W
Z_GM = Z_HG + 4 * MIX_W
Z_GATE = Z_GM + 2 * MIX_W
VMEM_LIMIT = 48 * 1024 * 1024
VMEM_LIMIT_BIG = 58 * 1024 * 1024
PROJ_TN_MAX = 1792


def _cparams(*sem):
    return pltpu.CompilerParams(dimension_semantics=sem, vmem_limit_bytes=VMEM_LIMIT)


def _log_sigmoid(x):
    return jnp.minimum(x, 0.0) - jnp.log1p(jnp.exp(-jnp.abs(x)))


def _layer_norm(x, g, b):
    mu = jnp.mean(x, axis=-1, keepdims=True)
    xc = x - mu
    var = jnp.mean(xc * xc, axis=-1, keepdims=True)
    return xc * lax.rsqrt(var + LN_EPS) * g + b


def _split3(x):
    hi = x.astype(BF16)
    r1 = x - hi.astype(F32)
    mid = r1.astype(BF16)
    lo = (r1 - mid.astype(F32)).astype(BF16)
    return hi, mid, lo


def _dot(a, b):
    return jnp.dot(a, b, preferred_element_type=F32)


def _dot_nt(a, b):
    return lax.dot_general(a, b, (((1,), (1,)), ((), ())), preferred_element_type=F32)


def _dot_tn(a, b):
    return lax.dot_general(a, b, (((0,), (0,)), ((), ())), preferred_element_type=F32)


def _cast_cols_kernel(w_ref, o_ref, *, l, depth):
    n_kt = o_ref.shape[0] // LANES
    for kt in range(n_kt):
        tile = w_ref[pl.ds(kt * depth + l, LANES, stride=n_kt * depth), :]
        o_ref[kt * LANES:(kt + 1) * LANES, :] = tile.T.astype(o_ref.dtype)


def _cast_cols(w, l, col0, n):
    depth, k, n_all = w.shape
    rows_per_col = (k // LANES) * depth
    view = w.reshape(depth, k // LANES, LANES, n_all).transpose(3, 1, 0, 2).reshape(n_all * rows_per_col, LANES)
    blk = LANES * rows_per_col
    return pl.pallas_call(
        functools.partial(_cast_cols_kernel, l=l, depth=depth),
        grid=(n // LANES,),
        in_specs=[pl.BlockSpec((pl.Element(blk), pl.Element(LANES)),
                               lambda j: (pl.multiple_of(col0 * rows_per_col + j * blk, rows_per_col), 0))],
        out_specs=pl.BlockSpec((k, LANES), lambda j: (0, j)),
        out_shape=jax.ShapeDtypeStruct((k, n), BF16),
        compiler_params=_cparams("parallel"),
        name="cast_cols",
    )(view)


def _proj_kernel(x_ref, w_ref, b_ref, *out_refs, act):
    z = _dot(x_ref[...], w_ref[...])
    if act == "gelu":
        z = jax.nn.gelu(z)
    elif act == "sigmoid":
        z = jax.nn.sigmoid(z)
    elif act == "log_sigmoid_bias":
        z = _log_sigmoid(z + b_ref[...])
    for o in out_refs:
        o[...] = z.astype(o.dtype)


def _proj(x, w, *, col0=0, n=None, act=None, bias=None, out_dtypes=(F32,)):
    m, k = x.shape
    n = w.shape[1] if n is None else n
    tm = min(m, 512)
    tn = max(t for t in range(LANES, min(n, PROJ_TN_MAX) + 1, LANES) if n % t == 0 and col0 % t == 0)
    j0 = col0 // tn
    if bias is None:
        bias = jnp.zeros((1, n), F32)
    return pl.pallas_call(
        functools.partial(_proj_kernel, act=act),
        grid=(n // tn, m // tm),
        in_specs=[pl.BlockSpec((tm, k), lambda j, i: (i, 0)),
                  pl.BlockSpec((k, tn), lambda j, i: (0, j0 + j)),
                  pl.BlockSpec((1, tn), lambda j, i: (0, j))],
        out_specs=[pl.BlockSpec((tm, tn), lambda j, i: (i, j)) for _ in out_dtypes],
        out_shape=[jax.ShapeDtypeStruct((m, n), d) for d in out_dtypes],
        compiler_params=_cparams("parallel", "arbitrary"),
        name="proj_" + (act or "id"),
    )(x, w, bias)


def _merge_kernel(x_ref, oa_ref, ob_ref, oc_ref, od_ref, g0_ref, g1_ref, g2_ref, g3_ref, wb_ref, o_ref):
    x = x_ref[...]
    acc = None
    for br, gr, k in ((oa_ref, g0_ref, 0), (ob_ref, g1_ref, 1), (oc_ref, g2_ref, 2), (od_ref, g3_ref, 3)):
        t = jax.nn.sigmoid(_dot(x, gr[...])) * _dot(br[...], wb_ref[k])
        acc = t if acc is None else acc + t
    o_ref[...] = acc.astype(o_ref.dtype)


def _merge(x, branches, w, gate_col0, w_branch):
    m, k_in = x.shape
    tm = min(m, 512)
    tn = 512
    nb = D_MODEL // tn
    assert gate_col0 % tn == 0
    j0 = gate_col0 // tn
    gate_specs = [pl.BlockSpec((k_in, tn), functools.partial(lambda j, i, k: (0, j0 + k * nb + j), k=k))
                  for k in range(N_BRANCH)]
    return pl.pallas_call(
        _merge_kernel,
        grid=(nb, m // tm),
        in_specs=[pl.BlockSpec((tm, k_in), lambda j, i: (i, 0))]
                 + [pl.BlockSpec((tm, MIX_W), lambda j, i: (i, 0)) for _ in range(N_BRANCH)] + gate_specs
                 + [pl.BlockSpec((N_BRANCH, MIX_W, tn), lambda j, i: (0, 0, j))],
        out_specs=pl.BlockSpec((tm, tn), lambda j, i: (i, j)),
        out_shape=jax.ShapeDtypeStruct((m, D_MODEL), BF16),
        compiler_params=_cparams("parallel", "arbitrary"),
        name="merge",
    )(x, *branches, w, w, w, w, w_branch)


def _res_ln_kernel(a_ref, w_ref, x_ref, g_ref, b_ref, y_ref, ybf_ref):
    y = _layer_norm(DEEPNORM_ALPHA * x_ref[...] + _dot(a_ref[...], w_ref[...]), g_ref[...], b_ref[...])
    y_ref[...] = y
    ybf_ref[...] = y.astype(BF16)


def _matmul_res_ln(a, w, x, g, b):
    m, k = a.shape
    n = w.shape[1]
    tm = min(m, 256)
    return pl.pallas_call(
        _res_ln_kernel,
        grid=(m // tm,),
        in_specs=[pl.BlockSpec((tm, k), lambda i: (i, 0)),
                  pl.BlockSpec((k, n), lambda i: (0, 0), pipeline_mode=pl.Buffered(1)),
                  pl.BlockSpec((tm, n), lambda i: (i, 0)),
                  pl.BlockSpec((1, n), lambda i: (0, 0)),
                  pl.BlockSpec((1, n), lambda i: (0, 0))],
        out_specs=[pl.BlockSpec((tm, n), lambda i: (i, 0)),
                   pl.BlockSpec((tm, n), lambda i: (i, 0))],
        out_shape=[jax.ShapeDtypeStruct((m, n), F32), jax.ShapeDtypeStruct((m, n), BF16)],
        compiler_params=pltpu.CompilerParams(dimension_semantics=("parallel",), vmem_limit_bytes=VMEM_LIMIT_BIG),
        name="matmul_res_ln",
    )(a, w, x, g.reshape(1, n), b.reshape(1, n))


def _ffn_up_kernel(x_ref, wg_ref, wu_ref, o_ref):
    x = x_ref[...]
    o_ref[...] = (jax.nn.silu(_dot(x, wg_ref[...])) * _dot(x, wu_ref[...])).astype(o_ref.dtype)


def _ffn_up(x, w_up):
    m, k = x.shape
    tm = min(m, 1024)
    tn = 512
    nb = D_FF // tn
    return pl.pallas_call(
        _ffn_up_kernel,
        grid=(nb, m // tm),
        in_specs=[pl.BlockSpec((tm, k), lambda j, i: (i, 0)),
                  pl.BlockSpec((k, tn), lambda j, i: (0, j)),
                  pl.BlockSpec((k, tn), lambda j, i: (0, nb + j))],
        out_specs=pl.BlockSpec((tm, tn), lambda j, i: (i, j)),
        out_shape=jax.ShapeDtypeStruct((m, D_FF), BF16),
        compiler_params=_cparams("parallel", "arbitrary"),
        name="ffn_up",
    )(x, w_up, w_up)


def _cumsum_lanes_kernel(x_ref, o_ref):
    x = x_ref[...]
    n = x.shape[-1]
    lane = lax.broadcasted_iota(jnp.int32, x.shape, 1)
    sh = 1
    while sh < n:
        x = x + jnp.where(lane >= sh, pltpu.roll(x, sh, axis=1), 0.0)
        sh *= 2
    o_ref[...] = x


def _cumsum_lanes(x):
    return pl.pallas_call(_cumsum_lanes_kernel, out_shape=jax.ShapeDtypeStruct(x.shape, F32),
                          name="cumsum_lanes")(x)


def _flash_kernel(q_ref, k_ref, v_ref, cq_ref, ck_ref, o_ref, qt_sc, m_sc, l_sc, acc_sc, *, tq, tk, scale):
    qi = pl.program_id(2)
    ki = pl.program_id(3)

    @pl.when(ki == 0)
    def _():
        qt_sc[...] = q_ref[...].astype(F32).T.astype(BF16)
        m_sc[...] = jnp.full_like(m_sc, -jnp.inf)
        l_sc[...] = jnp.zeros_like(l_sc)
        acc_sc[...] = jnp.zeros_like(acc_sc)

    def update(masked):
        s = _dot(k_ref[...], qt_sc[...]) * scale
        s = s + cq_ref[0, 0] - jnp.concatenate([ck_ref[0, 0]] * (tq // LANES), axis=1)
        if masked:
            k_pos = lax.broadcasted_iota(jnp.int32, (tk, tq), 0)
            q_pos = lax.broadcasted_iota(jnp.int32, (tk, tq), 1)
            s = jnp.where(k_pos <= q_pos, s, -jnp.inf)
        m_prev = m_sc[...]
        m_new = jnp.maximum(m_prev, jnp.max(s, axis=0, keepdims=True))
        alpha = jnp.exp(m_prev - m_new)
        p = jnp.exp(s - m_new)
        l_sc[...] = alpha * l_sc[...] + jnp.sum(p, axis=0, keepdims=True)
        acc_sc[...] = alpha * acc_sc[...] + _dot_tn(v_ref[...], p.astype(BF16))
        m_sc[...] = m_new

    pl.when(ki < qi)(functools.partial(update, False))
    pl.when(ki == qi)(functools.partial(update, True))

    @pl.when(ki == pl.num_programs(3) - 1)
    def _():
        o_ref[...] = (acc_sc[...] / l_sc[...]).T.astype(o_ref.dtype)


def _fox_prompt(qkv, c, bsz, seq):
    tq = tk = min(seq, 512)
    nq = seq // tq
    nk = seq // tk
    cq = c.reshape(bsz, FOX_H, 1, seq)
    ck = jnp.broadcast_to(c[..., None], (bsz, FOX_H, seq, LANES))

    def kv_blk(qi, ki):
        return jnp.minimum(ki, qi)

    return pl.pallas_call(
        functools.partial(_flash_kernel, tq=tq, tk=tk, scale=FOX_HD ** -0.5),
        grid=(bsz, FOX_H, nq, nk),
        in_specs=[pl.BlockSpec((tq, FOX_HD), lambda b, h, qi, ki: (b * nq + qi, h)),
                  pl.BlockSpec((tk, FOX_HD), lambda b, h, qi, ki: (b * nk + kv_blk(qi, ki), FOX_H + h)),
                  pl.BlockSpec((tk, FOX_HD), lambda b, h, qi, ki: (b * nk + kv_blk(qi, ki), 2 * FOX_H + h)),
                  pl.BlockSpec((1, 1, 1, tq), lambda b, h, qi, ki: (b, h, 0, qi)),
                  pl.BlockSpec((1, 1, tk, LANES), lambda b, h, qi, ki: (b, h, kv_blk(qi, ki), 0))],
        out_specs=pl.BlockSpec((tq, FOX_HD), lambda b, h, qi, ki: (b * nq + qi, h)),
        out_shape=jax.ShapeDtypeStruct((bsz * seq, MIX_W), BF16),
        scratch_shapes=[pltpu.VMEM((FOX_HD, tq), BF16), pltpu.VMEM((1, tq), F32), pltpu.VMEM((1, tq), F32),
                        pltpu.VMEM((FOX_HD, tq), F32)],
        compiler_params=_cparams("parallel", "parallel", "parallel", "arbitrary"),
        name="fox_prompt",
    )(qkv, qkv, qkv, cq, ck)


def _fox_decode_kernel(pt_ref, q_ref, cnew_ref, knew_ref, vnew_ref, *rest, scale, page, group):
    k_refs, v_refs, lf_refs = rest[:group], rest[group:2 * group], rest[2 * group:3 * group]
    o_ref, m_sc, l_sc, acc_sc, carry_sc = rest[3 * group:]
    p = pl.program_id(1)
    hrow = lax.broadcasted_iota(jnp.int32, (BF16_ROWS, MIX_W), 0)
    lane = lax.broadcasted_iota(jnp.int32, (BF16_ROWS, MIX_W), 1)
    hcol = lax.broadcasted_iota(jnp.int32, (BF16_ROWS, 1), 0)
    later = jnp.where(lax.broadcasted_iota(jnp.int32, (page, page), 0)
                      > lax.broadcasted_iota(jnp.int32, (page, page), 1), 1.0, 0.0).astype(BF16)

    @pl.when(p == 0)
    def _():
        m_sc[...] = jnp.full_like(m_sc, -jnp.inf)
        l_sc[...] = jnp.zeros_like(l_sc)
        acc_sc[...] = jnp.zeros_like(acc_sc)
        carry_sc[...] = jnp.zeros_like(carry_sc)

    q = q_ref[0]
    carry = carry_sc[...]
    cnew = cnew_ref[0]
    parts = []
    for j in range(group):
        lf = lf_refs[j][...]
        lf16 = jnp.zeros((BF16_ROWS, page), F32)
        for h in range(FOX_H):
            lf16 = jnp.where(hcol == h, lf[h:h + 1, :], lf16)
        hi, mid, lo = _split3(lf16)
        suffix = _dot(hi, later) + _dot(mid, later) + _dot(lo, later)
        s = None
        for h in range(FOX_H):
            kh = k_refs[j][pl.ds(h, page, stride=FOX_H), :]
            t = _dot_nt(q[:, h * FOX_HD:(h + 1) * FOX_HD], kh.astype(BF16))
            s = t if s is None else s + t
        parts.append(s * scale + (suffix + carry + cnew))
        carry = carry + jnp.sum(lf16, axis=-1, keepdims=True)
    carry_sc[...] = carry
    s = jnp.concatenate(parts, axis=1)

    m_prev = m_sc[...]
    m_new = jnp.maximum(m_prev, jnp.max(s, axis=-1, keepdims=True))
    alpha = jnp.exp(m_prev - m_new)
    pr = jnp.exp(s - m_new)
    l_sc[...] = alpha * l_sc[...] + jnp.sum(pr, axis=-1, keepdims=True)
    pr = pr.astype(BF16)
    for h in range(FOX_H):
        cols = slice(h * FOX_HD, (h + 1) * FOX_HD)
        upd = None
        for j in range(group):
            vh = v_refs[j][pl.ds(h, page, stride=FOX_H), :]
            t = _dot(pr[:, j * page:(j + 1) * page], vh.astype(BF16))
            upd = t if upd is None else upd + t
        acc_sc[:, cols] = alpha * acc_sc[:, cols] + upd
    m_sc[...] = m_new

    @pl.when(p == pl.num_programs(1) - 1)
    def _():
        kn = knew_ref[0].astype(BF16).astype(F32)
        vn = vnew_ref[0].astype(BF16).astype(F32)
        s_new = jnp.sum(q.astype(F32) * kn, axis=-1, keepdims=True) * scale
        m_prev = m_sc[...]
        m_fin = jnp.maximum(m_prev, s_new)
        alpha = jnp.exp(m_prev - m_fin)
        p_new = jnp.exp(s_new - m_fin)
        l_fin = alpha * l_sc[...] + p_new
        acc = alpha * acc_sc[...] + p_new.astype(BF16).astype(F32) * vn
        o = acc / l_fin
        o_ref[0] = jnp.sum(jnp.where((lane >> 7) == hrow, o, 0.0), axis=0, keepdims=True)


def _fox_decode(page_table, q, k_new, v_new, c_new, cache_k, cache_v, cache_lf, l):
    bsz, n_pages = page_table.shape
    depth, n_phys, page = cache_k.shape[:3]
    group = math.gcd(n_pages, FOX_DECODE_PAGES)
    cache_k = cache_k.reshape(depth, n_phys, page * FOX_H, FOX_HD)
    cache_v = cache_v.reshape(depth, n_phys, page * FOX_H, FOX_HD)
    cache_lf = cache_lf.transpose(0, 1, 3, 2)
    hmask = (np.arange(MIX_W)[None, :] // FOX_HD) == np.arange(BF16_ROWS)[:, None]
    q_rows = jnp.where(hmask[None], q[:, None, :], 0.0).astype(BF16)
    c_rows = jnp.pad(c_new, ((0, 0), (0, BF16_ROWS - FOX_H)))[:, :, None]
    pt = page_table.reshape(-1)

    def pg(j):
        return lambda b, p, pt_ref: pt_ref[b * n_pages + (n_pages - 1 - (p * group + j))]

    def kv_spec(j):
        return pl.BlockSpec((None, None, page * FOX_H, FOX_HD),
                            lambda b, p, pt_ref: (l, pg(j)(b, p, pt_ref), 0, 0))

    def lf_spec(j):
        return pl.BlockSpec((None, None, FOX_H, page), lambda b, p, pt_ref: (l, pg(j)(b, p, pt_ref), 0, 0))

    grid_spec = pltpu.PrefetchScalarGridSpec(
        num_scalar_prefetch=1,
        grid=(bsz, n_pages // group),
        in_specs=[pl.BlockSpec((1, BF16_ROWS, MIX_W), lambda b, p, pt_ref: (b, 0, 0)),
                  pl.BlockSpec((1, BF16_ROWS, 1), lambda b, p, pt_ref: (b, 0, 0)),
                  pl.BlockSpec((1, 1, MIX_W), lambda b, p, pt_ref: (b, 0, 0)),
                  pl.BlockSpec((1, 1, MIX_W), lambda b, p, pt_ref: (b, 0, 0))]
                 + [kv_spec(j) for j in range(group)] + [kv_spec(j) for j in range(group)]
                 + [lf_spec(j) for j in range(group)],
        out_specs=pl.BlockSpec((1, 1, MIX_W), lambda b, p, pt_ref: (b, 0, 0)),
        scratch_shapes=[pltpu.VMEM((BF16_ROWS, 1), F32), pltpu.VMEM((BF16_ROWS, 1), F32),
                        pltpu.VMEM((BF16_ROWS, MIX_W), F32), pltpu.VMEM((BF16_ROWS, 1), F32)],
    )
    out = pl.pallas_call(
        functools.partial(_fox_decode_kernel, scale=FOX_HD ** -0.5, page=page, group=group),
        grid_spec=grid_spec,
        out_shape=jax.ShapeDtypeStruct((bsz, 1, MIX_W), F32),
        compiler_params=_cparams("parallel", "arbitrary"),
        name="fox_decode",
    )(pt, q_rows, c_rows, k_new[:, None, :], v_new[:, None, :],
      *([cache_k] * group), *([cache_v] * group), *([cache_lf] * group))
    return out.reshape(bsz, MIX_W)


def _s5_kernel(u_ref, bblk_ref, cblk_ref, are_ref, aim_ref, d_ref, wglu_ref, bglu_ref, h0re_ref, h0im_ref,
               o_ref, hre_ref, him_ref, hs_sc, *, tc, bk, slab):
    c = pl.program_id(0)

    @pl.when(c == 0)
    def _():
        hre_ref[...] = h0re_ref[...]
        him_ref[...] = h0im_ref[...]

    u = u_ref[...]
    hs_sc[...] = _dot(u.astype(BF16), bblk_ref[...])
    for s0 in range(0, S5_STATE, slab):
        re_sl = slice(s0, s0 + slab)
        im_sl = slice(S5_STATE + s0, S5_STATE + s0 + slab)
        ar = jnp.broadcast_to(are_ref[:, re_sl], (bk, slab))
        ai = jnp.broadcast_to(aim_ref[:, re_sl], (bk, slab))

        def step(t, carry, re_sl=re_sl, im_sl=im_sl, ar=ar, ai=ai):
            hr, hi = carry
            rows = pl.ds(pl.multiple_of(t * bk, bk), bk)
            nr = ar * hr - ai * hi + hs_sc[rows, re_sl]
            ni = ar * hi + ai * hr + hs_sc[rows, im_sl]
            hs_sc[rows, re_sl] = nr
            hs_sc[rows, im_sl] = ni
            return nr, ni

        hr, hi = lax.fori_loop(0, tc, step, (hre_ref[:, re_sl], him_ref[:, re_sl]))
        hre_ref[:, re_sl] = hr
        him_ref[:, re_sl] = hi
    y = _dot(hs_sc[...].astype(BF16), cblk_ref[...]) + d_ref[...] * u
    g = jax.nn.gelu(y)
    gate = jax.nn.sigmoid(_dot(g.astype(BF16), wglu_ref[...]) + bglu_ref[...])
    o_ref[...] = (g * gate).astype(o_ref.dtype)


def _s5_params(lam_re, lam_im, log_dt, b_re, b_im, c_re, c_im):
    dt = jnp.exp(log_dt)[:, None]
    mag = jnp.exp(lam_re * dt)
    ab_re, ab_im = mag * jnp.cos(lam_im * dt), mag * jnp.sin(lam_im * dt)
    den = lam_re * lam_re + lam_im * lam_im
    nr, ni = ab_re - 1.0, ab_im
    coef_re = (nr * lam_re + ni * lam_im) / den
    coef_im = (ni * lam_re - nr * lam_im) / den
    bb_re = coef_re[..., None] * b_re - coef_im[..., None] * b_im
    bb_im = coef_re[..., None] * b_im + coef_im[..., None] * b_re
    eye = jnp.eye(S5_G, dtype=F32)
    blk_b = lambda t: jnp.einsum("gph,gk->ghkp", t, eye).reshape(MIX_W, S5_STATE)
    blk_c = lambda t: jnp.einsum("ghp,gk->gpkh", t, eye).reshape(S5_STATE, MIX_W)
    bblk = jnp.concatenate([blk_b(bb_re), blk_b(bb_im)], axis=1).astype(BF16)
    cblk = jnp.concatenate([blk_c(c_re), blk_c(-c_im)], axis=0).astype(BF16)
    return ab_re.reshape(1, S5_STATE), ab_im.reshape(1, S5_STATE), bblk, cblk


def _s5(u_tb, params, d, w_glu, b_glu, h0_re, h0_im, *, bk, steps):
    a_re, a_im, bblk, cblk = params
    tc = min(steps, 64)
    rows = tc * bk
    const = lambda shape: pl.BlockSpec(shape, lambda c: (0,) * len(shape))
    return pl.pallas_call(
        functools.partial(_s5_kernel, tc=tc, bk=bk, slab=512),
        grid=(steps // tc,),
        in_specs=[pl.BlockSpec((rows, MIX_W), lambda c: (c, 0)),
                  const((MIX_W, 2 * S5_STATE)), const((2 * S5_STATE, MIX_W)),
                  const((1, S5_STATE)), const((1, S5_STATE)), const((1, MIX_W)),
                  const((MIX_W, MIX_W)), const((1, MIX_W)),
                  const((bk, S5_STATE)), const((bk, S5_STATE))],
        out_specs=[pl.BlockSpec((rows, MIX_W), lambda c: (c, 0)),
                   const((bk, S5_STATE)), const((bk, S5_STATE))],
        out_shape=[jax.ShapeDtypeStruct((steps * bk, MIX_W), BF16),
                   jax.ShapeDtypeStruct((bk, S5_STATE), F32), jax.ShapeDtypeStruct((bk, S5_STATE), F32)],
        scratch_shapes=[pltpu.VMEM((rows, 2 * S5_STATE), F32)],
        compiler_params=_cparams("arbitrary"),
        name="s5",
    )(u_tb, bblk, cblk, a_re, a_im, d.reshape(1, MIX_W), w_glu, b_glu.reshape(1, MIX_W), h0_re, h0_im)


def _hgrn_kernel(q_ref, f_ref, i_ref, g_ref, lb_ref, ng_ref, tri_ref, s0_ref, o_ref, st_ref, *, c, sub, valid):
    n = pl.program_id(1)

    @pl.when(n == 0)
    def _():
        st_ref[...] = s0_ref[...]

    lb = lb_ref[...]
    la = jnp.log(lb)
    lbb = jnp.log1p(-lb) + _log_sigmoid(f_ref[...])
    lf = jnp.maximum(la, lbb) + jnp.log1p(jnp.exp(-jnp.abs(la - lbb)))
    if valid < c:
        lf = jnp.where(lax.broadcasted_iota(jnp.int32, lf.shape, 0) < valid, lf, 0.0)
    kt_all = 1.0 - jnp.exp(lf)
    bc_parts = _dot(tri_ref[...], jnp.concatenate(_split3(lf), axis=1))
    bc_all = bc_parts[:, :MIX_W] + bc_parts[:, MIX_W:2 * MIX_W] + bc_parts[:, 2 * MIX_W:]
    mid = bc_all[c // 2 - 1:c // 2]
    safe = jnp.max(jnp.maximum(-mid, mid - bc_all[c - 1:c])) < HG_SAFE_RANGE

    def run(exact):
        for h in range(HG_H):
            cols = slice(h * HG_DK, (h + 1) * HG_DK)
            o, st = _hgrn_head(q_ref[:, cols], kt_all[:, cols], bc_all[:, cols], i_ref[:, cols], st_ref[0, h],
                               c=c, sub=sub, exact=exact)
            st_ref[0, h] = st
            o = o * lax.rsqrt(jnp.mean(o * o, axis=-1, keepdims=True) + RMS_EPS) * ng_ref[:, cols]
            o_ref[:, cols] = (o * jax.nn.silu(g_ref[:, cols])).astype(o_ref.dtype)

    pl.when(safe)(functools.partial(run, False))
    pl.when(jnp.logical_not(safe))(functools.partial(run, True))


def _hgrn_head(q, kt, bc, v, st, *, c, sub, exact):
    v_bf = v.astype(BF16)
    o = _dot_nt((q * jnp.exp(bc)).astype(BF16), st.astype(BF16))
    if not exact:
        rel = bc - bc[c // 2 - 1:c // 2]
        att = _dot_nt((q * jnp.exp(rel)).astype(BF16), (kt * jnp.exp(-rel)).astype(BF16))
        t_pos = lax.broadcasted_iota(jnp.int32, (c, c), 0)
        s_pos = lax.broadcasted_iota(jnp.int32, (c, c), 1)
        o = o + _dot(jnp.where(s_pos <= t_pos, att, 0.0).astype(BF16), v_bf)
    else:
        nb = c // sub
        rows = lax.broadcasted_iota(jnp.int32, (sub, HG_DK), 0)
        blocks = []
        for i in range(nb):
            lo_r, hi_r = i * sub, (i + 1) * sub
            bci = bc[lo_r:hi_r]
            qi = q[lo_r:hi_r]
            oi = o[lo_r:hi_r]
            if i > 0:
                e = bc[lo_r - 1:lo_r]
                qs = (qi * jnp.exp(bci - e)).astype(BF16)
                ks = (kt[:lo_r] * jnp.exp(e - bc[:lo_r])).astype(BF16)
                oi = oi + _dot(_dot_nt(qs, ks).astype(BF16), v_bf[:lo_r])
            for s in range(sub):
                r = lo_r + s
                dec = jnp.exp(jnp.where(rows >= s, bci - bc[r:r + 1], -jnp.inf))
                a = jnp.sum(qi * dec * kt[r:r + 1], axis=-1, keepdims=True)
                oi = oi + a * v[r:r + 1]
            blocks.append(oi)
        o = jnp.concatenate(blocks, axis=0) if nb > 1 else blocks[0]
    bl = bc[c - 1:c]
    st = jnp.exp(bl) * st + _dot_tn(v_bf, (kt * jnp.exp(bl - bc)).astype(BF16))
    return o, st


def _hgrn(z, col0, lb, norm_g, s0_t, *, bsz, seq_pad, valid):
    c = min(HG_CHUNK, seq_pad)
    sub = min(HG_SUB, c)
    nc = seq_pad // c
    assert col0 % MIX_W == 0
    j0 = col0 // MIX_W
    tri = jnp.asarray(np.tril(np.ones((c, c))), BF16)
    col = lambda k: pl.BlockSpec((c, MIX_W), lambda b, n: (b * nc + n, j0 + k))
    vec = pl.BlockSpec((1, MIX_W), lambda b, n: (0, 0))
    state = pl.BlockSpec((1, HG_H, HG_DV, HG_DK), lambda b, n: (b, 0, 0, 0))
    return pl.pallas_call(
        functools.partial(_hgrn_kernel, c=c, sub=sub, valid=valid),
        grid=(bsz, nc),
        in_specs=[col(0), col(1), col(2), col(3), vec, vec, pl.BlockSpec((c, c), lambda b, n: (0, 0)), state],
        out_specs=[pl.BlockSpec((c, MIX_W), lambda b, n: (b * nc + n, 0)), state],
        out_shape=[jax.ShapeDtypeStruct((bsz * seq_pad, MIX_W), BF16),
                   jax.ShapeDtypeStruct((bsz, HG_H, HG_DV, HG_DK), F32)],
        compiler_params=_cparams("parallel", "arbitrary"),
        name="hgrn2",
    )(z, z, z, z, lb.reshape(1, MIX_W), norm_g.reshape(1, MIX_W), tri, s0_t)


def _gmlp_kernel(u_ref, v_ref, lg_ref, lbias_ref, ws_ref, bs_ref, o_ref):
    vn = _layer_norm(jax.nn.gelu(v_ref[...]), lg_ref[...], lbias_ref[...]).astype(BF16)
    u = jax.nn.gelu(u_ref[...])
    r = lax.broadcasted_iota(jnp.int32, (GM_CHUNK, GM_CHUNK), 0)
    cc = lax.broadcasted_iota(jnp.int32, (GM_CHUNK, GM_CHUNK), 1)
    for g in range(GM_GROUPS):
        sl = slice(g * GM_GW, (g + 1) * GM_GW)
        ws = jnp.where(r >= cc, ws_ref[g], 0.0).astype(BF16)
        mixed = _dot(ws, vn[:, sl]) + bs_ref[g]
        o_ref[:, sl] = (u[:, sl] * mixed).astype(o_ref.dtype)


def _gmlp(z, col0, ln_g, ln_b, w_s, b_s):
    m = z.shape[0]
    assert col0 % MIX_W == 0
    j0 = col0 // MIX_W
    return pl.pallas_call(
        _gmlp_kernel,
        grid=(m // GM_CHUNK,),
        in_specs=[pl.BlockSpec((GM_CHUNK, MIX_W), lambda i: (i, j0)),
                  pl.BlockSpec((GM_CHUNK, MIX_W), lambda i: (i, j0 + 1)),
                  pl.BlockSpec((1, MIX_W), lambda i: (0, 0)),
                  pl.BlockSpec((1, MIX_W), lambda i: (0, 0)),
                  pl.BlockSpec((GM_GROUPS, GM_CHUNK, GM_CHUNK), lambda i: (0, 0, 0)),
                  pl.BlockSpec((GM_GROUPS, GM_CHUNK, 1), lambda i: (0, 0, 0))],
        out_specs=pl.BlockSpec((GM_CHUNK, MIX_W), lambda i: (i, 0)),
        out_shape=jax.ShapeDtypeStruct((m, MIX_W), BF16),
        compiler_params=_cparams("parallel"),
        name="gmlp",
    )(z, z, ln_g.reshape(1, MIX_W), ln_b.reshape(1, MIX_W), w_s, b_s[:, :, None])


def _gmlp_first_kernel(u_ref, v_ref, lg_ref, lbias_ref, w00_ref, b0_ref, o_ref, vn_ref):
    vn = _layer_norm(jax.nn.gelu(v_ref[...]), lg_ref[...], lbias_ref[...])
    vn_ref[...] = vn
    o_ref[...] = (jax.nn.gelu(u_ref[...]) * (vn * w00_ref[...] + b0_ref[...])).astype(o_ref.dtype)


def _gmlp_first(z, col0, ln_g, ln_b, w_s, b_s):
    m = z.shape[0]
    assert col0 % MIX_W == 0
    j0 = col0 // MIX_W
    w00 = jnp.repeat(w_s[:, 0, 0], GM_GW).reshape(1, MIX_W)
    b0 = jnp.repeat(b_s[:, 0], GM_GW).reshape(1, MIX_W)
    vec = pl.BlockSpec((1, MIX_W), lambda i: (0, 0))
    return pl.pallas_call(
        _gmlp_first_kernel,
        grid=(1,),
        in_specs=[pl.BlockSpec((m, MIX_W), lambda i: (0, j0)), pl.BlockSpec((m, MIX_W), lambda i: (0, j0 + 1)),
                  vec, vec, vec, vec],
        out_specs=[pl.BlockSpec((m, MIX_W), lambda i: (0, 0)), pl.BlockSpec((m, MIX_W), lambda i: (0, 0))],
        out_shape=[jax.ShapeDtypeStruct((m, MIX_W), BF16), jax.ShapeDtypeStruct((m, MIX_W), F32)],
        name="gmlp_first",
    )(z, z, ln_g.reshape(1, MIX_W), ln_b.reshape(1, MIX_W), w00, b0)


def _layer_weights(l, w_in, fox_bf, s5_lambda_re, s5_lambda_im, s5_log_dt, s5_b_re, s5_b_im, s5_c_re, s5_c_im,
                   s5_w_glu, hgrn_lb, w_branch, w_out, w_ffn_up, w_ffn_down):
    n_a = 3 * MIX_W + LANES
    o_r = 3 * MIX_W + FOX_H
    lb_all = jnp.cumsum(jax.nn.softmax(hgrn_lb.astype(F32), axis=0), axis=0)
    return dict(
        w_a=_cast_cols(w_in, l, 0, n_a),
        w_r=_cast_cols(w_in, l, o_r, w_in.shape[2] - o_r),
        bf=jnp.pad(fox_bf[l], (0, LANES - FOX_H)).reshape(1, LANES),
        s5=_s5_params(s5_lambda_re[l], s5_lambda_im[l], s5_log_dt[l], s5_b_re[l], s5_b_im[l],
                      s5_c_re[l], s5_c_im[l]),
        w_glu=s5_w_glu[l].astype(BF16),
        lb=lb_all[l] - lb_all[0],
        w_branch=w_branch[l].astype(BF16),
        w_out=w_out[l].astype(BF16),
        w_up=w_ffn_up[l].astype(BF16),
        w_down=w_ffn_down[l].astype(BF16),
    )


def _trunk_layer(x, x_bf, l, lw, P, *, bsz, seq, fox_fn, s5_h0, hg_s0):
    m = bsz * seq
    qkv_f32, qkv_bf = _proj(x_bf, lw["w_a"], n=3 * MIX_W, out_dtypes=(F32, BF16))
    logf = _proj(x_bf, lw["w_a"], col0=3 * MIX_W, n=LANES, act="log_sigmoid_bias", bias=lw["bf"])[0][:, :FOX_H]
    o_a = fox_fn(qkv_f32, qkv_bf, logf)

    z = _proj(x_bf, lw["w_r"], n=Z_GATE)[0]
    su_tb = z[:, :MIX_W].reshape(bsz, seq, MIX_W).transpose(1, 0, 2).reshape(m, MIX_W)
    o_b_tb, s5_re, s5_im = _s5(su_tb, lw["s5"], P["s5_d"][l], lw["w_glu"], P["s5_b_glu"][l],
                               s5_h0[0].reshape(bsz, S5_STATE), s5_h0[1].reshape(bsz, S5_STATE),
                               bk=bsz, steps=seq)
    o_b = o_b_tb.reshape(seq, bsz, MIX_W).transpose(1, 0, 2).reshape(m, MIX_W)

    seq_pad = seq if seq % HG_CHUNK == 0 else -(-seq // BF16_ROWS) * BF16_ROWS
    zh, zh_col0 = z, Z_HG
    if seq_pad != seq:
        zh = z[:, Z_HG:Z_GM].reshape(bsz, seq, 4 * MIX_W)
        zh, zh_col0 = jnp.pad(zh, ((0, 0), (0, seq_pad - seq), (0, 0))).reshape(-1, 4 * MIX_W), 0
    o_c, hg_t = _hgrn(zh, zh_col0, lw["lb"], P["hgrn_norm_g"][l], hg_s0.transpose(0, 1, 3, 2),
                      bsz=bsz, seq_pad=seq_pad, valid=min(seq, HG_CHUNK))
    if seq_pad != seq:
        o_c = o_c.reshape(bsz, seq_pad, MIX_W)[:, :seq].reshape(m, MIX_W)
    hg_state = hg_t.transpose(0, 1, 3, 2)

    if seq == 1:
        o_d, v_rows = _gmlp_first(z, Z_GM, P["gmlp_ln_g"][l], P["gmlp_ln_b"][l], P["gmlp_w_s"][l], P["gmlp_b_s"][l])
    else:
        o_d = _gmlp(z, Z_GM, P["gmlp_ln_g"][l], P["gmlp_ln_b"][l], P["gmlp_w_s"][l], P["gmlp_b_s"][l])
        v_rows = None

    merged = _merge(x_bf, (o_a, o_b, o_c, o_d), lw["w_r"], Z_GATE, lw["w_branch"])
    x1, x1_bf = _matmul_res_ln(merged, lw["w_out"], x, P["ln1_g"][l], P["ln1_b"][l])
    hff = _ffn_up(x1_bf, lw["w_up"])
    x2, x2_bf = _matmul_res_ln(hff, lw["w_down"], x1, P["ln2_g"][l], P["ln2_b"][l])

    fk = qkv_f32[:, MIX_W:2 * MIX_W].reshape(bsz, seq, FOX_H, FOX_HD)
    fv = qkv_f32[:, 2 * MIX_W:].reshape(bsz, seq, FOX_H, FOX_HD)
    state = (fk, fv, logf.reshape(bsz, seq, FOX_H), s5_re.reshape(bsz, S5_G, S5_P),
             s5_im.reshape(bsz, S5_G, S5_P), hg_state, v_rows)
    return x2, x2_bf, state


def kernel(x_prompt, x_sample, cache_k, cache_v, cache_logf, page_table, state_s5_re, state_s5_im, state_hgrn,
           w_in, fox_bf, s5_lambda_re, s5_lambda_im, s5_log_dt, s5_b_re, s5_b_im, s5_c_re, s5_c_im, s5_d,
           s5_w_glu, s5_b_glu, hgrn_lb, hgrn_norm_g, gmlp_ln_g, gmlp_ln_b, gmlp_w_s, gmlp_b_s, w_branch, w_out,
           ln1_g, ln1_b, w_ffn_up, w_ffn_down, ln2_g, ln2_b):
    b_p, l_p, _ = x_prompt.shape
    b_s, l_s, _ = x_sample.shape
    assert l_s == 1, "the sample group decodes one token per sequence"
    P = dict(s5_d=s5_d, s5_b_glu=s5_b_glu, hgrn_norm_g=hgrn_norm_g, gmlp_ln_g=gmlp_ln_g, gmlp_ln_b=gmlp_ln_b,
             gmlp_w_s=gmlp_w_s, gmlp_b_s=gmlp_b_s, ln1_g=ln1_g, ln1_b=ln1_b, ln2_g=ln2_g, ln2_b=ln2_b)

    xp = x_prompt.reshape(b_p * l_p, D_MODEL)
    xs = x_sample.reshape(b_s * l_s, D_MODEL)
    xp_bf, xs_bf = xp.astype(BF16), xs.astype(BF16)
    s5_zero = jnp.zeros((b_p, S5_G, S5_P), F32)
    hg_zero = jnp.zeros((b_p, HG_H, HG_DK, HG_DV), F32)
    p_states, s_states = [], []
    for l in range(DEPTH):
        lw = _layer_weights(l, w_in, fox_bf, s5_lambda_re, s5_lambda_im, s5_log_dt, s5_b_re, s5_b_im,
                            s5_c_re, s5_c_im, s5_w_glu, hgrn_lb, w_branch, w_out, w_ffn_up, w_ffn_down)

        def fox_p(qkv_f32, qkv_bf, logf):
            c = _cumsum_lanes(logf.reshape(b_p, l_p, FOX_H).transpose(0, 2, 1).reshape(b_p * FOX_H, l_p))
            return _fox_prompt(qkv_bf, c.reshape(b_p, FOX_H, l_p), b_p, l_p)

        def fox_s(qkv_f32, qkv_bf, logf, l=l):
            o = _fox_decode(page_table, qkv_f32[:, :MIX_W], qkv_f32[:, MIX_W:2 * MIX_W], qkv_f32[:, 2 * MIX_W:],
                            logf, cache_k, cache_v, cache_logf, l)
            return o.astype(BF16)

        xp, xp_bf, sp = _trunk_layer(xp, xp_bf, l, lw, P, bsz=b_p, seq=l_p, fox_fn=fox_p,
                                     s5_h0=(s5_zero, s5_zero), hg_s0=hg_zero)
        xs, xs_bf, ss = _trunk_layer(xs, xs_bf, l, lw, P, bsz=b_s, seq=l_s, fox_fn=fox_s,
                                     s5_h0=(state_s5_re[l], state_s5_im[l]), hg_s0=state_hgrn[l])
        p_states.append(sp)
        s_states.append(ss)

    stack = lambda states, i: jnp.stack([s[i] for s in states])
    return (xp.reshape(b_p, l_p, D_MODEL), xs.reshape(b_s, l_s, D_MODEL),
            stack(p_states, 0), stack(p_states, 1), stack(p_states, 2),
            stack(s_states, 0), stack(s_states, 1), stack(s_states, 2),
            stack(p_states, 3), stack(p_states, 4), stack(s_states, 3), stack(s_states, 4),
            stack(p_states, 5), stack(s_states, 5),
            jnp.stack([s[6].reshape(b_s, l_s, MIX_W) for s in s_states]))
```

```python
import functools
import math

import numpy as np
import jax
import jax.numpy as jnp
from jax import lax
from jax.experimental import pallas as pl
from jax.experimental.pallas import tpu as pltpu

F32 = jnp.float32
BF16 = jnp.bfloat16

D_MODEL = 2048
DEPTH = 2
N_BRANCH = 4
MIX_W = D_MODEL // N_BRANCH
FOX_HD = 128
FOX_H = MIX_W // FOX_HD
S5_GROUP = 16
S5_G = MIX_W // S5_GROUP
S5_P = 64
S5_STATE = S5_G * S5_P
S5_SLAB = 512
HG_DK = 128
HG_DV = 128
HG_H = MIX_W // HG_DV
HG_CHUNK = 64
HG_SUB = 16
HG_SAFE_RANGE = 44.0
GM_CHUNK = 128
GM_GROUPS = 4
GM_GW = MIX_W // GM_GROUPS
D_FF = ((8 * D_MODEL // 3 + 255) // 256) * 256
DEEPNORM_ALPHA = (2 * DEPTH) ** 0.25
LN_EPS = 1e-5
RMS_EPS = 1e-6
LANES = 128
SUBLANES = 8
BF16_ROWS = 16
FOX_DECODE_PAGES = 32
Z_HG = MIX_W
Z_GM = Z_HG + 4 * MIX_W
Z_GATE = Z_GM + 2 * MIX_W
VMEM_LIMIT = 48 * 1024 * 1024
VMEM_LIMIT_BIG = 58 * 1024 * 1024
PROJ_TN_MAX = 1792


def _cparams(*sem):
    return pltpu.CompilerParams(dimension_semantics=sem, vmem_limit_bytes=VMEM_LIMIT)


def _log_sigmoid(x):
    return jnp.minimum(x, 0.0) - jnp.log1p(jnp.exp(-jnp.abs(x)))


def _layer_norm(x, g, b):
    mu = jnp.mean(x, axis=-1, keepdims=True)
    xc = x - mu
    var = jnp.mean(xc * xc, axis=-1, keepdims=True)
    return xc * lax.rsqrt(var + LN_EPS) * g + b


def _split3(x):
    hi = x.astype(BF16)
    r1 = x - hi.astype(F32)
    mid = r1.astype(BF16)
    lo = (r1 - mid.astype(F32)).astype(BF16)
    return hi, mid, lo


def _dot(a, b):
    return jnp.dot(a, b, preferred_element_type=F32)


def _dot_nt(a, b):
    return lax.dot_general(a, b, (((1,), (1,)), ((), ())), preferred_element_type=F32)


def _dot_tn(a, b):
    return lax.dot_general(a, b, (((0,), (0,)), ((), ())), preferred_element_type=F32)


def _cast_cols_kernel(w_ref, o_ref, *, l, depth):
    n_kt = o_ref.shape[0] // LANES
    for kt in range(n_kt):
        tile = w_ref[pl.ds(kt * depth + l, LANES, stride=n_kt * depth), :]
        o_ref[kt * LANES:(kt + 1) * LANES, :] = tile.T.astype(o_ref.dtype)


def _cast_cols(w, l, col0, n):
    depth, k, n_all = w.shape
    rows_per_col = (k // LANES) * depth
    view = w.reshape(depth, k // LANES, LANES, n_all).transpose(3, 1, 0, 2).reshape(n_all * rows_per_col, LANES)
    blk = LANES * rows_per_col
    return pl.pallas_call(
        functools.partial(_cast_cols_kernel, l=l, depth=depth),
        grid=(n // LANES,),
        in_specs=[pl.BlockSpec((pl.Element(blk), pl.Element(LANES)),
                               lambda j: (pl.multiple_of(col0 * rows_per_col + j * blk, rows_per_col), 0))],
        out_specs=pl.BlockSpec((k, LANES), lambda j: (0, j)),
        out_shape=jax.ShapeDtypeStruct((k, n), BF16),
        compiler_params=_cparams("parallel"),
        name="cast_cols",
    )(view)


def _proj_kernel(x_ref, w_ref, b_ref, *out_refs, act):
    z = _dot(x_ref[...], w_ref[...])
    if act == "gelu":
        z = jax.nn.gelu(z)
    elif act == "sigmoid":
        z = jax.nn.sigmoid(z)
    elif act == "log_sigmoid_bias":
        z = _log_sigmoid(z + b_ref[...])
    for o in out_refs:
        o[...] = z.astype(o.dtype)


def _proj(x, w, *, col0=0, n=None, act=None, bias=None, out_dtypes=(F32,)):
    m, k = x.shape
    n = w.shape[1] if n is None else n
    tm = min(m, 512)
    tn = max(t for t in range(LANES, min(n, PROJ_TN_MAX) + 1, LANES) if n % t == 0 and col0 % t == 0)
    j0 = col0 // tn
    if bias is None:
        bias = jnp.zeros((1, n), F32)
    return pl.pallas_call(
        functools.partial(_proj_kernel, act=act),
        grid=(n // tn, m // tm),
        in_specs=[pl.BlockSpec((tm, k), lambda j, i: (i, 0)),
                  pl.BlockSpec((k, tn), lambda j, i: (0, j0 + j)),
                  pl.BlockSpec((1, tn), lambda j, i: (0, j))],
        out_specs=[pl.BlockSpec((tm, tn), lambda j, i: (i, j)) for _ in out_dtypes],
        out_shape=[jax.ShapeDtypeStruct((m, n), d) for d in out_dtypes],
        compiler_params=_cparams("parallel", "arbitrary"),
        name="proj_" + (act or "id"),
    )(x, w, bias)


def _qkv_kernel(x_ref, w_ref, q_ref, kv_ref, k_ref, v_ref):
    z = _dot(x_ref[...], w_ref[...])
    tm = z.shape[0]
    q_ref[...] = z[:, :MIX_W].astype(q_ref.dtype)
    kv_ref[...] = z[:, MIX_W:].astype(kv_ref.dtype)
    for h in range(FOX_H):
        k_ref[pl.ds(h, tm, stride=FOX_H), :] = z[:, MIX_W + h * FOX_HD:MIX_W + (h + 1) * FOX_HD]
        v_ref[pl.ds(h, tm, stride=FOX_H), :] = z[:, 2 * MIX_W + h * FOX_HD:2 * MIX_W + (h + 1) * FOX_HD]


def _qkv_proj(x, w):
    m, k = x.shape
    tm = min(m, 512)
    n = 3 * MIX_W
    rows = pl.BlockSpec((tm * FOX_H, FOX_HD), lambda i: (i, 0))
    return pl.pallas_call(
        _qkv_kernel,
        grid=(m // tm,),
        in_specs=[pl.BlockSpec((tm, k), lambda i: (i, 0)), pl.BlockSpec((k, n), lambda i: (0, 0))],
        out_specs=[pl.BlockSpec((tm, MIX_W), lambda i: (i, 0)), pl.BlockSpec((tm, 2 * MIX_W), lambda i: (i, 0)),
                   rows, rows],
        out_shape=[jax.ShapeDtypeStruct((m, MIX_W), BF16), jax.ShapeDtypeStruct((m, 2 * MIX_W), BF16),
                   jax.ShapeDtypeStruct((m * FOX_H, FOX_HD), F32), jax.ShapeDtypeStruct((m * FOX_H, FOX_HD), F32)],
        compiler_params=_cparams("parallel"),
        name="qkv_proj",
    )(x, w)


def _merge_kernel(x_ref, oa_ref, ob_ref, oc_ref, od_ref, g0_ref, g1_ref, g2_ref, g3_ref, wb_ref, o_ref):
    x = x_ref[...]
    acc = None
    for br, gr, k in ((oa_ref, g0_ref, 0), (ob_ref, g1_ref, 1), (oc_ref, g2_ref, 2), (od_ref, g3_ref, 3)):
        t = jax.nn.sigmoid(_dot(x, gr[...])) * _dot(br[...], wb_ref[k])
        acc = t if acc is None else acc + t
    o_ref[...] = acc.astype(o_ref.dtype)


def _merge(x, branches, w, gate_col0, w_branch, l):
    m, k_in = x.shape
    tm = min(m, 512)
    tn = 512
    nb = D_MODEL // tn
    assert gate_col0 % tn == 0
    j0 = gate_col0 // tn
    gate_specs = [pl.BlockSpec((k_in, tn), functools.partial(lambda j, i, k: (0, j0 + k * nb + j), k=k))
                  for k in range(N_BRANCH)]
    return pl.pallas_call(
        _merge_kernel,
        grid=(nb, m // tm),
        in_specs=[pl.BlockSpec((tm, k_in), lambda j, i: (i, 0))]
                 + [pl.BlockSpec((tm, MIX_W), lambda j, i: (i, 0)) for _ in range(N_BRANCH)] + gate_specs
                 + [pl.BlockSpec((None, N_BRANCH, MIX_W, tn), lambda j, i: (l, 0, 0, j))],
        out_specs=pl.BlockSpec((tm, tn), lambda j, i: (i, j)),
        out_shape=jax.ShapeDtypeStruct((m, D_MODEL), BF16),
        compiler_params=_cparams("parallel", "arbitrary"),
        name="merge",
    )(x, *branches, w, w, w, w, w_branch)


def _res_ln_kernel(a_ref, w_ref, x_ref, g_ref, b_ref, y_ref, ybf_ref):
    y = _layer_norm(DEEPNORM_ALPHA * x_ref[...] + _dot(a_ref[...], w_ref[...]), g_ref[...], b_ref[...])
    y_ref[...] = y
    ybf_ref[...] = y.astype(BF16)


def _matmul_res_ln(a, w, l, x, g, b):
    m, k = a.shape
    n = w.shape[2]
    tm = min(m, 256)
    return pl.pallas_call(
        _res_ln_kernel,
        grid=(m // tm,),
        in_specs=[pl.BlockSpec((tm, k), lambda i: (i, 0)),
                  pl.BlockSpec((None, k, n), lambda i: (l, 0, 0), pipeline_mode=pl.Buffered(1)),
                  pl.BlockSpec((tm, n), lambda i: (i, 0)),
                  pl.BlockSpec((1, n), lambda i: (0, 0)),
                  pl.BlockSpec((1, n), lambda i: (0, 0))],
        out_specs=[pl.BlockSpec((tm, n), lambda i: (i, 0)),
                   pl.BlockSpec((tm, n), lambda i: (i, 0))],
        out_shape=[jax.ShapeDtypeStruct((m, n), F32), jax.ShapeDtypeStruct((m, n), BF16)],
        compiler_params=pltpu.CompilerParams(dimension_semantics=("parallel",), vmem_limit_bytes=VMEM_LIMIT_BIG),
        name="matmul_res_ln",
    )(a, w, x, g.reshape(1, n), b.reshape(1, n))


def _ffn_up_kernel(x_ref, wg_ref, wu_ref, o_ref):
    x = x_ref[...]
    o_ref[...] = (jax.nn.silu(_dot(x, wg_ref[...])) * _dot(x, wu_ref[...])).astype(o_ref.dtype)


def _ffn_up(x, w_up, l):
    m, k = x.shape
    tm = min(m, 1024)
    tn = 512
    nb = D_FF // tn
    return pl.pallas_call(
        _ffn_up_kernel,
        grid=(nb, m // tm),
        in_specs=[pl.BlockSpec((tm, k), lambda j, i: (i, 0)),
                  pl.BlockSpec((None, k, tn), lambda j, i: (l, 0, j)),
                  pl.BlockSpec((None, k, tn), lambda j, i: (l, 0, nb + j))],
        out_specs=pl.BlockSpec((tm, tn), lambda j, i: (i, j)),
        out_shape=jax.ShapeDtypeStruct((m, D_FF), BF16),
        compiler_params=_cparams("parallel", "arbitrary"),
        name="ffn_up",
    )(x, w_up, w_up)


def _cumsum_lanes_kernel(x_ref, o_ref):
    x = x_ref[...]
    n = x.shape[-1]
    lane = lax.broadcasted_iota(jnp.int32, x.shape, 1)
    sh = 1
    while sh < n:
        x = x + jnp.where(lane >= sh, pltpu.roll(x, sh, axis=1), 0.0)
        sh *= 2
    o_ref[...] = x


def _cumsum_lanes(x):
    return pl.pallas_call(_cumsum_lanes_kernel, out_shape=jax.ShapeDtypeStruct(x.shape, F32),
                          name="cumsum_lanes")(x)


def _flash_kernel(q_ref, k_ref, v_ref, cq_ref, ck_ref, o_ref, qt_sc, m_sc, l_sc, acc_sc, *, tq, tk, scale):
    qi = pl.program_id(2)
    ki = pl.program_id(3)

    @pl.when(ki == 0)
    def _():
        qt_sc[...] = q_ref[...].astype(F32).T.astype(BF16)
        m_sc[...] = jnp.full_like(m_sc, -jnp.inf)
        l_sc[...] = jnp.zeros_like(l_sc)
        acc_sc[...] = jnp.zeros_like(acc_sc)

    def update(masked):
        s = _dot(k_ref[...], qt_sc[...]) * scale
        s = s + cq_ref[0, 0] - jnp.concatenate([ck_ref[0, 0]] * (tq // LANES), axis=1)
        if masked:
            k_pos = lax.broadcasted_iota(jnp.int32, (tk, tq), 0)
            q_pos = lax.broadcasted_iota(jnp.int32, (tk, tq), 1)
            s = jnp.where(k_pos <= q_pos, s, -jnp.inf)
        m_prev = m_sc[...]
        m_new = jnp.maximum(m_prev, jnp.max(s, axis=0, keepdims=True))
        alpha = jnp.exp(m_prev - m_new)
        p = jnp.exp(s - m_new)
        l_sc[...] = alpha * l_sc[...] + jnp.sum(p, axis=0, keepdims=True)
        acc_sc[...] = alpha * acc_sc[...] + _dot_tn(v_ref[...], p.astype(BF16))
        m_sc[...] = m_new

    pl.when(ki < qi)(functools.partial(update, False))
    pl.when(ki == qi)(functools.partial(update, True))

    @pl.when(ki == pl.num_programs(3) - 1)
    def _():
        o_ref[...] = (acc_sc[...] / l_sc[...]).T.astype(o_ref.dtype)


def _fox_prompt(q, kv, c, bsz, seq):
    tq = tk = min(seq, 512)
    nq = seq // tq
    nk = seq // tk
    cq = c.reshape(bsz, FOX_H, 1, seq)
    ck = jnp.broadcast_to(c[..., None], (bsz, FOX_H, seq, LANES))

    def kv_blk(qi, ki):
        return jnp.minimum(ki, qi)

    return pl.pallas_call(
        functools.partial(_flash_kernel, tq=tq, tk=tk, scale=FOX_HD ** -0.5),
        grid=(bsz, FOX_H, nq, nk),
        in_specs=[pl.BlockSpec((tq, FOX_HD), lambda b, h, qi, ki: (b * nq + qi, h)),
                  pl.BlockSpec((tk, FOX_HD), lambda b, h, qi, ki: (b * nk + kv_blk(qi, ki), h)),
                  pl.BlockSpec((tk, FOX_HD), lambda b, h, qi, ki: (b * nk + kv_blk(qi, ki), FOX_H + h)),
                  pl.BlockSpec((1, 1, 1, tq), lambda b, h, qi, ki: (b, h, 0, qi)),
                  pl.BlockSpec((1, 1, tk, LANES), lambda b, h, qi, ki: (b, h, kv_blk(qi, ki), 0))],
        out_specs=pl.BlockSpec((tq, FOX_HD), lambda b, h, qi, ki: (b * nq + qi, h)),
        out_shape=jax.ShapeDtypeStruct((bsz * seq, MIX_W), BF16),
        scratch_shapes=[pltpu.VMEM((FOX_HD, tq), BF16), pltpu.VMEM((1, tq), F32), pltpu.VMEM((1, tq), F32),
                        pltpu.VMEM((FOX_HD, tq), F32)],
        compiler_params=_cparams("parallel", "parallel", "parallel", "arbitrary"),
        name="fox_prompt",
    )(q, kv, kv, cq, ck)


def _fox_decode_kernel(pt_ref, q_ref, cnew_ref, knew_ref, vnew_ref, *rest, scale, page, group):
    k_refs, v_refs, lf_refs = rest[:group], rest[group:2 * group], rest[2 * group:3 * group]
    o_ref, m_sc, l_sc, acc_sc, carry_sc = rest[3 * group:]
    p = pl.program_id(1)
    hrow = lax.broadcasted_iota(jnp.int32, (BF16_ROWS, MIX_W), 0)
    lane = lax.broadcasted_iota(jnp.int32, (BF16_ROWS, MIX_W), 1)
    hcol = lax.broadcasted_iota(jnp.int32, (BF16_ROWS, 1), 0)
    later = jnp.where(lax.broadcasted_iota(jnp.int32, (page, page), 0)
                      > lax.broadcasted_iota(jnp.int32, (page, page), 1), 1.0, 0.0).astype(BF16)

    @pl.when(p == 0)
    def _():
        m_sc[...] = jnp.full_like(m_sc, -jnp.inf)
        l_sc[...] = jnp.zeros_like(l_sc)
        acc_sc[...] = jnp.zeros_like(acc_sc)
        carry_sc[...] = jnp.zeros_like(carry_sc)

    q = q_ref[0]
    carry = carry_sc[...]
    cnew = cnew_ref[0]
    parts = []
    for j in range(group):
        lf = lf_refs[j][...]
        lf16 = jnp.zeros((BF16_ROWS, page), F32)
        for h in range(FOX_H):
            lf16 = jnp.where(hcol == h, lf[h:h + 1, :], lf16)
        hi, mid, lo = _split3(lf16)
        suffix = _dot(hi, later) + _dot(mid, later) + _dot(lo, later)
        s = None
        for h in range(FOX_H):
            kh = k_refs[j][pl.ds(h, page, stride=FOX_H), :]
            t = _dot_nt(q[:, h * FOX_HD:(h + 1) * FOX_HD], kh.astype(BF16))
            s = t if s is None else s + t
        parts.append(s * scale + (suffix + carry + cnew))
        carry = carry + jnp.sum(lf16, axis=-1, keepdims=True)
    carry_sc[...] = carry
    s = jnp.concatenate(parts, axis=1)

    m_prev = m_sc[...]
    m_new = jnp.maximum(m_prev, jnp.max(s, axis=-1, keepdims=True))
    alpha = jnp.exp(m_prev - m_new)
    pr = jnp.exp(s - m_new)
    l_sc[...] = alpha * l_sc[...] + jnp.sum(pr, axis=-1, keepdims=True)
    pr = pr.astype(BF16)
    for h in range(FOX_H):
        cols = slice(h * FOX_HD, (h + 1) * FOX_HD)
        upd = None
        for j in range(group):
            vh = v_refs[j][pl.ds(h, page, stride=FOX_H), :]
            t = _dot(pr[:, j * page:(j + 1) * page], vh.astype(BF16))
            upd = t if upd is None else upd + t
        acc_sc[:, cols] = alpha * acc_sc[:, cols] + upd
    m_sc[...] = m_new

    @pl.when(p == pl.num_programs(1) - 1)
    def _():
        kn = knew_ref[0].astype(BF16).astype(F32)
        vn = vnew_ref[0].astype(BF16).astype(F32)
        s_new = jnp.sum(q.astype(F32) * kn, axis=-1, keepdims=True) * scale
        m_prev = m_sc[...]
        m_fin = jnp.maximum(m_prev, s_new)
        alpha = jnp.exp(m_prev - m_fin)
        p_new = jnp.exp(s_new - m_fin)
        l_fin = alpha * l_sc[...] + p_new
        acc = alpha * acc_sc[...] + p_new.astype(BF16).astype(F32) * vn
        o = acc / l_fin
        o_ref[0] = jnp.sum(jnp.where((lane >> 7) == hrow, o, 0.0), axis=0, keepdims=True)


def _fox_decode(page_table, q, k_new, v_new, c_new, cache_k, cache_v, cache_lf, l):
    bsz, n_pages = page_table.shape
    depth, n_phys, page = cache_k.shape[:3]
    group = math.gcd(n_pages, FOX_DECODE_PAGES)
    cache_k = cache_k.reshape(depth, n_phys, page * FOX_H, FOX_HD)
    cache_v = cache_v.reshape(depth, n_phys, page * FOX_H, FOX_HD)
    cache_lf = cache_lf.transpose(0, 1, 3, 2)
    hmask = (np.arange(MIX_W)[None, :] // FOX_HD) == np.arange(BF16_ROWS)[:, None]
    q_rows = jnp.where(hmask[None], q[:, None, :], 0.0).astype(BF16)
    c_rows = jnp.pad(c_new, ((0, 0), (0, BF16_ROWS - FOX_H)))[:, :, None]
    pt = page_table.reshape(-1)

    def pg(j):
        return lambda b, p, pt_ref: pt_ref[b * n_pages + (n_pages - 1 - (p * group + j))]

    def kv_spec(j):
        return pl.BlockSpec((None, None, page * FOX_H, FOX_HD),
                            lambda b, p, pt_ref: (l, pg(j)(b, p, pt_ref), 0, 0))

    def lf_spec(j):
        return pl.BlockSpec((None, None, FOX_H, page), lambda b, p, pt_ref: (l, pg(j)(b, p, pt_ref), 0, 0))

    grid_spec = pltpu.PrefetchScalarGridSpec(
        num_scalar_prefetch=1,
        grid=(bsz, n_pages // group),
        in_specs=[pl.BlockSpec((1, BF16_ROWS, MIX_W), lambda b, p, pt_ref: (b, 0, 0)),
                  pl.BlockSpec((1, BF16_ROWS, 1), lambda b, p, pt_ref: (b, 0, 0)),
                  pl.BlockSpec((1, 1, MIX_W), lambda b, p, pt_ref: (b, 0, 0)),
                  pl.BlockSpec((1, 1, MIX_W), lambda b, p, pt_ref: (b, 0, 0))]
                 + [kv_spec(j) for j in range(group)] + [kv_spec(j) for j in range(group)]
                 + [lf_spec(j) for j in range(group)],
        out_specs=pl.BlockSpec((1, 1, MIX_W), lambda b, p, pt_ref: (b, 0, 0)),
        scratch_shapes=[pltpu.VMEM((BF16_ROWS, 1), F32), pltpu.VMEM((BF16_ROWS, 1), F32),
                        pltpu.VMEM((BF16_ROWS, MIX_W), F32), pltpu.VMEM((BF16_ROWS, 1), F32)],
    )
    out = pl.pallas_call(
        functools.partial(_fox_decode_kernel, scale=FOX_HD ** -0.5, page=page, group=group),
        grid_spec=grid_spec,
        out_shape=jax.ShapeDtypeStruct((bsz, 1, MIX_W), F32),
        compiler_params=_cparams("parallel", "arbitrary"),
        name="fox_decode",
    )(pt, q_rows, c_rows, k_new[:, None, :], v_new[:, None, :],
      *([cache_k] * group), *([cache_v] * group), *([cache_lf] * group))
    return out.reshape(bsz, MIX_W)


def _s5_kernel(u_ref, bblk_ref, cblk_ref, are_ref, aim_ref, d_ref, wglu_ref, bglu_ref, h0re_ref, h0im_ref,
               o_ref, hre_ref, him_ref, hs_sc, *, tc, bk, slab):
    c = pl.program_id(0)

    @pl.when(c == 0)
    def _():
        hre_ref[...] = h0re_ref[...]
        him_ref[...] = h0im_ref[...]

    u = u_ref[...]
    u_bf = u.astype(BF16)
    n_u = slab // S5_P * S5_GROUP
    y_parts = []
    for si, s0 in enumerate(range(0, S5_STATE, slab)):
        re_sl = slice(s0, s0 + slab)
        im_sl = slice(S5_STATE + s0, S5_STATE + s0 + slab)
        u_sl = slice(si * n_u, (si + 1) * n_u)
        bu = _dot(u_bf[:, u_sl], bblk_ref[u_sl, :])
        hs_sc[:, re_sl] = bu[:, :slab]
        hs_sc[:, im_sl] = bu[:, slab:]
        ar = jnp.broadcast_to(are_ref[:, re_sl], (bk, slab))
        ai = jnp.broadcast_to(aim_ref[:, re_sl], (bk, slab))

        def step(t, carry, re_sl=re_sl, im_sl=im_sl, ar=ar, ai=ai):
            hr, hi = carry
            rows = pl.ds(pl.multiple_of(t * bk, bk), bk)
            nr = ar * hr - ai * hi + hs_sc[rows, re_sl]
            ni = ar * hi + ai * hr + hs_sc[rows, im_sl]
            hs_sc[rows, re_sl] = nr
            hs_sc[rows, im_sl] = ni
            return nr, ni

        hr, hi = lax.fori_loop(0, tc, step, (hre_ref[:, re_sl], him_ref[:, re_sl]))
        hre_ref[:, re_sl] = hr
        him_ref[:, re_sl] = hi
        h_bf = jnp.concatenate([hs_sc[:, re_sl], hs_sc[:, im_sl]], axis=1).astype(BF16)
        y_parts.append(_dot(h_bf, cblk_ref[si * 2 * slab:(si + 1) * 2 * slab, :]))
    y = jnp.concatenate(y_parts, axis=1) + d_ref[...] * u
    g = jax.nn.gelu(y)
    gate = jax.nn.sigmoid(_dot(g.astype(BF16), wglu_ref[...]) + bglu_ref[...])
    o_ref[...] = (g * gate).astype(o_ref.dtype)


def _s5_params(lam_re, lam_im, log_dt, b_re, b_im, c_re, c_im):
    dt = jnp.exp(log_dt)[:, None]
    mag = jnp.exp(lam_re * dt)
    ab_re, ab_im = mag * jnp.cos(lam_im * dt), mag * jnp.sin(lam_im * dt)
    den = lam_re * lam_re + lam_im * lam_im
    nr, ni = ab_re - 1.0, ab_im
    coef_re = (nr * lam_re + ni * lam_im) / den
    coef_im = (ni * lam_re - nr * lam_im) / den
    bb_re = coef_re[..., None] * b_re - coef_im[..., None] * b_im
    bb_im = coef_re[..., None] * b_im + coef_im[..., None] * b_re
    gs = S5_SLAB // S5_P
    ns = S5_G // gs
    eye = jnp.eye(gs, dtype=F32)
    blk_b = lambda t: jnp.einsum("sgph,gk->sghkp", t.reshape(ns, gs, S5_P, S5_GROUP), eye).reshape(MIX_W, S5_SLAB)
    blk_c = lambda t: jnp.einsum("sghp,gk->sgpkh", t.reshape(ns, gs, S5_GROUP, S5_P), eye).reshape(
        ns, S5_SLAB, gs * S5_GROUP)
    bblk = jnp.concatenate([blk_b(bb_re), blk_b(bb_im)], axis=1).astype(BF16)
    cblk = jnp.concatenate([blk_c(c_re), blk_c(-c_im)], axis=1).astype(BF16)
    return (ab_re.reshape(1, S5_STATE), ab_im.reshape(1, S5_STATE), bblk,
            cblk.reshape(ns * 2 * S5_SLAB, gs * S5_GROUP))


def _s5(u_tb, params, d, w_glu, b_glu, h0_re, h0_im, *, bk, steps):
    a_re, a_im, bblk, cblk = params
    tc = min(steps, 64)
    rows = tc * bk
    const = lambda shape: pl.BlockSpec(shape, lambda c: (0,) * len(shape))
    return pl.pallas_call(
        functools.partial(_s5_kernel, tc=tc, bk=bk, slab=S5_SLAB),
        grid=(steps // tc,),
        in_specs=[pl.BlockSpec((rows, MIX_W), lambda c: (c, 0)),
                  const(bblk.shape), const(cblk.shape),
                  const((1, S5_STATE)), const((1, S5_STATE)), const((1, MIX_W)),
                  const((MIX_W, MIX_W)), const((1, MIX_W)),
                  const((bk, S5_STATE)), const((bk, S5_STATE))],
        out_specs=[pl.BlockSpec((rows, MIX_W), lambda c: (c, 0)),
                   const((bk, S5_STATE)), const((bk, S5_STATE))],
        out_shape=[jax.ShapeDtypeStruct((steps * bk, MIX_W), BF16),
                   jax.ShapeDtypeStruct((bk, S5_STATE), F32), jax.ShapeDtypeStruct((bk, S5_STATE), F32)],
        scratch_shapes=[pltpu.VMEM((rows, 2 * S5_STATE), F32)],
        compiler_params=_cparams("arbitrary"),
        name="s5",
    )(u_tb, bblk, cblk, a_re, a_im, d.reshape(1, MIX_W), w_glu, b_glu.reshape(1, MIX_W), h0_re, h0_im)


def _hgrn_kernel(q_ref, f_ref, i_ref, g_ref, lb_ref, ng_ref, tri_ref, s0_ref, o_ref, st_ref, *, c, sub, valid):
    n = pl.program_id(1)

    @pl.when(n == 0)
    def _():
        st_ref[...] = s0_ref[...]

    lb = lb_ref[...]
    la = jnp.log(lb)
    lbb = jnp.log1p(-lb) + _log_sigmoid(f_ref[...])
    lf = jnp.maximum(la, lbb) + jnp.log1p(jnp.exp(-jnp.abs(la - lbb)))
    if valid < c:
        lf = jnp.where(lax.broadcasted_iota(jnp.int32, lf.shape, 0) < valid, lf, 0.0)
    kt_all = 1.0 - jnp.exp(lf)
    bc_parts = _dot(tri_ref[...], jnp.concatenate(_split3(lf), axis=1))
    bc_all = bc_parts[:, :MIX_W] + bc_parts[:, MIX_W:2 * MIX_W] + bc_parts[:, 2 * MIX_W:]
    mid = bc_all[c // 2 - 1:c // 2]
    safe = jnp.max(jnp.maximum(-mid, mid - bc_all[c - 1:c])) < HG_SAFE_RANGE

    def run(exact):
        for h in range(HG_H):
            cols = slice(h * HG_DK, (h + 1) * HG_DK)
            o, st = _hgrn_head(q_ref[:, cols], kt_all[:, cols], bc_all[:, cols], i_ref[:, cols], st_ref[0, h],
                               c=c, sub=sub, exact=exact)
            st_ref[0, h] = st
            o = o * lax.rsqrt(jnp.mean(o * o, axis=-1, keepdims=True) + RMS_EPS) * ng_ref[:, cols]
            o_ref[:, cols] = (o * jax.nn.silu(g_ref[:, cols])).astype(o_ref.dtype)

    pl.when(safe)(functools.partial(run, False))
    pl.when(jnp.logical_not(safe))(functools.partial(run, True))


def _hgrn_head(q, kt, bc, v, st, *, c, sub, exact):
    v_bf = v.astype(BF16)
    o = _dot_nt((q * jnp.exp(bc)).astype(BF16), st.astype(BF16))
    if not exact:
        rel = bc - bc[c // 2 - 1:c // 2]
        att = _dot_nt((q * jnp.exp(rel)).astype(BF16), (kt * jnp.exp(-rel)).astype(BF16))
        t_pos = lax.broadcasted_iota(jnp.int32, (c, c), 0)
        s_pos = lax.broadcasted_iota(jnp.int32, (c, c), 1)
        o = o + _dot(jnp.where(s_pos <= t_pos, att, 0.0).astype(BF16), v_bf)
    else:
        nb = c // sub
        rows = lax.broadcasted_iota(jnp.int32, (sub, HG_DK), 0)
        blocks = []
        for i in range(nb):
            lo_r, hi_r = i * sub, (i + 1) * sub
            bci = bc[lo_r:hi_r]
            qi = q[lo_r:hi_r]
            oi = o[lo_r:hi_r]
            if i > 0:
                e = bc[lo_r - 1:lo_r]
                qs = (qi * jnp.exp(bci - e)).astype(BF16)
                ks = (kt[:lo_r] * jnp.exp(e - bc[:lo_r])).astype(BF16)
                oi = oi + _dot(_dot_nt(qs, ks).astype(BF16), v_bf[:lo_r])
            for s in range(sub):
                r = lo_r + s
                dec = jnp.exp(jnp.where(rows >= s, bci - bc[r:r + 1], -jnp.inf))
                a = jnp.sum(qi * dec * kt[r:r + 1], axis=-1, keepdims=True)
                oi = oi + a * v[r:r + 1]
            blocks.append(oi)
        o = jnp.concatenate(blocks, axis=0) if nb > 1 else blocks[0]
    bl = bc[c - 1:c]
    st = jnp.exp(bl) * st + _dot_tn(v_bf, (kt * jnp.exp(bl - bc)).astype(BF16))
    return o, st


def _hgrn(z, col0, lb, norm_g, s0_t, *, bsz, seq_pad, valid):
    c = min(HG_CHUNK, seq_pad)
    sub = min(HG_SUB, c)
    nc = seq_pad // c
    assert col0 % MIX_W == 0
    j0 = col0 // MIX_W
    tri = jnp.asarray(np.tril(np.ones((c, c))), BF16)
    col = lambda k: pl.BlockSpec((c, MIX_W), lambda b, n: (b * nc + n, j0 + k))
    vec = pl.BlockSpec((1, MIX_W), lambda b, n: (0, 0))
    state = pl.BlockSpec((1, HG_H, HG_DV, HG_DK), lambda b, n: (b, 0, 0, 0))
    return pl.pallas_call(
        functools.partial(_hgrn_kernel, c=c, sub=sub, valid=valid),
        grid=(bsz, nc),
        in_specs=[col(0), col(1), col(2), col(3), vec, vec, pl.BlockSpec((c, c), lambda b, n: (0, 0)), state],
        out_specs=[pl.BlockSpec((c, MIX_W), lambda b, n: (b * nc + n, 0)), state],
        out_shape=[jax.ShapeDtypeStruct((bsz * seq_pad, MIX_W), BF16),
                   jax.ShapeDtypeStruct((bsz, HG_H, HG_DV, HG_DK), F32)],
        compiler_params=_cparams("parallel", "arbitrary"),
        name="hgrn2",
    )(z, z, z, z, lb.reshape(1, MIX_W), norm_g.reshape(1, MIX_W), tri, s0_t)


def _gmlp_kernel(u_ref, v_ref, lg_ref, lbias_ref, ws_ref, bs_ref, o_ref):
    vn = _layer_norm(jax.nn.gelu(v_ref[...]), lg_ref[...], lbias_ref[...]).astype(BF16)
    u = jax.nn.gelu(u_ref[...])
    r = lax.broadcasted_iota(jnp.int32, (GM_CHUNK, GM_CHUNK), 0)
    cc = lax.broadcasted_iota(jnp.int32, (GM_CHUNK, GM_CHUNK), 1)
    for g in range(GM_GROUPS):
        sl = slice(g * GM_GW, (g + 1) * GM_GW)
        ws = jnp.where(r >= cc, ws_ref[g], 0.0).astype(BF16)
        mixed = _dot(ws, vn[:, sl]) + bs_ref[g]
        o_ref[:, sl] = (u[:, sl] * mixed).astype(o_ref.dtype)


def _gmlp(z, col0, ln_g, ln_b, w_s, b_s):
    m = z.shape[0]
    assert col0 % MIX_W == 0
    j0 = col0 // MIX_W
    return pl.pallas_call(
        _gmlp_kernel,
        grid=(m // GM_CHUNK,),
        in_specs=[pl.BlockSpec((GM_CHUNK, MIX_W), lambda i: (i, j0)),
                  pl.BlockSpec((GM_CHUNK, MIX_W), lambda i: (i, j0 + 1)),
                  pl.BlockSpec((1, MIX_W), lambda i: (0, 0)),
                  pl.BlockSpec((1, MIX_W), lambda i: (0, 0)),
                  pl.BlockSpec((GM_GROUPS, GM_CHUNK, GM_CHUNK), lambda i: (0, 0, 0)),
                  pl.BlockSpec((GM_GROUPS, GM_CHUNK, 1), lambda i: (0, 0, 0))],
        out_specs=pl.BlockSpec((GM_CHUNK, MIX_W), lambda i: (i, 0)),
        out_shape=jax.ShapeDtypeStruct((m, MIX_W), BF16),
        compiler_params=_cparams("parallel"),
        name="gmlp",
    )(z, z, ln_g.reshape(1, MIX_W), ln_b.reshape(1, MIX_W), w_s, b_s[:, :, None])


def _gmlp_first_kernel(u_ref, v_ref, lg_ref, lbias_ref, w00_ref, b0_ref, o_ref, vn_ref):
    vn = _layer_norm(jax.nn.gelu(v_ref[...]), lg_ref[...], lbias_ref[...])
    vn_ref[...] = vn
    o_ref[...] = (jax.nn.gelu(u_ref[...]) * (vn * w00_ref[...] + b0_ref[...])).astype(o_ref.dtype)


def _gmlp_first(z, col0, ln_g, ln_b, w_s, b_s):
    m = z.shape[0]
    assert col0 % MIX_W == 0
    j0 = col0 // MIX_W
    w00 = jnp.repeat(w_s[:, 0, 0], GM_GW).reshape(1, MIX_W)
    b0 = jnp.repeat(b_s[:, 0], GM_GW).reshape(1, MIX_W)
    vec = pl.BlockSpec((1, MIX_W), lambda i: (0, 0))
    return pl.pallas_call(
        _gmlp_first_kernel,
        grid=(1,),
        in_specs=[pl.BlockSpec((m, MIX_W), lambda i: (0, j0)), pl.BlockSpec((m, MIX_W), lambda i: (0, j0 + 1)),
                  vec, vec, vec, vec],
        out_specs=[pl.BlockSpec((m, MIX_W), lambda i: (0, 0)), pl.BlockSpec((m, MIX_W), lambda i: (0, 0))],
        out_shape=[jax.ShapeDtypeStruct((m, MIX_W), BF16), jax.ShapeDtypeStruct((m, MIX_W), F32)],
        name="gmlp_first",
    )(z, z, ln_g.reshape(1, MIX_W), ln_b.reshape(1, MIX_W), w00, b0)


def _layer_weights(l, w_in, fox_bf, s5_lambda_re, s5_lambda_im, s5_log_dt, s5_b_re, s5_b_im, s5_c_re, s5_c_im,
                   s5_w_glu, hgrn_lb, w_branch, w_out, w_ffn_up, w_ffn_down):
    n_a = 3 * MIX_W + LANES
    o_r = 3 * MIX_W + FOX_H
    lb_all = jnp.cumsum(jax.nn.softmax(hgrn_lb.astype(F32), axis=0), axis=0)
    return dict(
        w_a=_cast_cols(w_in, l, 0, n_a),
        w_r=_cast_cols(w_in, l, o_r, w_in.shape[2] - o_r),
        bf=jnp.pad(fox_bf[l], (0, LANES - FOX_H)).reshape(1, LANES),
        s5=_s5_params(s5_lambda_re[l], s5_lambda_im[l], s5_log_dt[l], s5_b_re[l], s5_b_im[l],
                      s5_c_re[l], s5_c_im[l]),
        w_glu=s5_w_glu[l].astype(BF16),
        lb=lb_all[l] - lb_all[0],
        w_branch=w_branch, w_out=w_out, w_up=w_ffn_up, w_down=w_ffn_down,
    )


def _trunk_layer(x, x_bf, l, lw, P, *, bsz, seq, fox_fn, s5_h0, hg_s0):
    m = bsz * seq
    q_bf, kv_bf, k_rows, v_rows_fox = _qkv_proj(x_bf, lw["w_a"])
    logf = _proj(x_bf, lw["w_a"], col0=3 * MIX_W, n=LANES, act="log_sigmoid_bias", bias=lw["bf"])[0][:, :FOX_H]
    o_a = fox_fn(q_bf, kv_bf, k_rows, v_rows_fox, logf)

    z = _proj(x_bf, lw["w_r"], n=Z_GATE)[0]
    su_tb = z[:, :MIX_W].reshape(bsz, seq, MIX_W).transpose(1, 0, 2).reshape(m, MIX_W)
    o_b_tb, s5_re, s5_im = _s5(su_tb, lw["s5"], P["s5_d"][l], lw["w_glu"], P["s5_b_glu"][l],
                               s5_h0[0].reshape(bsz, S5_STATE), s5_h0[1].reshape(bsz, S5_STATE),
                               bk=bsz, steps=seq)
    o_b = o_b_tb.reshape(seq, bsz, MIX_W).transpose(1, 0, 2).reshape(m, MIX_W)

    seq_pad = seq if seq % HG_CHUNK == 0 else -(-seq // BF16_ROWS) * BF16_ROWS
    zh, zh_col0 = z, Z_HG
    if seq_pad != seq:
        zh = z[:, Z_HG:Z_GM].reshape(bsz, seq, 4 * MIX_W)
        zh, zh_col0 = jnp.pad(zh, ((0, 0), (0, seq_pad - seq), (0, 0))).reshape(-1, 4 * MIX_W), 0
    o_c, hg_t = _hgrn(zh, zh_col0, lw["lb"], P["hgrn_norm_g"][l], hg_s0.transpose(0, 1, 3, 2),
                      bsz=bsz, seq_pad=seq_pad, valid=min(seq, HG_CHUNK))
    if seq_pad != seq:
        o_c = o_c.reshape(bsz, seq_pad, MIX_W)[:, :seq].reshape(m, MIX_W)
    hg_state = hg_t.transpose(0, 1, 3, 2)

    if seq == 1:
        o_d, v_rows = _gmlp_first(z, Z_GM, P["gmlp_ln_g"][l], P["gmlp_ln_b"][l], P["gmlp_w_s"][l], P["gmlp_b_s"][l])
    else:
        o_d = _gmlp(z, Z_GM, P["gmlp_ln_g"][l], P["gmlp_ln_b"][l], P["gmlp_w_s"][l], P["gmlp_b_s"][l])
        v_rows = None

    merged = _merge(x_bf, (o_a, o_b, o_c, o_d), lw["w_r"], Z_GATE, lw["w_branch"], l)
    x1, x1_bf = _matmul_res_ln(merged, lw["w_out"], l, x, P["ln1_g"][l], P["ln1_b"][l])
    hff = _ffn_up(x1_bf, lw["w_up"], l)
    x2, x2_bf = _matmul_res_ln(hff, lw["w_down"], l, x1, P["ln2_g"][l], P["ln2_b"][l])

    fk = k_rows.reshape(bsz, seq, FOX_H, FOX_HD)
    fv = v_rows_fox.reshape(bsz, seq, FOX_H, FOX_HD)
    state = (fk, fv, logf.reshape(bsz, seq, FOX_H), s5_re.reshape(bsz, S5_G, S5_P),
             s5_im.reshape(bsz, S5_G, S5_P), hg_state, v_rows)
    return x2, x2_bf, state


def kernel(x_prompt, x_sample, cache_k, cache_v, cache_logf, page_table, state_s5_re, state_s5_im, state_hgrn,
           w_in, fox_bf, s5_lambda_re, s5_lambda_im, s5_log_dt, s5_b_re, s5_b_im, s5_c_re, s5_c_im, s5_d,
           s5_w_glu, s5_b_glu, hgrn_lb, hgrn_norm_g, gmlp_ln_g, gmlp_ln_b, gmlp_w_s, gmlp_b_s, w_branch, w_out,
           ln1_g, ln1_b, w_ffn_up, w_ffn_down, ln2_g, ln2_b):
    b_p, l_p, _ = x_prompt.shape
    b_s, l_s, _ = x_sample.shape
    assert l_s == 1, "the sample group decodes one token per sequence"
    P = dict(s5_d=s5_d, s5_b_glu=s5_b_glu, hgrn_norm_g=hgrn_norm_g, gmlp_ln_g=gmlp_ln_g, gmlp_ln_b=gmlp_ln_b,
             gmlp_w_s=gmlp_w_s, gmlp_b_s=gmlp_b_s, ln1_g=ln1_g, ln1_b=ln1_b, ln2_g=ln2_g, ln2_b=ln2_b)

    xp = x_prompt.reshape(b_p * l_p, D_MODEL)
    xs = x_sample.reshape(b_s * l_s, D_MODEL)
    xp_bf, xs_bf = xp.astype(BF16), xs.astype(BF16)
    s5_zero = jnp.zeros((b_p, S5_G, S5_P), F32)
    hg_zero = jnp.zeros((b_p, HG_H, HG_DK, HG_DV), F32)
    p_states, s_states = [], []
    w_branch, w_out, w_ffn_up, w_ffn_down = (w.astype(BF16) for w in (w_branch, w_out, w_ffn_up, w_ffn_down))
    for l in range(DEPTH):
        lw = _layer_weights(l, w_in, fox_bf, s5_lambda_re, s5_lambda_im, s5_log_dt, s5_b_re, s5_b_im,
                            s5_c_re, s5_c_im, s5_w_glu, hgrn_lb, w_branch, w_out, w_ffn_up, w_ffn_down)

        def fox_p(q_bf, kv_bf, k_rows, v_rows, logf):
            c = _cumsum_lanes(logf.reshape(b_p, l_p, FOX_H).transpose(0, 2, 1).reshape(b_p * FOX_H, l_p))
            return _fox_prompt(q_bf, kv_bf, c.reshape(b_p, FOX_H, l_p), b_p, l_p)

        def fox_s(q_bf, kv_bf, k_rows, v_rows, logf, l=l):
            o = _fox_decode(page_table, q_bf, k_rows.reshape(b_s, MIX_W), v_rows.reshape(b_s, MIX_W),
                            logf, cache_k, cache_v, cache_logf, l)
            return o.astype(BF16)

        xp, xp_bf, sp = _trunk_layer(xp, xp_bf, l, lw, P, bsz=b_p, seq=l_p, fox_fn=fox_p,
                                     s5_h0=(s5_zero, s5_zero), hg_s0=hg_zero)
        xs, xs_bf, ss = _trunk_layer(xs, xs_bf, l, lw, P, bsz=b_s, seq=l_s, fox_fn=fox_s,
                                     s5_h0=(state_s5_re[l], state_s5_im[l]), hg_s0=state_hgrn[l])
        p_states.append(sp)
        s_states.append(ss)

    stack = lambda states, i: jnp.stack([s[i] for s in states])
    return (xp.reshape(b_p, l_p, D_MODEL), xs.reshape(b_s, l_s, D_MODEL),
            stack(p_states, 0), stack(p_states, 1), stack(p_states, 2),
            stack(s_states, 0), stack(s_states, 1), stack(s_states, 2),
            stack(p_states, 3), stack(p_states, 4), stack(s_states, 3), stack(s_states, 4),
            stack(p_states, 5), stack(s_states, 5),
            jnp.stack([s[6].reshape(b_s, l_s, MIX_W) for s in s_states]))
```

```python
import functools
import math

import numpy as np
import jax
import jax.numpy as jnp
from jax import lax
from jax.experimental import pallas as pl
from jax.experimental.pallas import tpu as pltpu

F32 = jnp.float32
BF16 = jnp.bfloat16

D_MODEL = 2048
DEPTH = 2
N_BRANCH = 4
MIX_W = D_MODEL // N_BRANCH
FOX_HD = 128
FOX_H = MIX_W // FOX_HD
S5_GROUP = 16
S5_G = MIX_W // S5_GROUP
S5_P = 64
S5_STATE = S5_G * S5_P
S5_SLAB = 512
HG_DK = 128
HG_DV = 128
HG_H = MIX_W // HG_DV
HG_CHUNK = 64
HG_SUB = 16
HG_SEQS = 2
HG_SAFE_RANGE = 44.0
GM_CHUNK = 128
GM_GROUPS = 4
GM_GW = MIX_W // GM_GROUPS
D_FF = ((8 * D_MODEL // 3 + 255) // 256) * 256
DEEPNORM_ALPHA = (2 * DEPTH) ** 0.25
LN_EPS = 1e-5
RMS_EPS = 1e-6
LANES = 128
SUBLANES = 8
BF16_ROWS = 16
FOX_DECODE_PAGES = 16
FOX_DECODE_SEQS = 2
Z_HG = MIX_W
Z_GM = Z_HG + 4 * MIX_W
Z_GATE = Z_GM + 2 * MIX_W
VMEM_LIMIT = 48 * 1024 * 1024
VMEM_LIMIT_BIG = 58 * 1024 * 1024
PROJ_TN_MAX = 1792


def _cparams(*sem):
    return pltpu.CompilerParams(dimension_semantics=sem, vmem_limit_bytes=VMEM_LIMIT)


def _log_sigmoid(x):
    return jnp.minimum(x, 0.0) - jnp.log1p(jnp.exp(-jnp.abs(x)))


def _layer_norm(x, g, b):
    mu = jnp.mean(x, axis=-1, keepdims=True)
    xc = x - mu
    var = jnp.mean(xc * xc, axis=-1, keepdims=True)
    return xc * lax.rsqrt(var + LN_EPS) * g + b


def _split3(x):
    hi = x.astype(BF16)
    r1 = x - hi.astype(F32)
    mid = r1.astype(BF16)
    lo = (r1 - mid.astype(F32)).astype(BF16)
    return hi, mid, lo


def _dot(a, b):
    return jnp.dot(a, b, preferred_element_type=F32)


def _dot_nt(a, b):
    return lax.dot_general(a, b, (((1,), (1,)), ((), ())), preferred_element_type=F32)


def _dot_tn(a, b):
    return lax.dot_general(a, b, (((0,), (0,)), ((), ())), preferred_element_type=F32)


def _cast_cols_kernel(w_ref, o_ref, *, l, depth):
    n_kt = o_ref.shape[0] // LANES
    for kt in range(n_kt):
        tile = w_ref[pl.ds(kt * depth + l, LANES, stride=n_kt * depth), :]
        o_ref[kt * LANES:(kt + 1) * LANES, :] = tile.T.astype(o_ref.dtype)


def _cast_cols(w, l, col0, n):
    depth, k, n_all = w.shape
    rows_per_col = (k // LANES) * depth
    view = w.reshape(depth, k // LANES, LANES, n_all).transpose(3, 1, 0, 2).reshape(n_all * rows_per_col, LANES)
    blk = LANES * rows_per_col
    return pl.pallas_call(
        functools.partial(_cast_cols_kernel, l=l, depth=depth),
        grid=(n // LANES,),
        in_specs=[pl.BlockSpec((pl.Element(blk), pl.Element(LANES)),
                               lambda j: (pl.multiple_of(col0 * rows_per_col + j * blk, rows_per_col), 0))],
        out_specs=pl.BlockSpec((k, LANES), lambda j: (0, j)),
        out_shape=jax.ShapeDtypeStruct((k, n), BF16),
        compiler_params=_cparams("parallel"),
        name="cast_cols",
    )(view)


def _proj_kernel(x_ref, w_ref, b_ref, *out_refs, act):
    z = _dot(x_ref[...], w_ref[...])
    if act == "gelu":
        z = jax.nn.gelu(z)
    elif act == "sigmoid":
        z = jax.nn.sigmoid(z)
    elif act == "log_sigmoid_bias":
        z = _log_sigmoid(z + b_ref[...])
    for o in out_refs:
        o[...] = z.astype(o.dtype)


def _proj(x, w, *, col0=0, n=None, act=None, bias=None, out_dtypes=(F32,)):
    m, k = x.shape
    n = w.shape[1] if n is None else n
    tm = min(m, 512)
    tn = max(t for t in range(LANES, min(n, PROJ_TN_MAX) + 1, LANES) if n % t == 0 and col0 % t == 0)
    j0 = col0 // tn
    if bias is None:
        bias = jnp.zeros((1, n), F32)
    return pl.pallas_call(
        functools.partial(_proj_kernel, act=act),
        grid=(n // tn, m // tm),
        in_specs=[pl.BlockSpec((tm, k), lambda j, i: (i, 0)),
                  pl.BlockSpec((k, tn), lambda j, i: (0, j0 + j)),
                  pl.BlockSpec((1, tn), lambda j, i: (0, j))],
        out_specs=[pl.BlockSpec((tm, tn), lambda j, i: (i, j)) for _ in out_dtypes],
        out_shape=[jax.ShapeDtypeStruct((m, n), d) for d in out_dtypes],
        compiler_params=_cparams("parallel", "arbitrary"),
        name="proj_" + (act or "id"),
    )(x, w, bias)


def _qkv_kernel(x_ref, w_ref, b_ref, xbf_ref, q_ref, kv_ref, k_ref, v_ref, lf_ref):
    x = x_ref[...].astype(BF16)
    xbf_ref[...] = x
    z = _dot(x, w_ref[...])
    tm = z.shape[0]
    q_ref[...] = z[:, :MIX_W].astype(q_ref.dtype)
    kv_ref[...] = z[:, MIX_W:3 * MIX_W].astype(kv_ref.dtype)
    for h in range(FOX_H):
        k_ref[pl.ds(h, tm, stride=FOX_H), :] = z[:, MIX_W + h * FOX_HD:MIX_W + (h + 1) * FOX_HD]
        v_ref[pl.ds(h, tm, stride=FOX_H), :] = z[:, 2 * MIX_W + h * FOX_HD:2 * MIX_W + (h + 1) * FOX_HD]
    lf_ref[...] = _log_sigmoid(z[:, 3 * MIX_W:] + b_ref[...])


def _qkv_proj(x, w, bias):
    m, k = x.shape
    tm = min(m, 512)
    n = w.shape[1]
    row = lambda width: pl.BlockSpec((tm, width), lambda i: (i, 0))
    rows = pl.BlockSpec((tm * FOX_H, FOX_HD), lambda i: (i, 0))
    return pl.pallas_call(
        _qkv_kernel,
        grid=(m // tm,),
        in_specs=[row(k), pl.BlockSpec((k, n), lambda i: (0, 0), pipeline_mode=pl.Buffered(1)),
                  pl.BlockSpec((1, LANES), lambda i: (0, 0))],
        out_specs=[row(k), row(MIX_W), row(2 * MIX_W), rows, rows, row(LANES)],
        out_shape=[jax.ShapeDtypeStruct((m, k), BF16), jax.ShapeDtypeStruct((m, MIX_W), BF16),
                   jax.ShapeDtypeStruct((m, 2 * MIX_W), BF16),
                   jax.ShapeDtypeStruct((m * FOX_H, FOX_HD), F32), jax.ShapeDtypeStruct((m * FOX_H, FOX_HD), F32),
                   jax.ShapeDtypeStruct((m, LANES), F32)],
        compiler_params=_cparams("parallel"),
        name="qkv_proj",
    )(x, w, bias)


def _merge_kernel(x_ref, oa_ref, ob_ref, oc_ref, od_ref, g0_ref, g1_ref, g2_ref, g3_ref, wb_ref, o_ref):
    x = x_ref[...]
    acc = None
    for br, gr, k in ((oa_ref, g0_ref, 0), (ob_ref, g1_ref, 1), (oc_ref, g2_ref, 2), (od_ref, g3_ref, 3)):
        t = jax.nn.sigmoid(_dot(x, gr[...])) * _dot(br[...], wb_ref[k])
        acc = t if acc is None else acc + t
    o_ref[...] = acc.astype(o_ref.dtype)


def _merge(x, branches, w, gate_col0, w_branch, l):
    m, k_in = x.shape
    tm = min(m, 512)
    tn = 512
    nb = D_MODEL // tn
    assert gate_col0 % tn == 0
    j0 = gate_col0 // tn
    gate_specs = [pl.BlockSpec((k_in, tn), functools.partial(lambda j, i, k: (0, j0 + k * nb + j), k=k))
                  for k in range(N_BRANCH)]
    return pl.pallas_call(
        _merge_kernel,
        grid=(nb, m // tm),
        in_specs=[pl.BlockSpec((tm, k_in), lambda j, i: (i, 0))]
                 + [pl.BlockSpec((tm, MIX_W), lambda j, i: (i, 0)) for _ in range(N_BRANCH)] + gate_specs
                 + [pl.BlockSpec((None, N_BRANCH, MIX_W, tn), lambda j, i: (l, 0, 0, j))],
        out_specs=pl.BlockSpec((tm, tn), lambda j, i: (i, j)),
        out_shape=jax.ShapeDtypeStruct((m, D_MODEL), BF16),
        compiler_params=_cparams("parallel", "arbitrary"),
        name="merge",
    )(x, *branches, w, w, w, w, w_branch)


def _res_ln_kernel(a_ref, w_ref, x_ref, g_ref, b_ref, *y_refs):
    y = _layer_norm(DEEPNORM_ALPHA * x_ref[...] + _dot(a_ref[...], w_ref[...]), g_ref[...], b_ref[...])
    for y_ref in y_refs:
        y_ref[...] = y.astype(y_ref.dtype)


def _matmul_res_ln(a, w, l, x, g, b, *, out_dtypes):
    m, k = a.shape
    n = w.shape[2]
    tm = min(m, 256)
    return pl.pallas_call(
        _res_ln_kernel,
        grid=(m // tm,),
        in_specs=[pl.BlockSpec((tm, k), lambda i: (i, 0)),
                  pl.BlockSpec((None, k, n), lambda i: (l, 0, 0), pipeline_mode=pl.Buffered(1)),
                  pl.BlockSpec((tm, n), lambda i: (i, 0)),
                  pl.BlockSpec((1, n), lambda i: (0, 0)),
                  pl.BlockSpec((1, n), lambda i: (0, 0))],
        out_specs=[pl.BlockSpec((tm, n), lambda i: (i, 0)) for _ in out_dtypes],
        out_shape=[jax.ShapeDtypeStruct((m, n), d) for d in out_dtypes],
        compiler_params=pltpu.CompilerParams(dimension_semantics=("parallel",), vmem_limit_bytes=VMEM_LIMIT_BIG),
        name="matmul_res_ln",
    )(a, w, x, g.reshape(1, n), b.reshape(1, n))


def _ffn_up_kernel(x_ref, wg_ref, wu_ref, o_ref):
    x = x_ref[...]
    o_ref[...] = (jax.nn.silu(_dot(x, wg_ref[...])) * _dot(x, wu_ref[...])).astype(o_ref.dtype)


def _ffn_up(x, w_up, l):
    m, k = x.shape
    tm = min(m, 1024)
    tn = 512
    nb = D_FF // tn
    return pl.pallas_call(
        _ffn_up_kernel,
        grid=(nb, m // tm),
        in_specs=[pl.BlockSpec((tm, k), lambda j, i: (i, 0)),
                  pl.BlockSpec((None, k, tn), lambda j, i: (l, 0, j)),
                  pl.BlockSpec((None, k, tn), lambda j, i: (l, 0, nb + j))],
        out_specs=pl.BlockSpec((tm, tn), lambda j, i: (i, j)),
        out_shape=jax.ShapeDtypeStruct((m, D_FF), BF16),
        compiler_params=_cparams("parallel", "arbitrary"),
        name="ffn_up",
    )(x, w_up, w_up)


def _cumsum_lanes_kernel(x_ref, o_ref):
    x = x_ref[...]
    n = x.shape[-1]
    lane = lax.broadcasted_iota(jnp.int32, x.shape, 1)
    sh = 1
    while sh < n:
        x = x + jnp.where(lane >= sh, pltpu.roll(x, sh, axis=1), 0.0)
        sh *= 2
    o_ref[...] = x


def _cumsum_lanes(x):
    return pl.pallas_call(_cumsum_lanes_kernel, out_shape=jax.ShapeDtypeStruct(x.shape, F32),
                          name="cumsum_lanes")(x)


def _flash_kernel(q_ref, k_ref, v_ref, cq_ref, ck_ref, o_ref, qt_sc, m_sc, l_sc, acc_sc, *, tq, tk, scale):
    qi = pl.program_id(2)
    ki = pl.program_id(3)

    @pl.when(ki == 0)
    def _():
        qt_sc[...] = q_ref[...].astype(F32).T.astype(BF16)
        m_sc[...] = jnp.full_like(m_sc, -jnp.inf)
        l_sc[...] = jnp.zeros_like(l_sc)
        acc_sc[...] = jnp.zeros_like(acc_sc)

    def update(masked):
        s = _dot(k_ref[...], qt_sc[...]) * scale
        s = s + cq_ref[0, 0] - jnp.concatenate([ck_ref[0, 0]] * (tq // LANES), axis=1)
        if masked:
            k_pos = lax.broadcasted_iota(jnp.int32, (tk, tq), 0)
            q_pos = lax.broadcasted_iota(jnp.int32, (tk, tq), 1)
            s = jnp.where(k_pos <= q_pos, s, -jnp.inf)
        m_prev = m_sc[...]
        m_new = jnp.maximum(m_prev, jnp.max(s, axis=0, keepdims=True))
        alpha = jnp.exp(m_prev - m_new)
        p = jnp.exp(s - m_new)
        l_sc[...] = alpha * l_sc[...] + jnp.sum(p, axis=0, keepdims=True)
        acc_sc[...] = alpha * acc_sc[...] + _dot_tn(v_ref[...], p.astype(BF16))
        m_sc[...] = m_new

    pl.when(ki < qi)(functools.partial(update, False))
    pl.when(ki == qi)(functools.partial(update, True))

    @pl.when(ki == pl.num_programs(3) - 1)
    def _():
        o_ref[...] = (acc_sc[...] / l_sc[...]).T.astype(o_ref.dtype)


def _fox_prompt(q, kv, c, bsz, seq):
    tq = tk = min(seq, 512)
    nq = seq // tq
    nk = seq // tk
    cq = c.reshape(bsz, FOX_H, 1, seq)
    ck = jnp.broadcast_to(c[..., None], (bsz, FOX_H, seq, LANES))

    def kv_blk(qi, ki):
        return jnp.minimum(ki, qi)

    return pl.pallas_call(
        functools.partial(_flash_kernel, tq=tq, tk=tk, scale=FOX_HD ** -0.5),
        grid=(bsz, FOX_H, nq, nk),
        in_specs=[pl.BlockSpec((tq, FOX_HD), lambda b, h, qi, ki: (b * nq + qi, h)),
                  pl.BlockSpec((tk, FOX_HD), lambda b, h, qi, ki: (b * nk + kv_blk(qi, ki), h)),
                  pl.BlockSpec((tk, FOX_HD), lambda b, h, qi, ki: (b * nk + kv_blk(qi, ki), FOX_H + h)),
                  pl.BlockSpec((1, 1, 1, tq), lambda b, h, qi, ki: (b, h, 0, qi)),
                  pl.BlockSpec((1, 1, tk, LANES), lambda b, h, qi, ki: (b, h, kv_blk(qi, ki), 0))],
        out_specs=pl.BlockSpec((tq, FOX_HD), lambda b, h, qi, ki: (b * nq + qi, h)),
        out_shape=jax.ShapeDtypeStruct((bsz * seq, MIX_W), BF16),
        scratch_shapes=[pltpu.VMEM((FOX_HD, tq), BF16), pltpu.VMEM((1, tq), F32), pltpu.VMEM((1, tq), F32),
                        pltpu.VMEM((FOX_HD, tq), F32)],
        compiler_params=_cparams("parallel", "parallel", "parallel", "arbitrary"),
        name="fox_prompt",
    )(q, kv, kv, cq, ck)


def _fox_decode_kernel(pt_ref, q_ref, cnew_ref, knew_ref, vnew_ref, *rest, scale, page, group, nseq):
    n = nseq * group
    o_ref, m_sc, l_sc, acc_sc, carry_sc = rest[3 * n:]
    p = pl.program_id(1)

    @pl.when(p == 0)
    def _():
        m_sc[...] = jnp.full_like(m_sc, -jnp.inf)
        l_sc[...] = jnp.zeros_like(l_sc)
        acc_sc[...] = jnp.zeros_like(acc_sc)
        carry_sc[...] = jnp.zeros_like(carry_sc)

    for r in range(nseq):
        sl = slice(r * group, (r + 1) * group)
        _fox_decode_pages(q_ref.at[r], cnew_ref.at[r], rest[:n][sl], rest[n:2 * n][sl], rest[2 * n:3 * n][sl],
                          m_sc.at[r], l_sc.at[r], acc_sc.at[r], carry_sc.at[r], scale=scale, page=page, group=group)

    @pl.when(p == pl.num_programs(1) - 1)
    def _():
        for r in range(nseq):
            _fox_decode_finish(q_ref.at[r], knew_ref.at[r], vnew_ref.at[r], o_ref.at[r], m_sc.at[r], l_sc.at[r],
                               acc_sc.at[r], scale=scale)


def _fox_decode_pages(q_ref, cnew_ref, k_refs, v_refs, lf_refs, m_sc, l_sc, acc_sc, carry_sc, *, scale, page, group):
    hcol = lax.broadcasted_iota(jnp.int32, (BF16_ROWS, 1), 0)
    later = jnp.where(lax.broadcasted_iota(jnp.int32, (page, page), 0)
                      > lax.broadcasted_iota(jnp.int32, (page, page), 1), 1.0, 0.0).astype(BF16)

    q = q_ref[...]
    carry = carry_sc[...]
    cnew = cnew_ref[...]
    parts = []
    for j in range(group):
        lf = lf_refs[j][...]
        lf16 = jnp.zeros((BF16_ROWS, page), F32)
        for h in range(FOX_H):
            lf16 = jnp.where(hcol == h, lf[h:h + 1, :], lf16)
        hi, mid, lo = _split3(lf16)
        suffix = _dot(hi, later) + _dot(mid, later) + _dot(lo, later)
        s = None
        for h in range(FOX_H):
            kh = k_refs[j][pl.ds(h, page, stride=FOX_H), :]
            t = _dot_nt(q[:, h * FOX_HD:(h + 1) * FOX_HD], kh.astype(BF16))
            s = t if s is None else s + t
        parts.append(s * scale + (suffix + carry + cnew))
        carry = carry + jnp.sum(lf16, axis=-1, keepdims=True)
    carry_sc[...] = carry
    s = jnp.concatenate(parts, axis=1)

    m_prev = m_sc[...]
    m_new = jnp.maximum(m_prev, jnp.max(s, axis=-1, keepdims=True))
    alpha = jnp.exp(m_prev - m_new)
    pr = jnp.exp(s - m_new)
    l_sc[...] = alpha * l_sc[...] + jnp.sum(pr, axis=-1, keepdims=True)
    pr = pr.astype(BF16)
    for h in range(FOX_H):
        cols = slice(h * FOX_HD, (h + 1) * FOX_HD)
        upd = None
        for j in range(group):
            vh = v_refs[j][pl.ds(h, page, stride=FOX_H), :]
            t = _dot(pr[:, j * page:(j + 1) * page], vh.astype(BF16))
            upd = t if upd is None else upd + t
        acc_sc[:, cols] = alpha * acc_sc[:, cols] + upd
    m_sc[...] = m_new


def _fox_decode_finish(q_ref, knew_ref, vnew_ref, o_ref, m_sc, l_sc, acc_sc, *, scale):
    hrow = lax.broadcasted_iota(jnp.int32, (BF16_ROWS, MIX_W), 0)
    lane = lax.broadcasted_iota(jnp.int32, (BF16_ROWS, MIX_W), 1)
    kn = knew_ref[...].astype(BF16).astype(F32)
    vn = vnew_ref[...].astype(BF16).astype(F32)
    s_new = jnp.sum(q_ref[...].astype(F32) * kn, axis=-1, keepdims=True) * scale
    m_prev = m_sc[...]
    m_fin = jnp.maximum(m_prev, s_new)
    alpha = jnp.exp(m_prev - m_fin)
    p_new = jnp.exp(s_new - m_fin)
    l_fin = alpha * l_sc[...] + p_new
    acc = alpha * acc_sc[...] + p_new.astype(BF16).astype(F32) * vn
    o = acc / l_fin
    o_ref[...] = jnp.sum(jnp.where((lane >> 7) == hrow, o, 0.0), axis=0, keepdims=True)


def _fox_decode(page_table, q, k_new, v_new, c_new, cache_k, cache_v, cache_lf, l):
    bsz, n_pages = page_table.shape
    depth, n_phys, page = cache_k.shape[:3]
    group = math.gcd(n_pages, FOX_DECODE_PAGES)
    cache_k = cache_k.reshape(depth, n_phys, page * FOX_H, FOX_HD)
    cache_v = cache_v.reshape(depth, n_phys, page * FOX_H, FOX_HD)
    cache_lf = cache_lf.transpose(0, 1, 3, 2)
    hmask = (np.arange(MIX_W)[None, :] // FOX_HD) == np.arange(BF16_ROWS)[:, None]
    q_rows = jnp.where(hmask[None], q[:, None, :], 0.0).astype(BF16)
    c_rows = jnp.pad(c_new, ((0, 0), (0, BF16_ROWS - FOX_H)))[:, :, None]
    pt = page_table.reshape(-1)

    nseq = math.gcd(bsz, FOX_DECODE_SEQS)
    slots = [(r, j) for r in range(nseq) for j in range(group)]

    def pg(r, j):
        return lambda b, p, pt_ref: pt_ref[(b * nseq + r) * n_pages + (n_pages - 1 - (p * group + j))]

    def kv_spec(r, j):
        return pl.BlockSpec((None, None, page * FOX_H, FOX_HD),
                            lambda b, p, pt_ref: (l, pg(r, j)(b, p, pt_ref), 0, 0))

    def lf_spec(r, j):
        return pl.BlockSpec((None, None, FOX_H, page), lambda b, p, pt_ref: (l, pg(r, j)(b, p, pt_ref), 0, 0))

    per_seq = lambda *shape: pl.BlockSpec((nseq,) + shape, lambda b, p, pt_ref: (b,) + (0,) * len(shape))
    grid_spec = pltpu.PrefetchScalarGridSpec(
        num_scalar_prefetch=1,
        grid=(bsz // nseq, n_pages // group),
        in_specs=[per_seq(BF16_ROWS, MIX_W), per_seq(BF16_ROWS, 1), per_seq(1, MIX_W), per_seq(1, MIX_W)]
                 + [kv_spec(r, j) for r, j in slots] + [kv_spec(r, j) for r, j in slots]
                 + [lf_spec(r, j) for r, j in slots],
        out_specs=per_seq(1, MIX_W),
        scratch_shapes=[pltpu.VMEM((nseq, BF16_ROWS, 1), F32), pltpu.VMEM((nseq, BF16_ROWS, 1), F32),
                        pltpu.VMEM((nseq, BF16_ROWS, MIX_W), F32), pltpu.VMEM((nseq, BF16_ROWS, 1), F32)],
    )
    out = pl.pallas_call(
        functools.partial(_fox_decode_kernel, scale=FOX_HD ** -0.5, page=page, group=group, nseq=nseq),
        grid_spec=grid_spec,
        out_shape=jax.ShapeDtypeStruct((bsz, 1, MIX_W), F32),
        compiler_params=_cparams("parallel", "arbitrary"),
        name="fox_decode",
    )(pt, q_rows, c_rows, k_new[:, None, :], v_new[:, None, :],
      *([cache_k] * len(slots)), *([cache_v] * len(slots)), *([cache_lf] * len(slots)))
    return out.reshape(bsz, MIX_W)


def _s5_kernel(u_ref, bblk_ref, cblk_ref, are_ref, aim_ref, d_ref, wglu_ref, bglu_ref, h0re_ref, h0im_ref,
               o_ref, hre_ref, him_ref, hs_sc, *, tc, bk, slab):
    c = pl.program_id(0)

    @pl.when(c == 0)
    def _():
        hre_ref[...] = h0re_ref[...]
        him_ref[...] = h0im_ref[...]

    u = u_ref[...]
    u_bf = u.astype(BF16)
    n_u = slab // S5_P * S5_GROUP
    y_parts = []
    for si, s0 in enumerate(range(0, S5_STATE, slab)):
        re_sl = slice(s0, s0 + slab)
        im_sl = slice(S5_STATE + s0, S5_STATE + s0 + slab)
        u_sl = slice(si * n_u, (si + 1) * n_u)
        bu = _dot(u_bf[:, u_sl], bblk_ref[u_sl, :])
        hs_sc[:, re_sl] = bu[:, :slab]
        hs_sc[:, im_sl] = bu[:, slab:]
        ar = jnp.broadcast_to(are_ref[:, re_sl], (bk, slab))
        ai = jnp.broadcast_to(aim_ref[:, re_sl], (bk, slab))

        def step(t, carry, re_sl=re_sl, im_sl=im_sl, ar=ar, ai=ai):
            hr, hi = carry
            rows = pl.ds(pl.multiple_of(t * bk, bk), bk)
            nr = ar * hr - ai * hi + hs_sc[rows, re_sl]
            ni = ar * hi + ai * hr + hs_sc[rows, im_sl]
            hs_sc[rows, re_sl] = nr
            hs_sc[rows, im_sl] = ni
            return nr, ni

        hr, hi = lax.fori_loop(0, tc, step, (hre_ref[:, re_sl], him_ref[:, re_sl]))
        hre_ref[:, re_sl] = hr
        him_ref[:, re_sl] = hi
        h_bf = jnp.concatenate([hs_sc[:, re_sl], hs_sc[:, im_sl]], axis=1).astype(BF16)
        y_parts.append(_dot(h_bf, cblk_ref[si * 2 * slab:(si + 1) * 2 * slab, :]))
    y = jnp.concatenate(y_parts, axis=1) + d_ref[...] * u
    g = jax.nn.gelu(y)
    gate = jax.nn.sigmoid(_dot(g.astype(BF16), wglu_ref[...]) + bglu_ref[...])
    o_ref[...] = (g * gate).astype(o_ref.dtype)


def _s5_params(lam_re, lam_im, log_dt, b_re, b_im, c_re, c_im):
    dt = jnp.exp(log_dt)[:, None]
    mag = jnp.exp(lam_re * dt)
    ab_re, ab_im = mag * jnp.cos(lam_im * dt), mag * jnp.sin(lam_im * dt)
    den = lam_re * lam_re + lam_im * lam_im
    nr, ni = ab_re - 1.0, ab_im
    coef_re = (nr * lam_re + ni * lam_im) / den
    coef_im = (ni * lam_re - nr * lam_im) / den
    bb_re = coef_re[..., None] * b_re - coef_im[..., None] * b_im
    bb_im = coef_re[..., None] * b_im + coef_im[..., None] * b_re
    gs = S5_SLAB // S5_P
    ns = S5_G // gs
    eye = jnp.eye(gs, dtype=F32)
    blk_b = lambda t: jnp.einsum("sgph,gk->sghkp", t.reshape(ns, gs, S5_P, S5_GROUP), eye).reshape(MIX_W, S5_SLAB)
    blk_c = lambda t: jnp.einsum("sghp,gk->sgpkh", t.reshape(ns, gs, S5_GROUP, S5_P), eye).reshape(
        ns, S5_SLAB, gs * S5_GROUP)
    bblk = jnp.concatenate([blk_b(bb_re), blk_b(bb_im)], axis=1).astype(BF16)
    cblk = jnp.concatenate([blk_c(c_re), blk_c(-c_im)], axis=1).astype(BF16)
    return (ab_re.reshape(1, S5_STATE), ab_im.reshape(1, S5_STATE), bblk,
            cblk.reshape(ns * 2 * S5_SLAB, gs * S5_GROUP))


def _s5(u_tb, params, d, w_glu, b_glu, h0_re, h0_im, *, bk, steps):
    a_re, a_im, bblk, cblk = params
    tc = min(steps, 64)
    rows = tc * bk
    const = lambda shape: pl.BlockSpec(shape, lambda c: (0,) * len(shape))
    return pl.pallas_call(
        functools.partial(_s5_kernel, tc=tc, bk=bk, slab=S5_SLAB),
        grid=(steps // tc,),
        in_specs=[pl.BlockSpec((rows, MIX_W), lambda c: (c, 0)),
                  const(bblk.shape), const(cblk.shape),
                  const((1, S5_STATE)), const((1, S5_STATE)), const((1, MIX_W)),
                  const((MIX_W, MIX_W)), const((1, MIX_W)),
                  const((bk, S5_STATE)), const((bk, S5_STATE))],
        out_specs=[pl.BlockSpec((rows, MIX_W), lambda c: (c, 0)),
                   const((bk, S5_STATE)), const((bk, S5_STATE))],
        out_shape=[jax.ShapeDtypeStruct((steps * bk, MIX_W), BF16),
                   jax.ShapeDtypeStruct((bk, S5_STATE), F32), jax.ShapeDtypeStruct((bk, S5_STATE), F32)],
        scratch_shapes=[pltpu.VMEM((rows, 2 * S5_STATE), F32)],
        compiler_params=_cparams("arbitrary"),
        name="s5",
    )(u_tb, bblk, cblk, a_re, a_im, d.reshape(1, MIX_W), w_glu, b_glu.reshape(1, MIX_W), h0_re, h0_im)


def _hgrn_kernel(q_ref, f_ref, i_ref, g_ref, lb_ref, ng_ref, tri_ref, s0_ref, o_ref, st_ref, *, c, sub, valid):
    n = pl.program_id(1)

    @pl.when(n == 0)
    def _():
        st_ref[...] = s0_ref[...]

    n_seq = q_ref.shape[0]
    lb = lb_ref[...]
    la = jnp.log(lb)
    lb1 = jnp.log1p(-lb)
    decay = []
    worst = jnp.zeros((1, MIX_W), F32)
    for r in range(n_seq):
        lbb = lb1 + _log_sigmoid(f_ref[r])
        lf = jnp.maximum(la, lbb) + jnp.log1p(jnp.exp(-jnp.abs(la - lbb)))
        if valid < c:
            lf = jnp.where(lax.broadcasted_iota(jnp.int32, lf.shape, 0) < valid, lf, 0.0)
        parts = _dot(tri_ref[...], jnp.concatenate(_split3(lf), axis=1))
        bc = parts[:, :MIX_W] + parts[:, MIX_W:2 * MIX_W] + parts[:, 2 * MIX_W:]
        decay.append((1.0 - jnp.exp(lf), bc))
        mid = bc[c // 2 - 1:c // 2]
        worst = jnp.maximum(worst, jnp.maximum(-mid, mid - bc[c - 1:c]))
    safe = jnp.max(worst) < HG_SAFE_RANGE

    def run(exact):
        for r in range(n_seq):
            kt_all, bc_all = decay[r]
            for h in range(HG_H):
                cols = slice(h * HG_DK, (h + 1) * HG_DK)
                o, st = _hgrn_head(q_ref[r, :, cols], kt_all[:, cols], bc_all[:, cols], i_ref[r, :, cols],
                                   st_ref[r, h], c=c, sub=sub, exact=exact)
                st_ref[r, h] = st
                o = o * lax.rsqrt(jnp.mean(o * o, axis=-1, keepdims=True) + RMS_EPS) * ng_ref[:, cols]
                o_ref[r, :, cols] = (o * jax.nn.silu(g_ref[r, :, cols])).astype(o_ref.dtype)

    pl.when(safe)(functools.partial(run, False))
    pl.when(jnp.logical_not(safe))(functools.partial(run, True))


def _hgrn_head(q, kt, bc, v, st, *, c, sub, exact):
    v_bf = v.astype(BF16)
    o = _dot_nt((q * jnp.exp(bc)).astype(BF16), st.astype(BF16))
    if not exact:
        rel = bc - bc[c // 2 - 1:c // 2]
        att = _dot_nt((q * jnp.exp(rel)).astype(BF16), (kt * jnp.exp(-rel)).astype(BF16))
        t_pos = lax.broadcasted_iota(jnp.int32, (c, c), 0)
        s_pos = lax.broadcasted_iota(jnp.int32, (c, c), 1)
        o = o + _dot(jnp.where(s_pos <= t_pos, att, 0.0).astype(BF16), v_bf)
    else:
        nb = c // sub
        rows = lax.broadcasted_iota(jnp.int32, (sub, HG_DK), 0)
        blocks = []
        for i in range(nb):
            lo_r, hi_r = i * sub, (i + 1) * sub
            bci = bc[lo_r:hi_r]
            qi = q[lo_r:hi_r]
            oi = o[lo_r:hi_r]
            if i > 0:
                e = bc[lo_r - 1:lo_r]
                qs = (qi * jnp.exp(bci - e)).astype(BF16)
                ks = (kt[:lo_r] * jnp.exp(e - bc[:lo_r])).astype(BF16)
                oi = oi + _dot(_dot_nt(qs, ks).astype(BF16), v_bf[:lo_r])
            for s in range(sub):
                r = lo_r + s
                dec = jnp.exp(jnp.where(rows >= s, bci - bc[r:r + 1], -jnp.inf))
                a = jnp.sum(qi * dec * kt[r:r + 1], axis=-1, keepdims=True)
                oi = oi + a * v[r:r + 1]
            blocks.append(oi)
        o = jnp.concatenate(blocks, axis=0) if nb > 1 else blocks[0]
    bl = bc[c - 1:c]
    st = jnp.exp(bl) * st + _dot_tn(v_bf, (kt * jnp.exp(bl - bc)).astype(BF16))
    return o, st


def _hgrn(z, col0, lb, norm_g, s0_t, *, bsz, seq_pad, valid):
    c = min(HG_CHUNK, seq_pad)
    sub = min(HG_SUB, c)
    nc = seq_pad // c
    assert col0 % MIX_W == 0 and bsz % HG_SEQS == 0
    j0 = col0 // MIX_W
    z = z.reshape(bsz, seq_pad, z.shape[-1])
    tri = jnp.asarray(np.tril(np.ones((c, c))), BF16)
    col = lambda k: pl.BlockSpec((HG_SEQS, c, MIX_W), lambda b, n: (b, n, j0 + k))
    vec = pl.BlockSpec((1, MIX_W), lambda b, n: (0, 0))
    state = pl.BlockSpec((HG_SEQS, HG_H, HG_DV, HG_DK), lambda b, n: (b, 0, 0, 0))
    o, st = pl.pallas_call(
        functools.partial(_hgrn_kernel, c=c, sub=sub, valid=valid),
        grid=(bsz // HG_SEQS, nc),
        in_specs=[col(0), col(1), col(2), col(3), vec, vec, pl.BlockSpec((c, c), lambda b, n: (0, 0)), state],
        out_specs=[pl.BlockSpec((HG_SEQS, c, MIX_W), lambda b, n: (b, n, 0)), state],
        out_shape=[jax.ShapeDtypeStruct((bsz, seq_pad, MIX_W), BF16),
                   jax.ShapeDtypeStruct((bsz, HG_H, HG_DV, HG_DK), F32)],
        compiler_params=_cparams("parallel", "arbitrary"),
        name="hgrn2",
    )(z, z, z, z, lb.reshape(1, MIX_W), norm_g.reshape(1, MIX_W), tri, s0_t)
    return o.reshape(bsz * seq_pad, MIX_W), st


def _gmlp_kernel(u_ref, v_ref, lg_ref, lbias_ref, ws_ref, bs_ref, o_ref):
    vn = _layer_norm(jax.nn.gelu(v_ref[...]), lg_ref[...], lbias_ref[...]).astype(BF16)
    u = jax.nn.gelu(u_ref[...])
    r = lax.broadcasted_iota(jnp.int32, (GM_CHUNK, GM_CHUNK), 0)
    cc = lax.broadcasted_iota(jnp.int32, (GM_CHUNK, GM_CHUNK), 1)
    for g in range(GM_GROUPS):
        sl = slice(g * GM_GW, (g + 1) * GM_GW)
        ws = jnp.where(r >= cc, ws_ref[g], 0.0).astype(BF16)
        mixed = _dot(ws, vn[:, sl]) + bs_ref[g]
        o_ref[:, sl] = (u[:, sl] * mixed).astype(o_ref.dtype)


def _gmlp(z, col0, ln_g, ln_b, w_s, b_s):
    m = z.shape[0]
    assert col0 % MIX_W == 0
    j0 = col0 // MIX_W
    return pl.pallas_call(
        _gmlp_kernel,
        grid=(m // GM_CHUNK,),
        in_specs=[pl.BlockSpec((GM_CHUNK, MIX_W), lambda i: (i, j0)),
                  pl.BlockSpec((GM_CHUNK, MIX_W), lambda i: (i, j0 + 1)),
                  pl.BlockSpec((1, MIX_W), lambda i: (0, 0)),
                  pl.BlockSpec((1, MIX_W), lambda i: (0, 0)),
                  pl.BlockSpec((GM_GROUPS, GM_CHUNK, GM_CHUNK), lambda i: (0, 0, 0)),
                  pl.BlockSpec((GM_GROUPS, GM_CHUNK, 1), lambda i: (0, 0, 0))],
        out_specs=pl.BlockSpec((GM_CHUNK, MIX_W), lambda i: (i, 0)),
        out_shape=jax.ShapeDtypeStruct((m, MIX_W), BF16),
        compiler_params=_cparams("parallel"),
        name="gmlp",
    )(z, z, ln_g.reshape(1, MIX_W), ln_b.reshape(1, MIX_W), w_s, b_s[:, :, None])


def _gmlp_first_kernel(u_ref, v_ref, lg_ref, lbias_ref, w00_ref, b0_ref, o_ref, vn_ref):
    vn = _layer_norm(jax.nn.gelu(v_ref[...]), lg_ref[...], lbias_ref[...])
    vn_ref[...] = vn
    o_ref[...] = (jax.nn.gelu(u_ref[...]) * (vn * w00_ref[...] + b0_ref[...])).astype(o_ref.dtype)


def _gmlp_first(z, col0, ln_g, ln_b, w_s, b_s):
    m = z.shape[0]
    assert col0 % MIX_W == 0
    j0 = col0 // MIX_W
    w00 = jnp.repeat(w_s[:, 0, 0], GM_GW).reshape(1, MIX_W)
    b0 = jnp.repeat(b_s[:, 0], GM_GW).reshape(1, MIX_W)
    vec = pl.BlockSpec((1, MIX_W), lambda i: (0, 0))
    return pl.pallas_call(
        _gmlp_first_kernel,
        grid=(1,),
        in_specs=[pl.BlockSpec((m, MIX_W), lambda i: (0, j0)), pl.BlockSpec((m, MIX_W), lambda i: (0, j0 + 1)),
                  vec, vec, vec, vec],
        out_specs=[pl.BlockSpec((m, MIX_W), lambda i: (0, 0)), pl.BlockSpec((m, MIX_W), lambda i: (0, 0))],
        out_shape=[jax.ShapeDtypeStruct((m, MIX_W), BF16), jax.ShapeDtypeStruct((m, MIX_W), F32)],
        name="gmlp_first",
    )(z, z, ln_g.reshape(1, MIX_W), ln_b.reshape(1, MIX_W), w00, b0)


def _layer_weights(l, w_in, fox_bf, s5_lambda_re, s5_lambda_im, s5_log_dt, s5_b_re, s5_b_im, s5_c_re, s5_c_im,
                   s5_w_glu, hgrn_lb, w_branch, w_out, w_ffn_up, w_ffn_down):
    n_a = 3 * MIX_W + LANES
    o_r = 3 * MIX_W + FOX_H
    lb_all = jnp.cumsum(jax.nn.softmax(hgrn_lb.astype(F32), axis=0), axis=0)
    return dict(
        w_a=_cast_cols(w_in, l, 0, n_a),
        w_r=_cast_cols(w_in, l, o_r, w_in.shape[2] - o_r),
        bf=jnp.pad(fox_bf[l], (0, LANES - FOX_H)).reshape(1, LANES),
        s5=_s5_params(s5_lambda_re[l], s5_lambda_im[l], s5_log_dt[l], s5_b_re[l], s5_b_im[l],
                      s5_c_re[l], s5_c_im[l]),
        w_glu=s5_w_glu[l].astype(BF16),
        lb=lb_all[l] - lb_all[0],
        w_branch=w_branch, w_out=w_out, w_up=w_ffn_up, w_down=w_ffn_down,
    )


def _trunk_layer(x, l, lw, P, *, bsz, seq, fox_fn, s5_h0, hg_s0):
    m = bsz * seq
    x_bf, q_bf, kv_bf, k_rows, v_rows_fox, logf = _qkv_proj(x, lw["w_a"], lw["bf"])
    logf = logf[:, :FOX_H]
    o_a = fox_fn(q_bf, kv_bf, k_rows, v_rows_fox, logf)

    z = _proj(x_bf, lw["w_r"], n=Z_GATE)[0]
    su_tb = z[:, :MIX_W].reshape(bsz, seq, MIX_W).transpose(1, 0, 2).reshape(m, MIX_W)
    o_b_tb, s5_re, s5_im = _s5(su_tb, lw["s5"], P["s5_d"][l], lw["w_glu"], P["s5_b_glu"][l],
                               s5_h0[0].reshape(bsz, S5_STATE), s5_h0[1].reshape(bsz, S5_STATE),
                               bk=bsz, steps=seq)
    o_b = o_b_tb.reshape(seq, bsz, MIX_W).transpose(1, 0, 2).reshape(m, MIX_W)

    seq_pad = seq if seq % HG_CHUNK == 0 else -(-seq // BF16_ROWS) * BF16_ROWS
    zh, zh_col0 = z, Z_HG
    if seq_pad != seq:
        zh = z[:, Z_HG:Z_GM].reshape(bsz, seq, 4 * MIX_W)
        zh, zh_col0 = jnp.pad(zh, ((0, 0), (0, seq_pad - seq), (0, 0))).reshape(-1, 4 * MIX_W), 0
    o_c, hg_t = _hgrn(zh, zh_col0, lw["lb"], P["hgrn_norm_g"][l], hg_s0.transpose(0, 1, 3, 2),
                      bsz=bsz, seq_pad=seq_pad, valid=min(seq, HG_CHUNK))
    if seq_pad != seq:
        o_c = o_c.reshape(bsz, seq_pad, MIX_W)[:, :seq].reshape(m, MIX_W)
    hg_state = hg_t.transpose(0, 1, 3, 2)

    if seq == 1:
        o_d, v_rows = _gmlp_first(z, Z_GM, P["gmlp_ln_g"][l], P["gmlp_ln_b"][l], P["gmlp_w_s"][l], P["gmlp_b_s"][l])
    else:
        o_d = _gmlp(z, Z_GM, P["gmlp_ln_g"][l], P["gmlp_ln_b"][l], P["gmlp_w_s"][l], P["gmlp_b_s"][l])
        v_rows = None

    merged = _merge(x_bf, (o_a, o_b, o_c, o_d), lw["w_r"], Z_GATE, lw["w_branch"], l)
    x1, x1_bf = _matmul_res_ln(merged, lw["w_out"], l, x, P["ln1_g"][l], P["ln1_b"][l], out_dtypes=(F32, BF16))
    hff = _ffn_up(x1_bf, lw["w_up"], l)
    x2, = _matmul_res_ln(hff, lw["w_down"], l, x1, P["ln2_g"][l], P["ln2_b"][l], out_dtypes=(F32,))

    fk = k_rows.reshape(bsz, seq, FOX_H, FOX_HD)
    fv = v_rows_fox.reshape(bsz, seq, FOX_H, FOX_HD)
    state = (fk, fv, logf.reshape(bsz, seq, FOX_H), s5_re.reshape(bsz, S5_G, S5_P),
             s5_im.reshape(bsz, S5_G, S5_P), hg_state, v_rows)
    return x2, state


def kernel(x_prompt, x_sample, cache_k, cache_v, cache_logf, page_table, state_s5_re, state_s5_im, state_hgrn,
           w_in, fox_bf, s5_lambda_re, s5_lambda_im, s5_log_dt, s5_b_re, s5_b_im, s5_c_re, s5_c_im, s5_d,
           s5_w_glu, s5_b_glu, hgrn_lb, hgrn_norm_g, gmlp_ln_g, gmlp_ln_b, gmlp_w_s, gmlp_b_s, w_branch, w_out,
           ln1_g, ln1_b, w_ffn_up, w_ffn_down, ln2_g, ln2_b):
    b_p, l_p, _ = x_prompt.shape
    b_s, l_s, _ = x_sample.shape
    assert l_s == 1, "the sample group decodes one token per sequence"
    P = dict(s5_d=s5_d, s5_b_glu=s5_b_glu, hgrn_norm_g=hgrn_norm_g, gmlp_ln_g=gmlp_ln_g, gmlp_ln_b=gmlp_ln_b,
             gmlp_w_s=gmlp_w_s, gmlp_b_s=gmlp_b_s, ln1_g=ln1_g, ln1_b=ln1_b, ln2_g=ln2_g, ln2_b=ln2_b)

    xp = x_prompt.reshape(b_p * l_p, D_MODEL)
    xs = x_sample.reshape(b_s * l_s, D_MODEL)
    s5_zero = jnp.zeros((b_p, S5_G, S5_P), F32)
    hg_zero = jnp.zeros((b_p, HG_H, HG_DK, HG_DV), F32)
    p_states, s_states = [], []
    w_branch, w_out, w_ffn_up, w_ffn_down = (w.astype(BF16) for w in (w_branch, w_out, w_ffn_up, w_ffn_down))
    for l in range(DEPTH):
        lw = _layer_weights(l, w_in, fox_bf, s5_lambda_re, s5_lambda_im, s5_log_dt, s5_b_re, s5_b_im,
                            s5_c_re, s5_c_im, s5_w_glu, hgrn_lb, w_branch, w_out, w_ffn_up, w_ffn_down)

        def fox_p(q_bf, kv_bf, k_rows, v_rows, logf):
            c = _cumsum_lanes(logf.reshape(b_p, l_p, FOX_H).transpose(0, 2, 1).reshape(b_p * FOX_H, l_p))
            return _fox_prompt(q_bf, kv_bf, c.reshape(b_p, FOX_H, l_p), b_p, l_p)

        def fox_s(q_bf, kv_bf, k_rows, v_rows, logf, l=l):
            o = _fox_decode(page_table, q_bf, k_rows.reshape(b_s, MIX_W), v_rows.reshape(b_s, MIX_W),
                            logf, cache_k, cache_v, cache_logf, l)
            return o.astype(BF16)

        xp, sp = _trunk_layer(xp, l, lw, P, bsz=b_p, seq=l_p, fox_fn=fox_p,
                              s5_h0=(s5_zero, s5_zero), hg_s0=hg_zero)
        xs, ss = _trunk_layer(xs, l, lw, P, bsz=b_s, seq=l_s, fox_fn=fox_s,
                              s5_h0=(state_s5_re[l], state_s5_im[l]), hg_s0=state_hgrn[l])
        p_states.append(sp)
        s_states.append(ss)

    stack = lambda states, i: jnp.stack([s[i] for s in states])
    return (xp.reshape(b_p, l_p, D_MODEL), xs.reshape(b_s, l_s, D_MODEL),
            stack(p_states, 0), stack(p_states, 1), stack(p_states, 2),
            stack(s_states, 0), stack(s_states, 1), stack(s_states, 2),
            stack(p_states, 3), stack(p_states, 4), stack(s_states, 3), stack(s_states, 4),
            stack(p_states, 5), stack(s_states, 5),
            jnp.stack([s[6].reshape(b_s, l_s, MIX_W) for s in s_states]))
```

```python
import functools
import math

import numpy as np
import jax
import jax.numpy as jnp
from jax import lax
from jax.experimental import pallas as pl
from jax.experimental.pallas import tpu as pltpu

F32 = jnp.float32
BF16 = jnp.bfloat16

D_MODEL = 2048
DEPTH = 2
N_BRANCH = 4
MIX_W = D_MODEL // N_BRANCH
FOX_HD = 128
FOX_H = MIX_W // FOX_HD
S5_GROUP = 16
S5_G = MIX_W // S5_GROUP
S5_P = 64
S5_STATE = S5_G * S5_P
S5_SLAB = 512
HG_DK = 128
HG_DV = 128
HG_H = MIX_W // HG_DV
HG_CHUNK = 64
HG_SUB = 16
HG_SEQS = 4
HG_SAFE_RANGE = 44.0
GM_CHUNK = 128
GM_GROUPS = 4
GM_GW = MIX_W // GM_GROUPS
GM_STEP_CHUNKS = 4
D_FF = ((8 * D_MODEL // 3 + 255) // 256) * 256
DEEPNORM_ALPHA = (2 * DEPTH) ** 0.25
LN_EPS = 1e-5
RMS_EPS = 1e-6
LANES = 128
SUBLANES = 8
BF16_ROWS = 16
FOX_DECODE_PAGES = 32
FOX_DECODE_SEQS = 1
Z_HG = MIX_W
Z_GM = Z_HG + 4 * MIX_W
Z_GATE = Z_GM + 2 * MIX_W
VMEM_LIMIT = 48 * 1024 * 1024
VMEM_LIMIT_BIG = 58 * 1024 * 1024
PROJ_TN_MAX = 1792


def _cparams(*sem):
    return pltpu.CompilerParams(dimension_semantics=sem, vmem_limit_bytes=VMEM_LIMIT)


def _log_sigmoid(x):
    return jnp.minimum(x, 0.0) - jnp.log1p(jnp.exp(-jnp.abs(x)))


def _layer_norm(x, g, b):
    mu = jnp.mean(x, axis=-1, keepdims=True)
    xc = x - mu
    var = jnp.mean(xc * xc, axis=-1, keepdims=True)
    return xc * lax.rsqrt(var + LN_EPS) * g + b


def _split3(x):
    hi = x.astype(BF16)
    r1 = x - hi.astype(F32)
    mid = r1.astype(BF16)
    lo = (r1 - mid.astype(F32)).astype(BF16)
    return hi, mid, lo


def _dot(a, b):
    return jnp.dot(a, b, preferred_element_type=F32)


def _dot_nt(a, b):
    return lax.dot_general(a, b, (((1,), (1,)), ((), ())), preferred_element_type=F32)


def _dot_tn(a, b):
    return lax.dot_general(a, b, (((0,), (0,)), ((), ())), preferred_element_type=F32)


def _cast_cols_kernel(w_ref, o_ref, *, l, depth):
    n_kt = o_ref.shape[0] // LANES
    rows_per_col = n_kt * depth
    for cb in range(o_ref.shape[1] // LANES):
        for kt in range(n_kt):
            first = cb * LANES * rows_per_col + kt * depth + l
            tile = w_ref[pl.ds(first, LANES, stride=rows_per_col), :]
            o_ref[kt * LANES:(kt + 1) * LANES, cb * LANES:(cb + 1) * LANES] = tile.T.astype(o_ref.dtype)


def _cast_cols(w, l, col0, n):
    depth, k, n_all = w.shape
    rows_per_col = (k // LANES) * depth
    view = w.reshape(depth, k // LANES, LANES, n_all).transpose(3, 1, 0, 2).reshape(n_all * rows_per_col, LANES)
    tn = 2 * LANES if n % (2 * LANES) == 0 else LANES
    blk = tn * rows_per_col
    return pl.pallas_call(
        functools.partial(_cast_cols_kernel, l=l, depth=depth),
        grid=(n // tn,),
        in_specs=[pl.BlockSpec((pl.Element(blk), pl.Element(LANES)),
                               lambda j: (pl.multiple_of(col0 * rows_per_col + j * blk, rows_per_col), 0))],
        out_specs=pl.BlockSpec((k, tn), lambda j: (0, j)),
        out_shape=jax.ShapeDtypeStruct((k, n), BF16),
        compiler_params=_cparams("parallel"),
        name="cast_cols",
    )(view)


def _proj_kernel(x_ref, w_ref, b_ref, *out_refs, act):
    z = _dot(x_ref[...], w_ref[...])
    if act == "gelu":
        z = jax.nn.gelu(z)
    elif act == "sigmoid":
        z = jax.nn.sigmoid(z)
    elif act == "log_sigmoid_bias":
        z = _log_sigmoid(z + b_ref[...])
    for o in out_refs:
        o[...] = z.astype(o.dtype)


def _proj(x, w, *, col0=0, n=None, act=None, bias=None, out_dtypes=(F32,)):
    m, k = x.shape
    n = w.shape[1] if n is None else n
    tm = min(m, 512)
    tn = max(t for t in range(LANES, min(n, PROJ_TN_MAX) + 1, LANES) if n % t == 0 and col0 % t == 0)
    j0 = col0 // tn
    if bias is None:
        bias = jnp.zeros((1, n), F32)
    return pl.pallas_call(
        functools.partial(_proj_kernel, act=act),
        grid=(n // tn, m // tm),
        in_specs=[pl.BlockSpec((tm, k), lambda j, i: (i, 0)),
                  pl.BlockSpec((k, tn), lambda j, i: (0, j0 + j)),
                  pl.BlockSpec((1, tn), lambda j, i: (0, j))],
        out_specs=[pl.BlockSpec((tm, tn), lambda j, i: (i, j)) for _ in out_dtypes],
        out_shape=[jax.ShapeDtypeStruct((m, n), d) for d in out_dtypes],
        compiler_params=_cparams("parallel", "arbitrary"),
        name="proj_" + (act or "id"),
    )(x, w, bias)


def _qkv_kernel(x_ref, w_ref, b_ref, xbf_ref, q_ref, kv_ref, k_ref, v_ref, lf_ref):
    x = x_ref[...].astype(BF16)
    xbf_ref[...] = x
    z = _dot(x, w_ref[...])
    tm = z.shape[0]
    q_ref[...] = z[:, :MIX_W].astype(q_ref.dtype)
    kv_ref[...] = z[:, MIX_W:3 * MIX_W].astype(kv_ref.dtype)
    for h in range(FOX_H):
        k_ref[pl.ds(h, tm, stride=FOX_H), :] = z[:, MIX_W + h * FOX_HD:MIX_W + (h + 1) * FOX_HD]
        v_ref[pl.ds(h, tm, stride=FOX_H), :] = z[:, 2 * MIX_W + h * FOX_HD:2 * MIX_W + (h + 1) * FOX_HD]
    lf_ref[...] = _log_sigmoid(z[:, 3 * MIX_W:] + b_ref[...])


def _qkv_proj(x, w, bias):
    m, k = x.shape
    tm = min(m, 512)
    n = w.shape[1]
    row = lambda width: pl.BlockSpec((tm, width), lambda i: (i, 0))
    rows = pl.BlockSpec((tm * FOX_H, FOX_HD), lambda i: (i, 0))
    return pl.pallas_call(
        _qkv_kernel,
        grid=(m // tm,),
        in_specs=[row(k), pl.BlockSpec((k, n), lambda i: (0, 0), pipeline_mode=pl.Buffered(1)),
                  pl.BlockSpec((1, LANES), lambda i: (0, 0))],
        out_specs=[row(k), row(MIX_W), row(2 * MIX_W), rows, rows, row(LANES)],
        out_shape=[jax.ShapeDtypeStruct((m, k), BF16), jax.ShapeDtypeStruct((m, MIX_W), BF16),
                   jax.ShapeDtypeStruct((m, 2 * MIX_W), BF16),
                   jax.ShapeDtypeStruct((m * FOX_H, FOX_HD), F32), jax.ShapeDtypeStruct((m * FOX_H, FOX_HD), F32),
                   jax.ShapeDtypeStruct((m, LANES), F32)],
        compiler_params=_cparams("parallel"),
        name="qkv_proj",
    )(x, w, bias)


def _merge_kernel(x_ref, oa_ref, ob_ref, oc_ref, od_ref, g0_ref, g1_ref, g2_ref, g3_ref, wb_ref, o_ref):
    x = x_ref[...]
    acc = None
    for br, gr, k in ((oa_ref, g0_ref, 0), (ob_ref, g1_ref, 1), (oc_ref, g2_ref, 2), (od_ref, g3_ref, 3)):
        t = jax.nn.sigmoid(_dot(x, gr[...])) * _dot(br[...], wb_ref[k])
        acc = t if acc is None else acc + t
    o_ref[...] = acc.astype(o_ref.dtype)


def _merge(x, branches, w, gate_col0, w_branch, l):
    m, k_in = x.shape
    tm = min(m, 512)
    tn = 512
    nb = D_MODEL // tn
    assert gate_col0 % tn == 0
    j0 = gate_col0 // tn
    gate_specs = [pl.BlockSpec((k_in, tn), functools.partial(lambda j, i, k: (0, j0 + k * nb + j), k=k))
                  for k in range(N_BRANCH)]
    return pl.pallas_call(
        _merge_kernel,
        grid=(nb, m // tm),
        in_specs=[pl.BlockSpec((tm, k_in), lambda j, i: (i, 0))]
                 + [pl.BlockSpec((tm, MIX_W), lambda j, i: (i, 0)) for _ in range(N_BRANCH)] + gate_specs
                 + [pl.BlockSpec((None, N_BRANCH, MIX_W, tn), lambda j, i: (l, 0, 0, j))],
        out_specs=pl.BlockSpec((tm, tn), lambda j, i: (i, j)),
        out_shape=jax.ShapeDtypeStruct((m, D_MODEL), BF16),
        compiler_params=_cparams("parallel", "arbitrary"),
        name="merge",
    )(x, *branches, w, w, w, w, w_branch)


def _res_ln_kernel(a_ref, w_ref, x_ref, g_ref, b_ref, *y_refs):
    y = _layer_norm(DEEPNORM_ALPHA * x_ref[...] + _dot(a_ref[...], w_ref[...]), g_ref[...], b_ref[...])
    for y_ref in y_refs:
        y_ref[...] = y.astype(y_ref.dtype)


def _matmul_res_ln(a, w, l, x, g, b, *, out_dtypes):
    m, k = a.shape
    n = w.shape[2]
    tm = min(m, 512 if k <= D_MODEL else 256)
    return pl.pallas_call(
        _res_ln_kernel,
        grid=(m // tm,),
        in_specs=[pl.BlockSpec((tm, k), lambda i: (i, 0)),
                  pl.BlockSpec((None, k, n), lambda i: (l, 0, 0), pipeline_mode=pl.Buffered(1)),
                  pl.BlockSpec((tm, n), lambda i: (i, 0)),
                  pl.BlockSpec((1, n), lambda i: (0, 0)),
                  pl.BlockSpec((1, n), lambda i: (0, 0))],
        out_specs=[pl.BlockSpec((tm, n), lambda i: (i, 0)) for _ in out_dtypes],
        out_shape=[jax.ShapeDtypeStruct((m, n), d) for d in out_dtypes],
        compiler_params=pltpu.CompilerParams(dimension_semantics=("parallel",), vmem_limit_bytes=VMEM_LIMIT_BIG),
        name="matmul_res_ln",
    )(a, w, x, g.reshape(1, n), b.reshape(1, n))


def _ffn_up_kernel(x_ref, wg_ref, wu_ref, o_ref):
    x = x_ref[...]
    o_ref[...] = (jax.nn.silu(_dot(x, wg_ref[...])) * _dot(x, wu_ref[...])).astype(o_ref.dtype)


def _ffn_up(x, w_up, l):
    m, k = x.shape
    tm = min(m, 1024)
    tn = 512
    nb = D_FF // tn
    return pl.pallas_call(
        _ffn_up_kernel,
        grid=(nb, m // tm),
        in_specs=[pl.BlockSpec((tm, k), lambda j, i: (i, 0)),
                  pl.BlockSpec((None, k, tn), lambda j, i: (l, 0, j)),
                  pl.BlockSpec((None, k, tn), lambda j, i: (l, 0, nb + j))],
        out_specs=pl.BlockSpec((tm, tn), lambda j, i: (i, j)),
        out_shape=jax.ShapeDtypeStruct((m, D_FF), BF16),
        compiler_params=_cparams("parallel", "arbitrary"),
        name="ffn_up",
    )(x, w_up, w_up)


def _cumsum_lanes_kernel(x_ref, o_ref):
    x = x_ref[...]
    n = x.shape[-1]
    lane = lax.broadcasted_iota(jnp.int32, x.shape, 1)
    sh = 1
    while sh < n:
        x = x + jnp.where(lane >= sh, pltpu.roll(x, sh, axis=1), 0.0)
        sh *= 2
    o_ref[...] = x


def _cumsum_lanes(x):
    return pl.pallas_call(_cumsum_lanes_kernel, out_shape=jax.ShapeDtypeStruct(x.shape, F32),
                          name="cumsum_lanes")(x)


def _flash_kernel(q_ref, k_ref, v_ref, cq_ref, ck_ref, o_ref, qt_sc, m_sc, l_sc, acc_sc, *, tq, tk, scale):
    qi = pl.program_id(2)
    ki = pl.program_id(3)

    @pl.when(ki == 0)
    def _():
        qt_sc[...] = q_ref[...].astype(F32).T.astype(BF16)
        m_sc[...] = jnp.full_like(m_sc, -jnp.inf)
        l_sc[...] = jnp.zeros_like(l_sc)
        acc_sc[...] = jnp.zeros_like(acc_sc)

    def update(masked):
        s = _dot(k_ref[...], qt_sc[...]) * scale
        s = s + cq_ref[0, 0] - jnp.concatenate([ck_ref[0, 0]] * (tq // LANES), axis=1)
        if masked:
            k_pos = lax.broadcasted_iota(jnp.int32, (tk, tq), 0)
            q_pos = lax.broadcasted_iota(jnp.int32, (tk, tq), 1)
            s = jnp.where(k_pos <= q_pos, s, -jnp.inf)
        m_prev = m_sc[...]
        m_new = jnp.maximum(m_prev, jnp.max(s, axis=0, keepdims=True))
        alpha = jnp.exp(m_prev - m_new)
        p = jnp.exp(s - m_new)
        l_sc[...] = alpha * l_sc[...] + jnp.sum(p, axis=0, keepdims=True)
        acc_sc[...] = alpha * acc_sc[...] + _dot_tn(v_ref[...], p.astype(BF16))
        m_sc[...] = m_new

    pl.when(ki < qi)(functools.partial(update, False))
    pl.when(ki == qi)(functools.partial(update, True))

    @pl.when(ki == pl.num_programs(3) - 1)
    def _():
        o_ref[...] = (acc_sc[...] / l_sc[...]).T.astype(o_ref.dtype)


def _fox_prompt(q, kv, c, bsz, seq):
    tq = tk = min(seq, 512)
    nq = seq // tq
    nk = seq // tk
    cq = c.reshape(bsz, FOX_H, 1, seq)
    ck = jnp.broadcast_to(c[..., None], (bsz, FOX_H, seq, LANES))

    def kv_blk(qi, ki):
        return jnp.minimum(ki, qi)

    return pl.pallas_call(
        functools.partial(_flash_kernel, tq=tq, tk=tk, scale=FOX_HD ** -0.5),
        grid=(bsz, FOX_H, nq, nk),
        in_specs=[pl.BlockSpec((tq, FOX_HD), lambda b, h, qi, ki: (b * nq + qi, h)),
                  pl.BlockSpec((tk, FOX_HD), lambda b, h, qi, ki: (b * nk + kv_blk(qi, ki), h)),
                  pl.BlockSpec((tk, FOX_HD), lambda b, h, qi, ki: (b * nk + kv_blk(qi, ki), FOX_H + h)),
                  pl.BlockSpec((1, 1, 1, tq), lambda b, h, qi, ki: (b, h, 0, qi)),
                  pl.BlockSpec((1, 1, tk, LANES), lambda b, h, qi, ki: (b, h, kv_blk(qi, ki), 0))],
        out_specs=pl.BlockSpec((tq, FOX_HD), lambda b, h, qi, ki: (b * nq + qi, h)),
        out_shape=jax.ShapeDtypeStruct((bsz * seq, MIX_W), BF16),
        scratch_shapes=[pltpu.VMEM((FOX_HD, tq), BF16), pltpu.VMEM((1, tq), F32), pltpu.VMEM((1, tq), F32),
                        pltpu.VMEM((FOX_HD, tq), F32)],
        compiler_params=_cparams("parallel", "parallel", "parallel", "arbitrary"),
        name="fox_prompt",
    )(q, kv, kv, cq, ck)


def _fox_decode_kernel(pt_ref, q_ref, cnew_ref, knew_ref, vnew_ref, *rest, scale, page, group, nseq):
    n = nseq * group
    o_ref, m_sc, l_sc, acc_sc, carry_sc = rest[3 * n:]
    p = pl.program_id(1)

    @pl.when(p == 0)
    def _():
        m_sc[...] = jnp.full_like(m_sc, -jnp.inf)
        l_sc[...] = jnp.zeros_like(l_sc)
        acc_sc[...] = jnp.zeros_like(acc_sc)
        carry_sc[...] = jnp.zeros_like(carry_sc)

    for r in range(nseq):
        sl = slice(r * group, (r + 1) * group)
        _fox_decode_pages(q_ref.at[r], cnew_ref.at[r], rest[:n][sl], rest[n:2 * n][sl], rest[2 * n:3 * n][sl],
                          m_sc.at[r], l_sc.at[r], acc_sc.at[r], carry_sc.at[r], scale=scale, page=page, group=group)

    @pl.when(p == pl.num_programs(1) - 1)
    def _():
        for r in range(nseq):
            _fox_decode_finish(q_ref.at[r], knew_ref.at[r], vnew_ref.at[r], o_ref.at[r], m_sc.at[r], l_sc.at[r],
                               acc_sc.at[r], scale=scale)


def _fox_decode_pages(q_ref, cnew_ref, k_refs, v_refs, lf_refs, m_sc, l_sc, acc_sc, carry_sc, *, scale, page, group):
    hcol = lax.broadcasted_iota(jnp.int32, (BF16_ROWS, 1), 0)
    later = jnp.where(lax.broadcasted_iota(jnp.int32, (page, page), 0)
                      > lax.broadcasted_iota(jnp.int32, (page, page), 1), 1.0, 0.0).astype(BF16)

    q = q_ref[...]
    carry = carry_sc[...]
    cnew = cnew_ref[...]
    parts = []
    for j in range(group):
        lf = lf_refs[j][...]
        lf16 = jnp.zeros((BF16_ROWS, page), F32)
        for h in range(FOX_H):
            lf16 = jnp.where(hcol == h, lf[h:h + 1, :], lf16)
        hi, mid, lo = _split3(lf16)
        suffix = _dot(hi, later) + _dot(mid, later) + _dot(lo, later)
        s = None
        for h in range(FOX_H):
            kh = k_refs[j][pl.ds(h, page, stride=FOX_H), :]
            t = _dot_nt(q[:, h * FOX_HD:(h + 1) * FOX_HD], kh.astype(BF16))
            s = t if s is None else s + t
        parts.append(s * scale + (suffix + carry + cnew))
        carry = carry + jnp.sum(lf16, axis=-1, keepdims=True)
    carry_sc[...] = carry
    s = jnp.concatenate(parts, axis=1)

    m_prev = m_sc[...]
    m_new = jnp.maximum(m_prev, jnp.max(s, axis=-1, keepdims=True))
    alpha = jnp.exp(m_prev - m_new)
    pr = jnp.exp(s - m_new)
    l_sc[...] = alpha * l_sc[...] + jnp.sum(pr, axis=-1, keepdims=True)
    pr = pr.astype(BF16)
    for h in range(FOX_H):
        cols = slice(h * FOX_HD, (h + 1) * FOX_HD)
        upd = None
        for j in range(group):
            vh = v_refs[j][pl.ds(h, page, stride=FOX_H), :]
            t = _dot(pr[:, j * page:(j + 1) * page], vh.astype(BF16))
            upd = t if upd is None else upd + t
        acc_sc[:, cols] = alpha * acc_sc[:, cols] + upd
    m_sc[...] = m_new


def _fox_decode_finish(q_ref, knew_ref, vnew_ref, o_ref, m_sc, l_sc, acc_sc, *, scale):
    hrow = lax.broadcasted_iota(jnp.int32, (BF16_ROWS, MIX_W), 0)
    lane = lax.broadcasted_iota(jnp.int32, (BF16_ROWS, MIX_W), 1)
    kn = knew_ref[...].astype(BF16).astype(F32)
    vn = vnew_ref[...].astype(BF16).astype(F32)
    s_new = jnp.sum(q_ref[...].astype(F32) * kn, axis=-1, keepdims=True) * scale
    m_prev = m_sc[...]
    m_fin = jnp.maximum(m_prev, s_new)
    alpha = jnp.exp(m_prev - m_fin)
    p_new = jnp.exp(s_new - m_fin)
    l_fin = alpha * l_sc[...] + p_new
    acc = alpha * acc_sc[...] + p_new.astype(BF16).astype(F32) * vn
    o = acc / l_fin
    o_ref[...] = jnp.sum(jnp.where((lane >> 7) == hrow, o, 0.0), axis=0, keepdims=True)


def _fox_decode(page_table, q, k_new, v_new, c_new, cache_k, cache_v, cache_lf, l):
    bsz, n_pages = page_table.shape
    depth, n_phys, page = cache_k.shape[:3]
    group = math.gcd(n_pages, FOX_DECODE_PAGES)
    cache_k = cache_k.reshape(depth, n_phys, page * FOX_H, FOX_HD)
    cache_v = cache_v.reshape(depth, n_phys, page * FOX_H, FOX_HD)
    cache_lf = cache_lf.transpose(0, 1, 3, 2)
    hmask = (np.arange(MIX_W)[None, :] // FOX_HD) == np.arange(BF16_ROWS)[:, None]
    q_rows = jnp.where(hmask[None], q[:, None, :], 0.0).astype(BF16)
    c_rows = jnp.pad(c_new, ((0, 0), (0, BF16_ROWS - FOX_H)))[:, :, None]
    pt = page_table.reshape(-1)

    nseq = math.gcd(bsz, FOX_DECODE_SEQS)
    slots = [(r, j) for r in range(nseq) for j in range(group)]

    def pg(r, j):
        return lambda b, p, pt_ref: pt_ref[(b * nseq + r) * n_pages + (n_pages - 1 - (p * group + j))]

    def kv_spec(r, j):
        return pl.BlockSpec((None, None, page * FOX_H, FOX_HD),
                            lambda b, p, pt_ref: (l, pg(r, j)(b, p, pt_ref), 0, 0))

    def lf_spec(r, j):
        return pl.BlockSpec((None, None, FOX_H, page), lambda b, p, pt_ref: (l, pg(r, j)(b, p, pt_ref), 0, 0))

    per_seq = lambda *shape: pl.BlockSpec((nseq,) + shape, lambda b, p, pt_ref: (b,) + (0,) * len(shape))
    grid_spec = pltpu.PrefetchScalarGridSpec(
        num_scalar_prefetch=1,
        grid=(bsz // nseq, n_pages // group),
        in_specs=[per_seq(BF16_ROWS, MIX_W), per_seq(BF16_ROWS, 1), per_seq(1, MIX_W), per_seq(1, MIX_W)]
                 + [kv_spec(r, j) for r, j in slots] + [kv_spec(r, j) for r, j in slots]
                 + [lf_spec(r, j) for r, j in slots],
        out_specs=per_seq(1, MIX_W),
        scratch_shapes=[pltpu.VMEM((nseq, BF16_ROWS, 1), F32), pltpu.VMEM((nseq, BF16_ROWS, 1), F32),
                        pltpu.VMEM((nseq, BF16_ROWS, MIX_W), F32), pltpu.VMEM((nseq, BF16_ROWS, 1), F32)],
    )
    out = pl.pallas_call(
        functools.partial(_fox_decode_kernel, scale=FOX_HD ** -0.5, page=page, group=group, nseq=nseq),
        grid_spec=grid_spec,
        out_shape=jax.ShapeDtypeStruct((bsz, 1, MIX_W), F32),
        compiler_params=_cparams("parallel", "arbitrary"),
        name="fox_decode",
    )(pt, q_rows, c_rows, k_new[:, None, :], v_new[:, None, :],
      *([cache_k] * len(slots)), *([cache_v] * len(slots)), *([cache_lf] * len(slots)))
    return out.reshape(bsz, MIX_W)


def _s5_kernel(u_ref, bblk_ref, cblk_ref, are_ref, aim_ref, d_ref, wglu_ref, bglu_ref, h0re_ref, h0im_ref,
               o_ref, hre_ref, him_ref, hs_sc, *, tc, bk, slab):
    c = pl.program_id(0)

    @pl.when(c == 0)
    def _():
        hre_ref[...] = h0re_ref[...]
        him_ref[...] = h0im_ref[...]

    u = u_ref[...]
    u_bf = u.astype(BF16)
    n_u = slab // S5_P * S5_GROUP
    y_parts = []
    for si, s0 in enumerate(range(0, S5_STATE, slab)):
        re_sl = slice(s0, s0 + slab)
        im_sl = slice(S5_STATE + s0, S5_STATE + s0 + slab)
        u_sl = slice(si * n_u, (si + 1) * n_u)
        bu = _dot(u_bf[:, u_sl], bblk_ref[u_sl, :])
        hs_sc[:, re_sl] = bu[:, :slab]
        hs_sc[:, im_sl] = bu[:, slab:]
        ar = jnp.broadcast_to(are_ref[:, re_sl], (bk, slab))
        ai = jnp.broadcast_to(aim_ref[:, re_sl], (bk, slab))

        def step(t, carry, re_sl=re_sl, im_sl=im_sl, ar=ar, ai=ai):
            hr, hi = carry
            rows = pl.ds(pl.multiple_of(t * bk, bk), bk)
            nr = ar * hr - ai * hi + hs_sc[rows, re_sl]
            ni = ar * hi + ai * hr + hs_sc[rows, im_sl]
            hs_sc[rows, re_sl] = nr
            hs_sc[rows, im_sl] = ni
            return nr, ni

        hr, hi = lax.fori_loop(0, tc, step, (hre_ref[:, re_sl], him_ref[:, re_sl]))
        hre_ref[:, re_sl] = hr
        him_ref[:, re_sl] = hi
        h_bf = jnp.concatenate([hs_sc[:, re_sl], hs_sc[:, im_sl]], axis=1).astype(BF16)
        y_parts.append(_dot(h_bf, cblk_ref[si * 2 * slab:(si + 1) * 2 * slab, :]))
    y = jnp.concatenate(y_parts, axis=1) + d_ref[...] * u
    g = jax.nn.gelu(y)
    gate = jax.nn.sigmoid(_dot(g.astype(BF16), wglu_ref[...]) + bglu_ref[...])
    o_ref[...] = (g * gate).astype(o_ref.dtype)


def _s5_params(lam_re, lam_im, log_dt, b_re, b_im, c_re, c_im):
    dt = jnp.exp(log_dt)[:, None]
    mag = jnp.exp(lam_re * dt)
    ab_re, ab_im = mag * jnp.cos(lam_im * dt), mag * jnp.sin(lam_im * dt)
    den = lam_re * lam_re + lam_im * lam_im
    nr, ni = ab_re - 1.0, ab_im
    coef_re = (nr * lam_re + ni * lam_im) / den
    coef_im = (ni * lam_re - nr * lam_im) / den
    bb_re = coef_re[..., None] * b_re - coef_im[..., None] * b_im
    bb_im = coef_re[..., None] * b_im + coef_im[..., None] * b_re
    gs = S5_SLAB // S5_P
    ns = S5_G // gs
    eye = jnp.eye(gs, dtype=F32)
    blk_b = lambda t: jnp.einsum("sgph,gk->sghkp", t.reshape(ns, gs, S5_P, S5_GROUP), eye).reshape(MIX_W, S5_SLAB)
    blk_c = lambda t: jnp.einsum("sghp,gk->sgpkh", t.reshape(ns, gs, S5_GROUP, S5_P), eye).reshape(
        ns, S5_SLAB, gs * S5_GROUP)
    bblk = jnp.concatenate([blk_b(bb_re), blk_b(bb_im)], axis=1).astype(BF16)
    cblk = jnp.concatenate([blk_c(c_re), blk_c(-c_im)], axis=1).astype(BF16)
    return (ab_re.reshape(1, S5_STATE), ab_im.reshape(1, S5_STATE), bblk,
            cblk.reshape(ns * 2 * S5_SLAB, gs * S5_GROUP))


def _s5(u_tb, params, d, w_glu, b_glu, h0_re, h0_im, *, bk, steps):
    a_re, a_im, bblk, cblk = params
    tc = min(steps, 64)
    rows = tc * bk
    const = lambda shape: pl.BlockSpec(shape, lambda c: (0,) * len(shape))
    return pl.pallas_call(
        functools.partial(_s5_kernel, tc=tc, bk=bk, slab=S5_SLAB),
        grid=(steps // tc,),
        in_specs=[pl.BlockSpec((rows, MIX_W), lambda c: (c, 0)),
                  const(bblk.shape), const(cblk.shape),
                  const((1, S5_STATE)), const((1, S5_STATE)), const((1, MIX_W)),
                  const((MIX_W, MIX_W)), const((1, MIX_W)),
                  const((bk, S5_STATE)), const((bk, S5_STATE))],
        out_specs=[pl.BlockSpec((rows, MIX_W), lambda c: (c, 0)),
                   const((bk, S5_STATE)), const((bk, S5_STATE))],
        out_shape=[jax.ShapeDtypeStruct((steps * bk, MIX_W), BF16),
                   jax.ShapeDtypeStruct((bk, S5_STATE), F32), jax.ShapeDtypeStruct((bk, S5_STATE), F32)],
        scratch_shapes=[pltpu.VMEM((rows, 2 * S5_STATE), F32)],
        compiler_params=_cparams("arbitrary"),
        name="s5",
    )(u_tb, bblk, cblk, a_re, a_im, d.reshape(1, MIX_W), w_glu, b_glu.reshape(1, MIX_W), h0_re, h0_im)


def _hgrn_kernel(q_ref, f_ref, i_ref, g_ref, lb_ref, ng_ref, tri_ref, s0_ref, o_ref, st_ref, *, c, sub, valid):
    n = pl.program_id(1)

    @pl.when(n == 0)
    def _():
        st_ref[...] = s0_ref[...]

    n_seq = q_ref.shape[0]
    lb = lb_ref[...]
    la = jnp.log(lb)
    lb1 = jnp.log1p(-lb)
    decay = []
    worst = jnp.zeros((1, MIX_W), F32)
    for r in range(n_seq):
        lbb = lb1 + _log_sigmoid(f_ref[r])
        lf = jnp.maximum(la, lbb) + jnp.log1p(jnp.exp(-jnp.abs(la - lbb)))
        if valid < c:
            lf = jnp.where(lax.broadcasted_iota(jnp.int32, lf.shape, 0) < valid, lf, 0.0)
        parts = _dot(tri_ref[...], jnp.concatenate(_split3(lf), axis=1))
        bc = parts[:, :MIX_W] + parts[:, MIX_W:2 * MIX_W] + parts[:, 2 * MIX_W:]
        decay.append((1.0 - jnp.exp(lf), bc))
        mid = bc[c // 2 - 1:c // 2]
        worst = jnp.maximum(worst, jnp.maximum(-mid, mid - bc[c - 1:c]))
    safe = jnp.max(worst) < HG_SAFE_RANGE

    def run(exact):
        for r in range(n_seq):
            kt_all, bc_all = decay[r]
            for h in range(HG_H):
                cols = slice(h * HG_DK, (h + 1) * HG_DK)
                o, st = _hgrn_head(q_ref[r, :, cols], kt_all[:, cols], bc_all[:, cols], i_ref[r, :, cols],
                                   st_ref[r, h], c=c, sub=sub, exact=exact)
                st_ref[r, h] = st
                o = o * lax.rsqrt(jnp.mean(o * o, axis=-1, keepdims=True) + RMS_EPS) * ng_ref[:, cols]
                o_ref[r, :, cols] = (o * jax.nn.silu(g_ref[r, :, cols])).astype(o_ref.dtype)

    pl.when(safe)(functools.partial(run, False))
    pl.when(jnp.logical_not(safe))(functools.partial(run, True))


def _hgrn_head(q, kt, bc, v, st, *, c, sub, exact):
    v_bf = v.astype(BF16)
    o = _dot_nt((q * jnp.exp(bc)).astype(BF16), st.astype(BF16))
    if not exact:
        rel = bc - bc[c // 2 - 1:c // 2]
        att = _dot_nt((q * jnp.exp(rel)).astype(BF16), (kt * jnp.exp(-rel)).astype(BF16))
        t_pos = lax.broadcasted_iota(jnp.int32, (c, c), 0)
        s_pos = lax.broadcasted_iota(jnp.int32, (c, c), 1)
        o = o + _dot(jnp.where(s_pos <= t_pos, att, 0.0).astype(BF16), v_bf)
    else:
        nb = c // sub
        rows = lax.broadcasted_iota(jnp.int32, (sub, HG_DK), 0)
        blocks = []
        for i in range(nb):
            lo_r, hi_r = i * sub, (i + 1) * sub
            bci = bc[lo_r:hi_r]
            qi = q[lo_r:hi_r]
            oi = o[lo_r:hi_r]
            if i > 0:
                e = bc[lo_r - 1:lo_r]
                qs = (qi * jnp.exp(bci - e)).astype(BF16)
                ks = (kt[:lo_r] * jnp.exp(e - bc[:lo_r])).astype(BF16)
                oi = oi + _dot(_dot_nt(qs, ks).astype(BF16), v_bf[:lo_r])
            for s in range(sub):
                r = lo_r + s
                dec = jnp.exp(jnp.where(rows >= s, bci - bc[r:r + 1], -jnp.inf))
                a = jnp.sum(qi * dec * kt[r:r + 1], axis=-1, keepdims=True)
                oi = oi + a * v[r:r + 1]
            blocks.append(oi)
        o = jnp.concatenate(blocks, axis=0) if nb > 1 else blocks[0]
    bl = bc[c - 1:c]
    st = jnp.exp(bl) * st + _dot_tn(v_bf, (kt * jnp.exp(bl - bc)).astype(BF16))
    return o, st


def _hgrn(z, col0, lb, norm_g, s0_t, *, bsz, seq_pad, valid):
    c = min(HG_CHUNK, seq_pad)
    sub = min(HG_SUB, c)
    nc = seq_pad // c
    assert col0 % MIX_W == 0 and bsz % HG_SEQS == 0
    j0 = col0 // MIX_W
    z = z.reshape(bsz, seq_pad, z.shape[-1])
    tri = jnp.asarray(np.tril(np.ones((c, c))), BF16)
    col = lambda k: pl.BlockSpec((HG_SEQS, c, MIX_W), lambda b, n: (b, n, j0 + k))
    vec = pl.BlockSpec((1, MIX_W), lambda b, n: (0, 0))
    state = pl.BlockSpec((HG_SEQS, HG_H, HG_DV, HG_DK), lambda b, n: (b, 0, 0, 0))
    o, st = pl.pallas_call(
        functools.partial(_hgrn_kernel, c=c, sub=sub, valid=valid),
        grid=(bsz // HG_SEQS, nc),
        in_specs=[col(0), col(1), col(2), col(3), vec, vec, pl.BlockSpec((c, c), lambda b, n: (0, 0)), state],
        out_specs=[pl.BlockSpec((HG_SEQS, c, MIX_W), lambda b, n: (b, n, 0)), state],
        out_shape=[jax.ShapeDtypeStruct((bsz, seq_pad, MIX_W), BF16),
                   jax.ShapeDtypeStruct((bsz, HG_H, HG_DV, HG_DK), F32)],
        compiler_params=_cparams("parallel", "arbitrary"),
        name="hgrn2",
    )(z, z, z, z, lb.reshape(1, MIX_W), norm_g.reshape(1, MIX_W), tri, s0_t)
    return o.reshape(bsz * seq_pad, MIX_W), st


def _gmlp_kernel(u_ref, v_ref, lg_ref, lbias_ref, ws_ref, bs_ref, o_ref):
    vn = _layer_norm(jax.nn.gelu(v_ref[...]), lg_ref[...], lbias_ref[...]).astype(BF16)
    u = jax.nn.gelu(u_ref[...])
    r = lax.broadcasted_iota(jnp.int32, (GM_CHUNK, GM_CHUNK), 0)
    cc = lax.broadcasted_iota(jnp.int32, (GM_CHUNK, GM_CHUNK), 1)
    chunks = [slice(c * GM_CHUNK, (c + 1) * GM_CHUNK) for c in range(u.shape[0] // GM_CHUNK)]
    for g in range(GM_GROUPS):
        sl = slice(g * GM_GW, (g + 1) * GM_GW)
        ws = jnp.where(r >= cc, ws_ref[g], 0.0).astype(BF16)
        mixed = _dot(ws, jnp.concatenate([vn[rows, sl] for rows in chunks], axis=1))
        for c, rows in enumerate(chunks):
            m_c = mixed[:, c * GM_GW:(c + 1) * GM_GW] + bs_ref[g]
            o_ref[rows, sl] = (u[rows, sl] * m_c).astype(o_ref.dtype)


def _gmlp(z, col0, ln_g, ln_b, w_s, b_s):
    m = z.shape[0]
    assert col0 % MIX_W == 0
    j0 = col0 // MIX_W
    tm = math.gcd(m, GM_STEP_CHUNKS * GM_CHUNK)
    return pl.pallas_call(
        _gmlp_kernel,
        grid=(m // tm,),
        in_specs=[pl.BlockSpec((tm, MIX_W), lambda i: (i, j0)),
                  pl.BlockSpec((tm, MIX_W), lambda i: (i, j0 + 1)),
                  pl.BlockSpec((1, MIX_W), lambda i: (0, 0)),
                  pl.BlockSpec((1, MIX_W), lambda i: (0, 0)),
                  pl.BlockSpec((GM_GROUPS, GM_CHUNK, GM_CHUNK), lambda i: (0, 0, 0)),
                  pl.BlockSpec((GM_GROUPS, GM_CHUNK, 1), lambda i: (0, 0, 0))],
        out_specs=pl.BlockSpec((tm, MIX_W), lambda i: (i, 0)),
        out_shape=jax.ShapeDtypeStruct((m, MIX_W), BF16),
        compiler_params=_cparams("parallel"),
        name="gmlp",
    )(z, z, ln_g.reshape(1, MIX_W), ln_b.reshape(1, MIX_W), w_s, b_s[:, :, None])


def _gmlp_first_kernel(u_ref, v_ref, lg_ref, lbias_ref, w00_ref, b0_ref, o_ref, vn_ref):
    vn = _layer_norm(jax.nn.gelu(v_ref[...]), lg_ref[...], lbias_ref[...])
    vn_ref[...] = vn
    o_ref[...] = (jax.nn.gelu(u_ref[...]) * (vn * w00_ref[...] + b0_ref[...])).astype(o_ref.dtype)


def _gmlp_first(z, col0, ln_g, ln_b, w_s, b_s):
    m = z.shape[0]
    assert col0 % MIX_W == 0
    j0 = col0 // MIX_W
    w00 = jnp.repeat(w_s[:, 0, 0], GM_GW).reshape(1, MIX_W)
    b0 = jnp.repeat(b_s[:, 0], GM_GW).reshape(1, MIX_W)
    vec = pl.BlockSpec((1, MIX_W), lambda i: (0, 0))
    return pl.pallas_call(
        _gmlp_first_kernel,
        grid=(1,),
        in_specs=[pl.BlockSpec((m, MIX_W), lambda i: (0, j0)), pl.BlockSpec((m, MIX_W), lambda i: (0, j0 + 1)),
                  vec, vec, vec, vec],
        out_specs=[pl.BlockSpec((m, MIX_W), lambda i: (0, 0)), pl.BlockSpec((m, MIX_W), lambda i: (0, 0))],
        out_shape=[jax.ShapeDtypeStruct((m, MIX_W), BF16), jax.ShapeDtypeStruct((m, MIX_W), F32)],
        name="gmlp_first",
    )(z, z, ln_g.reshape(1, MIX_W), ln_b.reshape(1, MIX_W), w00, b0)


def _layer_weights(l, w_in, fox_bf, s5_lambda_re, s5_lambda_im, s5_log_dt, s5_b_re, s5_b_im, s5_c_re, s5_c_im,
                   s5_w_glu, hgrn_lb, w_branch, w_out, w_ffn_up, w_ffn_down):
    n_a = 3 * MIX_W + LANES
    o_r = 3 * MIX_W + FOX_H
    lb_all = jnp.cumsum(jax.nn.softmax(hgrn_lb.astype(F32), axis=0), axis=0)
    return dict(
        w_a=_cast_cols(w_in, l, 0, n_a),
        w_r=_cast_cols(w_in, l, o_r, w_in.shape[2] - o_r),
        bf=jnp.pad(fox_bf[l], (0, LANES - FOX_H)).reshape(1, LANES),
        s5=_s5_params(s5_lambda_re[l], s5_lambda_im[l], s5_log_dt[l], s5_b_re[l], s5_b_im[l],
                      s5_c_re[l], s5_c_im[l]),
        w_glu=s5_w_glu[l].astype(BF16),
        lb=lb_all[l] - lb_all[0],
        w_branch=w_branch, w_out=w_out, w_up=w_ffn_up, w_down=w_ffn_down,
    )


def _trunk_layer(x, l, lw, P, *, bsz, seq, fox_fn, s5_h0, hg_s0):
    m = bsz * seq
    x_bf, q_bf, kv_bf, k_rows, v_rows_fox, logf = _qkv_proj(x, lw["w_a"], lw["bf"])
    logf = logf[:, :FOX_H]
    o_a = fox_fn(q_bf, kv_bf, k_rows, v_rows_fox, logf)

    z = _proj(x_bf, lw["w_r"], n=Z_GATE)[0]
    su_tb = z[:, :MIX_W].reshape(bsz, seq, MIX_W).transpose(1, 0, 2).reshape(m, MIX_W)
    o_b_tb, s5_re, s5_im = _s5(su_tb, lw["s5"], P["s5_d"][l], lw["w_glu"], P["s5_b_glu"][l],
                               s5_h0[0].reshape(bsz, S5_STATE), s5_h0[1].reshape(bsz, S5_STATE),
                               bk=bsz, steps=seq)
    o_b = o_b_tb.reshape(seq, bsz, MIX_W).transpose(1, 0, 2).reshape(m, MIX_W)

    seq_pad = seq if seq % HG_CHUNK == 0 else -(-seq // BF16_ROWS) * BF16_ROWS
    zh, zh_col0 = z, Z_HG
    if seq_pad != seq:
        zh = z[:, Z_HG:Z_GM].reshape(bsz, seq, 4 * MIX_W)
        zh, zh_col0 = jnp.pad(zh, ((0, 0), (0, seq_pad - seq), (0, 0))).reshape(-1, 4 * MIX_W), 0
    o_c, hg_t = _hgrn(zh, zh_col0, lw["lb"], P["hgrn_norm_g"][l], hg_s0.transpose(0, 1, 3, 2),
                      bsz=bsz, seq_pad=seq_pad, valid=min(seq, HG_CHUNK))
    if seq_pad != seq:
        o_c = o_c.reshape(bsz, seq_pad, MIX_W)[:, :seq].reshape(m, MIX_W)
    hg_state = hg_t.transpose(0, 1, 3, 2)

    if seq == 1:
        o_d, v_rows = _gmlp_first(z, Z_GM, P["gmlp_ln_g"][l], P["gmlp_ln_b"][l], P["gmlp_w_s"][l], P["gmlp_b_s"][l])
    else:
        o_d = _gmlp(z, Z_GM, P["gmlp_ln_g"][l], P["gmlp_ln_b"][l], P["gmlp_w_s"][l], P["gmlp_b_s"][l])
        v_rows = None

    merged = _merge(x_bf, (o_a, o_b, o_c, o_d), lw["w_r"], Z_GATE, lw["w_branch"], l)
    x1, x1_bf = _matmul_res_ln(merged, lw["w_out"], l, x, P["ln1_g"][l], P["ln1_b"][l], out_dtypes=(F32, BF16))
    hff = _ffn_up(x1_bf, lw["w_up"], l)
    x2, = _matmul_res_ln(hff, lw["w_down"], l, x1, P["ln2_g"][l], P["ln2_b"][l], out_dtypes=(F32,))

    fk = k_rows.reshape(bsz, seq, FOX_H, FOX_HD)
    fv = v_rows_fox.reshape(bsz, seq, FOX_H, FOX_HD)
    state = (fk, fv, logf.reshape(bsz, seq, FOX_H), s5_re.reshape(bsz, S5_G, S5_P),
             s5_im.reshape(bsz, S5_G, S5_P), hg_state, v_rows)
    return x2, state


def kernel(x_prompt, x_sample, cache_k, cache_v, cache_logf, page_table, state_s5_re, state_s5_im, state_hgrn,
           w_in, fox_bf, s5_lambda_re, s5_lambda_im, s5_log_dt, s5_b_re, s5_b_im, s5_c_re, s5_c_im, s5_d,
           s5_w_glu, s5_b_glu, hgrn_lb, hgrn_norm_g, gmlp_ln_g, gmlp_ln_b, gmlp_w_s, gmlp_b_s, w_branch, w_out,
           ln1_g, ln1_b, w_ffn_up, w_ffn_down, ln2_g, ln2_b):
    b_p, l_p, _ = x_prompt.shape
    b_s, l_s, _ = x_sample.shape
    assert l_s == 1, "the sample group decodes one token per sequence"
    P = dict(s5_d=s5_d, s5_b_glu=s5_b_glu, hgrn_norm_g=hgrn_norm_g, gmlp_ln_g=gmlp_ln_g, gmlp_ln_b=gmlp_ln_b,
             gmlp_w_s=gmlp_w_s, gmlp_b_s=gmlp_b_s, ln1_g=ln1_g, ln1_b=ln1_b, ln2_g=ln2_g, ln2_b=ln2_b)

    xp = x_prompt.reshape(b_p * l_p, D_MODEL)
    xs = x_sample.reshape(b_s * l_s, D_MODEL)
    s5_zero = jnp.zeros((b_p, S5_G, S5_P), F32)
    hg_zero = jnp.zeros((b_p, HG_H, HG_DK, HG_DV), F32)
    p_states, s_states = [], []
    w_branch, w_out, w_ffn_up, w_ffn_down = (w.astype(BF16) for w in (w_branch, w_out, w_ffn_up, w_ffn_down))
    for l in range(DEPTH):
        lw = _layer_weights(l, w_in, fox_bf, s5_lambda_re, s5_lambda_im, s5_log_dt, s5_b_re, s5_b_im,
                            s5_c_re, s5_c_im, s5_w_glu, hgrn_lb, w_branch, w_out, w_ffn_up, w_ffn_down)

        def fox_p(q_bf, kv_bf, k_rows, v_rows, logf):
            c = _cumsum_lanes(logf.reshape(b_p, l_p, FOX_H).transpose(0, 2, 1).reshape(b_p * FOX_H, l_p))
            return _fox_prompt(q_bf, kv_bf, c.reshape(b_p, FOX_H, l_p), b_p, l_p)

        def fox_s(q_bf, kv_bf, k_rows, v_rows, logf, l=l):
            o = _fox_decode(page_table, q_bf, k_rows.reshape(b_s, MIX_W), v_rows.reshape(b_s, MIX_W),
                            logf, cache_k, cache_v, cache_logf, l)
            return o.astype(BF16)

        xp, sp = _trunk_layer(xp, l, lw, P, bsz=b_p, seq=l_p, fox_fn=fox_p,
                              s5_h0=(s5_zero, s5_zero), hg_s0=hg_zero)
        xs, ss = _trunk_layer(xs, l, lw, P, bsz=b_s, seq=l_s, fox_fn=fox_s,
                              s5_h0=(state_s5_re[l], state_s5_im[l]), hg_s0=state_hgrn[l])
        p_states.append(sp)
        s_states.append(ss)

    stack = lambda states, i: jnp.stack([s[i] for s in states])
    return (xp.reshape(b_p, l_p, D_MODEL), xs.reshape(b_s, l_s, D_MODEL),
            stack(p_states, 0), stack(p_states, 1), stack(p_states, 2),
            stack(s_states, 0), stack(s_states, 1), stack(s_states, 2),
            stack(p_states, 3), stack(p_states, 4), stack(s_states, 3), stack(s_states, 4),
            stack(p_states, 5), stack(s_states, 5),
            jnp.stack([s[6].reshape(b_s, l_s, MIX_W) for s in s_states]))
```

```python
import functools
import math

import numpy as np
import jax
import jax.numpy as jnp
from jax import lax
from jax.experimental import pallas as pl
from jax.experimental.pallas import tpu as pltpu

F32 = jnp.float32
BF16 = jnp.bfloat16

D_MODEL = 2048
DEPTH = 2
N_BRANCH = 4
MIX_W = D_MODEL // N_BRANCH
FOX_HD = 128
FOX_H = MIX_W // FOX_HD
S5_GROUP = 16
S5_G = MIX_W // S5_GROUP
S5_P = 64
S5_STATE = S5_G * S5_P
S5_SLAB = 512
HG_DK = 128
HG_DV = 128
HG_H = MIX_W // HG_DV
HG_CHUNK = 64
HG_SUB = 16
HG_SEQS = 4
HG_SAFE_RANGE = 44.0
GM_CHUNK = 128
GM_GROUPS = 4
GM_GW = MIX_W // GM_GROUPS
GM_STEP_CHUNKS = 8
D_FF = ((8 * D_MODEL // 3 + 255) // 256) * 256
DEEPNORM_ALPHA = (2 * DEPTH) ** 0.25
LN_EPS = 1e-5
RMS_EPS = 1e-6
LANES = 128
SUBLANES = 8
BF16_ROWS = 16
FOX_DECODE_PAGES = 32
FOX_DECODE_SEQS = 1
Z_HG = MIX_W
Z_GM = Z_HG + 4 * MIX_W
Z_GATE = Z_GM + 2 * MIX_W
VMEM_LIMIT = 48 * 1024 * 1024
VMEM_LIMIT_BIG = 58 * 1024 * 1024
PROJ_TN_MAX = 1792


def _cparams(*sem):
    return pltpu.CompilerParams(dimension_semantics=sem, vmem_limit_bytes=VMEM_LIMIT)


def _log_sigmoid(x):
    return jnp.minimum(x, 0.0) - jnp.log1p(jnp.exp(-jnp.abs(x)))


def _layer_norm(x, g, b):
    mu = jnp.mean(x, axis=-1, keepdims=True)
    xc = x - mu
    var = jnp.mean(xc * xc, axis=-1, keepdims=True)
    return xc * lax.rsqrt(var + LN_EPS) * g + b


def _split3(x):
    hi = x.astype(BF16)
    r1 = x - hi.astype(F32)
    mid = r1.astype(BF16)
    lo = (r1 - mid.astype(F32)).astype(BF16)
    return hi, mid, lo


def _dot(a, b):
    return jnp.dot(a, b, preferred_element_type=F32)


def _dot_nt(a, b):
    return lax.dot_general(a, b, (((1,), (1,)), ((), ())), preferred_element_type=F32)


def _dot_tn(a, b):
    return lax.dot_general(a, b, (((0,), (0,)), ((), ())), preferred_element_type=F32)


def _cast_cols_kernel(w_ref, o_ref, *, l, depth):
    n_kt = o_ref.shape[0] // LANES
    rows_per_col = n_kt * depth
    for cb in range(o_ref.shape[1] // LANES):
        for kt in range(n_kt):
            first = cb * LANES * rows_per_col + kt * depth + l
            tile = w_ref[pl.ds(first, LANES, stride=rows_per_col), :]
            o_ref[kt * LANES:(kt + 1) * LANES, cb * LANES:(cb + 1) * LANES] = tile.T.astype(o_ref.dtype)


def _cast_cols(w, l, col0, n):
    depth, k, n_all = w.shape
    rows_per_col = (k // LANES) * depth
    view = w.reshape(depth, k // LANES, LANES, n_all).transpose(3, 1, 0, 2).reshape(n_all * rows_per_col, LANES)
    tn = max(t for t in (LANES, 2 * LANES, 4 * LANES) if n % t == 0)
    blk = tn * rows_per_col
    return pl.pallas_call(
        functools.partial(_cast_cols_kernel, l=l, depth=depth),
        grid=(n // tn,),
        in_specs=[pl.BlockSpec((pl.Element(blk), pl.Element(LANES)),
                               lambda j: (pl.multiple_of(col0 * rows_per_col + j * blk, rows_per_col), 0))],
        out_specs=pl.BlockSpec((k, tn), lambda j: (0, j)),
        out_shape=jax.ShapeDtypeStruct((k, n), BF16),
        compiler_params=_cparams("parallel"),
        name="cast_cols",
    )(view)


def _proj_kernel(x_ref, w_ref, b_ref, *out_refs, act):
    z = _dot(x_ref[...], w_ref[...])
    if act == "gelu":
        z = jax.nn.gelu(z)
    elif act == "sigmoid":
        z = jax.nn.sigmoid(z)
    elif act == "log_sigmoid_bias":
        z = _log_sigmoid(z + b_ref[...])
    for o in out_refs:
        o[...] = z.astype(o.dtype)


def _proj(x, w, *, col0=0, n=None, act=None, bias=None, out_dtypes=(F32,)):
    m, k = x.shape
    n = w.shape[1] if n is None else n
    tm = min(m, 512)
    tn = max(t for t in range(LANES, min(n, PROJ_TN_MAX) + 1, LANES) if n % t == 0 and col0 % t == 0)
    j0 = col0 // tn
    if bias is None:
        bias = jnp.zeros((1, n), F32)
    return pl.pallas_call(
        functools.partial(_proj_kernel, act=act),
        grid=(n // tn, m // tm),
        in_specs=[pl.BlockSpec((tm, k), lambda j, i: (i, 0)),
                  pl.BlockSpec((k, tn), lambda j, i: (0, j0 + j)),
                  pl.BlockSpec((1, tn), lambda j, i: (0, j))],
        out_specs=[pl.BlockSpec((tm, tn), lambda j, i: (i, j)) for _ in out_dtypes],
        out_shape=[jax.ShapeDtypeStruct((m, n), d) for d in out_dtypes],
        compiler_params=_cparams("parallel", "arbitrary"),
        name="proj_" + (act or "id"),
    )(x, w, bias)


def _qkv_kernel(x_ref, w_ref, b_ref, xbf_ref, q_ref, kv_ref, k_ref, v_ref, lf_ref):
    x = x_ref[...].astype(BF16)
    xbf_ref[...] = x
    z = _dot(x, w_ref[...])
    tm = z.shape[0]
    q_ref[...] = z[:, :MIX_W].astype(q_ref.dtype)
    kv_ref[...] = z[:, MIX_W:3 * MIX_W].astype(kv_ref.dtype)
    for h in range(FOX_H):
        k_ref[pl.ds(h, tm, stride=FOX_H), :] = z[:, MIX_W + h * FOX_HD:MIX_W + (h + 1) * FOX_HD]
        v_ref[pl.ds(h, tm, stride=FOX_H), :] = z[:, 2 * MIX_W + h * FOX_HD:2 * MIX_W + (h + 1) * FOX_HD]
    lf_ref[...] = _log_sigmoid(z[:, 3 * MIX_W:] + b_ref[...])


def _qkv_proj(x, w, bias):
    m, k = x.shape
    tm = min(m, 512)
    n = w.shape[1]
    row = lambda width: pl.BlockSpec((tm, width), lambda i: (i, 0))
    rows = pl.BlockSpec((tm * FOX_H, FOX_HD), lambda i: (i, 0))
    return pl.pallas_call(
        _qkv_kernel,
        grid=(m // tm,),
        in_specs=[row(k), pl.BlockSpec((k, n), lambda i: (0, 0), pipeline_mode=pl.Buffered(1)),
                  pl.BlockSpec((1, LANES), lambda i: (0, 0))],
        out_specs=[row(k), row(MIX_W), row(2 * MIX_W), rows, rows, row(LANES)],
        out_shape=[jax.ShapeDtypeStruct((m, k), BF16), jax.ShapeDtypeStruct((m, MIX_W), BF16),
                   jax.ShapeDtypeStruct((m, 2 * MIX_W), BF16),
                   jax.ShapeDtypeStruct((m * FOX_H, FOX_HD), F32), jax.ShapeDtypeStruct((m * FOX_H, FOX_HD), F32),
                   jax.ShapeDtypeStruct((m, LANES), F32)],
        compiler_params=_cparams("parallel"),
        name="qkv_proj",
    )(x, w, bias)


def _merge_kernel(x_ref, oa_ref, ob_ref, oc_ref, od_ref, g0_ref, g1_ref, g2_ref, g3_ref, wb_ref, o_ref):
    x = x_ref[...]
    acc = None
    for br, gr, k in ((oa_ref, g0_ref, 0), (ob_ref, g1_ref, 1), (oc_ref, g2_ref, 2), (od_ref, g3_ref, 3)):
        t = jax.nn.sigmoid(_dot(x, gr[...])) * _dot(br[...], wb_ref[k])
        acc = t if acc is None else acc + t
    o_ref[...] = acc.astype(o_ref.dtype)


def _merge(x, branches, w, gate_col0, w_branch, l):
    m, k_in = x.shape
    tm = min(m, 512)
    tn = 512
    nb = D_MODEL // tn
    assert gate_col0 % tn == 0
    j0 = gate_col0 // tn
    gate_specs = [pl.BlockSpec((k_in, tn), functools.partial(lambda j, i, k: (0, j0 + k * nb + j), k=k))
                  for k in range(N_BRANCH)]
    return pl.pallas_call(
        _merge_kernel,
        grid=(nb, m // tm),
        in_specs=[pl.BlockSpec((tm, k_in), lambda j, i: (i, 0))]
                 + [pl.BlockSpec((tm, MIX_W), lambda j, i: (i, 0)) for _ in range(N_BRANCH)] + gate_specs
                 + [pl.BlockSpec((None, N_BRANCH, MIX_W, tn), lambda j, i: (l, 0, 0, j))],
        out_specs=pl.BlockSpec((tm, tn), lambda j, i: (i, j)),
        out_shape=jax.ShapeDtypeStruct((m, D_MODEL), BF16),
        compiler_params=_cparams("parallel", "arbitrary"),
        name="merge",
    )(x, *branches, w, w, w, w, w_branch)


def _res_ln_kernel(a_ref, w_ref, x_ref, g_ref, b_ref, *y_refs):
    y = _layer_norm(DEEPNORM_ALPHA * x_ref[...] + _dot(a_ref[...], w_ref[...]), g_ref[...], b_ref[...])
    for y_ref in y_refs:
        y_ref[...] = y.astype(y_ref.dtype)


def _matmul_res_ln(a, w, l, x, g, b, *, out_dtypes):
    m, k = a.shape
    n = w.shape[2]
    tm = min(m, 512 if k <= D_MODEL else 256)
    return pl.pallas_call(
        _res_ln_kernel,
        grid=(m // tm,),
        in_specs=[pl.BlockSpec((tm, k), lambda i: (i, 0)),
                  pl.BlockSpec((None, k, n), lambda i: (l, 0, 0), pipeline_mode=pl.Buffered(1)),
                  pl.BlockSpec((tm, n), lambda i: (i, 0)),
                  pl.BlockSpec((1, n), lambda i: (0, 0)),
                  pl.BlockSpec((1, n), lambda i: (0, 0))],
        out_specs=[pl.BlockSpec((tm, n), lambda i: (i, 0)) for _ in out_dtypes],
        out_shape=[jax.ShapeDtypeStruct((m, n), d) for d in out_dtypes],
        compiler_params=pltpu.CompilerParams(dimension_semantics=("parallel",), vmem_limit_bytes=VMEM_LIMIT_BIG),
        name="matmul_res_ln",
    )(a, w, x, g.reshape(1, n), b.reshape(1, n))


def _ffn_up_kernel(x_ref, wg_ref, wu_ref, o_ref):
    x = x_ref[...]
    o_ref[...] = (jax.nn.silu(_dot(x, wg_ref[...])) * _dot(x, wu_ref[...])).astype(o_ref.dtype)


def _ffn_up(x, w_up, l):
    m, k = x.shape
    tm = min(m, 1024)
    tn = 512
    nb = D_FF // tn
    return pl.pallas_call(
        _ffn_up_kernel,
        grid=(nb, m // tm),
        in_specs=[pl.BlockSpec((tm, k), lambda j, i: (i, 0)),
                  pl.BlockSpec((None, k, tn), lambda j, i: (l, 0, j)),
                  pl.BlockSpec((None, k, tn), lambda j, i: (l, 0, nb + j))],
        out_specs=pl.BlockSpec((tm, tn), lambda j, i: (i, j)),
        out_shape=jax.ShapeDtypeStruct((m, D_FF), BF16),
        compiler_params=_cparams("parallel", "arbitrary"),
        name="ffn_up",
    )(x, w_up, w_up)


def _cumsum_lanes_kernel(x_ref, o_ref):
    x = x_ref[...]
    n = x.shape[-1]
    lane = lax.broadcasted_iota(jnp.int32, x.shape, 1)
    sh = 1
    while sh < n:
        x = x + jnp.where(lane >= sh, pltpu.roll(x, sh, axis=1), 0.0)
        sh *= 2
    o_ref[...] = x


def _cumsum_lanes(x):
    return pl.pallas_call(_cumsum_lanes_kernel, out_shape=jax.ShapeDtypeStruct(x.shape, F32),
                          name="cumsum_lanes")(x)


def _flash_kernel(qi_ref, ki_ref, q_ref, k_ref, v_ref, cq_ref, ck_ref, o_ref, qt_sc, m_sc, l_sc, acc_sc, *,
                  tq, tk, scale):
    qi = qi_ref[pl.program_id(2)]
    ki = ki_ref[pl.program_id(2)]

    @pl.when(ki == 0)
    def _():
        qt_sc[...] = q_ref[...].astype(F32).T.astype(BF16)
        m_sc[...] = jnp.full_like(m_sc, -jnp.inf)
        l_sc[...] = jnp.zeros_like(l_sc)
        acc_sc[...] = jnp.zeros_like(acc_sc)

    def update(masked):
        s = _dot(k_ref[...], qt_sc[...]) * scale
        s = s + cq_ref[0, 0] - jnp.concatenate([ck_ref[0, 0]] * (tq // LANES), axis=1)
        if masked:
            k_pos = lax.broadcasted_iota(jnp.int32, (tk, tq), 0)
            q_pos = lax.broadcasted_iota(jnp.int32, (tk, tq), 1)
            s = jnp.where(k_pos <= q_pos, s, -jnp.inf)
        m_prev = m_sc[...]
        m_new = jnp.maximum(m_prev, jnp.max(s, axis=0, keepdims=True))
        alpha = jnp.exp(m_prev - m_new)
        p = jnp.exp(s - m_new)
        l_sc[...] = alpha * l_sc[...] + jnp.sum(p, axis=0, keepdims=True)
        acc_sc[...] = alpha * acc_sc[...] + _dot_tn(v_ref[...], p.astype(BF16))
        m_sc[...] = m_new

    pl.when(ki < qi)(functools.partial(update, False))

    @pl.when(ki == qi)
    def _():
        update(True)
        o_ref[...] = (acc_sc[...] / l_sc[...]).T.astype(o_ref.dtype)


def _fox_prompt(q, kv, c, bsz, seq):
    tq = tk = min(seq, 512)
    nq = seq // tq
    nk = seq // tk
    cq = c.reshape(bsz, FOX_H, 1, seq)
    ck = jnp.broadcast_to(c[..., None], (bsz, FOX_H, seq, LANES))

    pairs = [(qi, ki) for qi in range(nq) for ki in range(qi + 1)]
    qi_tab = jnp.asarray([p[0] for p in pairs], jnp.int32)
    ki_tab = jnp.asarray([p[1] for p in pairs], jnp.int32)
    grid_spec = pltpu.PrefetchScalarGridSpec(
        num_scalar_prefetch=2,
        grid=(bsz, FOX_H, len(pairs)),
        in_specs=[pl.BlockSpec((tq, FOX_HD), lambda b, h, s, qt, kt: (b * nq + qt[s], h)),
                  pl.BlockSpec((tk, FOX_HD), lambda b, h, s, qt, kt: (b * nk + kt[s], h)),
                  pl.BlockSpec((tk, FOX_HD), lambda b, h, s, qt, kt: (b * nk + kt[s], FOX_H + h)),
                  pl.BlockSpec((1, 1, 1, tq), lambda b, h, s, qt, kt: (b, h, 0, qt[s])),
                  pl.BlockSpec((1, 1, tk, LANES), lambda b, h, s, qt, kt: (b, h, kt[s], 0))],
        out_specs=pl.BlockSpec((tq, FOX_HD), lambda b, h, s, qt, kt: (b * nq + qt[s], h)),
        scratch_shapes=[pltpu.VMEM((FOX_HD, tq), BF16), pltpu.VMEM((1, tq), F32), pltpu.VMEM((1, tq), F32),
                        pltpu.VMEM((FOX_HD, tq), F32)],
    )
    return pl.pallas_call(
        functools.partial(_flash_kernel, tq=tq, tk=tk, scale=FOX_HD ** -0.5),
        grid_spec=grid_spec,
        out_shape=jax.ShapeDtypeStruct((bsz * seq, MIX_W), BF16),
        compiler_params=_cparams("parallel", "parallel", "arbitrary"),
        name="fox_prompt",
    )(qi_tab, ki_tab, q, kv, kv, cq, ck)


def _fox_decode_kernel(pt_ref, q_ref, cnew_ref, knew_ref, vnew_ref, *rest, scale, page, group, nseq):
    n = nseq * group
    o_ref, m_sc, l_sc, acc_sc, carry_sc = rest[3 * n:]
    p = pl.program_id(1)

    @pl.when(p == 0)
    def _():
        m_sc[...] = jnp.full_like(m_sc, -jnp.inf)
        l_sc[...] = jnp.zeros_like(l_sc)
        acc_sc[...] = jnp.zeros_like(acc_sc)
        carry_sc[...] = jnp.zeros_like(carry_sc)

    for r in range(nseq):
        sl = slice(r * group, (r + 1) * group)
        _fox_decode_pages(q_ref.at[r], cnew_ref.at[r], rest[:n][sl], rest[n:2 * n][sl], rest[2 * n:3 * n][sl],
                          m_sc.at[r], l_sc.at[r], acc_sc.at[r], carry_sc.at[r], scale=scale, page=page, group=group)

    @pl.when(p == pl.num_programs(1) - 1)
    def _():
        for r in range(nseq):
            _fox_decode_finish(q_ref.at[r], knew_ref.at[r], vnew_ref.at[r], o_ref.at[r], m_sc.at[r], l_sc.at[r],
                               acc_sc.at[r], scale=scale)


def _fox_decode_pages(q_ref, cnew_ref, k_refs, v_refs, lf_refs, m_sc, l_sc, acc_sc, carry_sc, *, scale, page, group):
    hcol = lax.broadcasted_iota(jnp.int32, (BF16_ROWS, 1), 0)
    later = jnp.where(lax.broadcasted_iota(jnp.int32, (page, page), 0)
                      > lax.broadcasted_iota(jnp.int32, (page, page), 1), 1.0, 0.0).astype(BF16)

    q = q_ref[...]
    carry = carry_sc[...]
    cnew = cnew_ref[...]
    parts = []
    for j in range(group):
        lf = lf_refs[j][...]
        lf16 = jnp.zeros((BF16_ROWS, page), F32)
        for h in range(FOX_H):
            lf16 = jnp.where(hcol == h, lf[h:h + 1, :], lf16)
        hi, mid, lo = _split3(lf16)
        suffix = _dot(hi, later) + _dot(mid, later) + _dot(lo, later)
        s = None
        for h in range(FOX_H):
            kh = k_refs[j][pl.ds(h, page, stride=FOX_H), :]
            t = _dot_nt(q[:, h * FOX_HD:(h + 1) * FOX_HD], kh.astype(BF16))
            s = t if s is None else s + t
        parts.append(s * scale + (suffix + carry + cnew))
        carry = carry + jnp.sum(lf16, axis=-1, keepdims=True)
    carry_sc[...] = carry
    s = jnp.concatenate(parts, axis=1)

    m_prev = m_sc[...]
    m_new = jnp.maximum(m_prev, jnp.max(s, axis=-1, keepdims=True))
    alpha = jnp.exp(m_prev - m_new)
    pr = jnp.exp(s - m_new)
    l_sc[...] = alpha * l_sc[...] + jnp.sum(pr, axis=-1, keepdims=True)
    pr = pr.astype(BF16)
    for h in range(FOX_H):
        cols = slice(h * FOX_HD, (h + 1) * FOX_HD)
        upd = None
        for j in range(group):
            vh = v_refs[j][pl.ds(h, page, stride=FOX_H), :]
            t = _dot(pr[:, j * page:(j + 1) * page], vh.astype(BF16))
            upd = t if upd is None else upd + t
        acc_sc[:, cols] = alpha * acc_sc[:, cols] + upd
    m_sc[...] = m_new


def _fox_decode_finish(q_ref, knew_ref, vnew_ref, o_ref, m_sc, l_sc, acc_sc, *, scale):
    hrow = lax.broadcasted_iota(jnp.int32, (BF16_ROWS, MIX_W), 0)
    lane = lax.broadcasted_iota(jnp.int32, (BF16_ROWS, MIX_W), 1)
    kn = knew_ref[...].astype(BF16).astype(F32)
    vn = vnew_ref[...].astype(BF16).astype(F32)
    s_new = jnp.sum(q_ref[...].astype(F32) * kn, axis=-1, keepdims=True) * scale
    m_prev = m_sc[...]
    m_fin = jnp.maximum(m_prev, s_new)
    alpha = jnp.exp(m_prev - m_fin)
    p_new = jnp.exp(s_new - m_fin)
    l_fin = alpha * l_sc[...] + p_new
    acc = alpha * acc_sc[...] + p_new.astype(BF16).astype(F32) * vn
    o = acc / l_fin
    o_ref[...] = jnp.sum(jnp.where((lane >> 7) == hrow, o, 0.0), axis=0, keepdims=True)


def _fox_decode(page_table, q, k_new, v_new, c_new, cache_k, cache_v, cache_lf, l):
    bsz, n_pages = page_table.shape
    depth, n_phys, page = cache_k.shape[:3]
    group = math.gcd(n_pages, FOX_DECODE_PAGES)
    cache_k = cache_k.reshape(depth, n_phys, page * FOX_H, FOX_HD)
    cache_v = cache_v.reshape(depth, n_phys, page * FOX_H, FOX_HD)
    cache_lf = cache_lf.transpose(0, 1, 3, 2)
    hmask = (np.arange(MIX_W)[None, :] // FOX_HD) == np.arange(BF16_ROWS)[:, None]
    q_rows = jnp.where(hmask[None], q[:, None, :], 0.0).astype(BF16)
    c_rows = jnp.pad(c_new, ((0, 0), (0, BF16_ROWS - FOX_H)))[:, :, None]
    pt = page_table.reshape(-1)

    nseq = math.gcd(bsz, FOX_DECODE_SEQS)
    slots = [(r, j) for r in range(nseq) for j in range(group)]

    def pg(r, j):
        return lambda b, p, pt_ref: pt_ref[(b * nseq + r) * n_pages + (n_pages - 1 - (p * group + j))]

    def kv_spec(r, j):
        return pl.BlockSpec((None, None, page * FOX_H, FOX_HD),
                            lambda b, p, pt_ref: (l, pg(r, j)(b, p, pt_ref), 0, 0))

    def lf_spec(r, j):
        return pl.BlockSpec((None, None, FOX_H, page), lambda b, p, pt_ref: (l, pg(r, j)(b, p, pt_ref), 0, 0))

    per_seq = lambda *shape: pl.BlockSpec((nseq,) + shape, lambda b, p, pt_ref: (b,) + (0,) * len(shape))
    grid_spec = pltpu.PrefetchScalarGridSpec(
        num_scalar_prefetch=1,
        grid=(bsz // nseq, n_pages // group),
        in_specs=[per_seq(BF16_ROWS, MIX_W), per_seq(BF16_ROWS, 1), per_seq(1, MIX_W), per_seq(1, MIX_W)]
                 + [kv_spec(r, j) for r, j in slots] + [kv_spec(r, j) for r, j in slots]
                 + [lf_spec(r, j) for r, j in slots],
        out_specs=per_seq(1, MIX_W),
        scratch_shapes=[pltpu.VMEM((nseq, BF16_ROWS, 1), F32), pltpu.VMEM((nseq, BF16_ROWS, 1), F32),
                        pltpu.VMEM((nseq, BF16_ROWS, MIX_W), F32), pltpu.VMEM((nseq, BF16_ROWS, 1), F32)],
    )
    out = pl.pallas_call(
        functools.partial(_fox_decode_kernel, scale=FOX_HD ** -0.5, page=page, group=group, nseq=nseq),
        grid_spec=grid_spec,
        out_shape=jax.ShapeDtypeStruct((bsz, 1, MIX_W), F32),
        compiler_params=_cparams("parallel", "arbitrary"),
        name="fox_decode",
    )(pt, q_rows, c_rows, k_new[:, None, :], v_new[:, None, :],
      *([cache_k] * len(slots)), *([cache_v] * len(slots)), *([cache_lf] * len(slots)))
    return out.reshape(bsz, MIX_W)


def _s5_kernel(u_ref, bblk_ref, cblk_ref, are_ref, aim_ref, d_ref, wglu_ref, bglu_ref, h0re_ref, h0im_ref,
               o_ref, hre_ref, him_ref, hs_sc, *, tc, bk, slab):
    c = pl.program_id(0)

    @pl.when(c == 0)
    def _():
        hre_ref[...] = h0re_ref[...]
        him_ref[...] = h0im_ref[...]

    u = u_ref[...]
    u_bf = u.astype(BF16)
    n_u = slab // S5_P * S5_GROUP
    y_parts = []
    for si, s0 in enumerate(range(0, S5_STATE, slab)):
        re_sl = slice(s0, s0 + slab)
        im_sl = slice(S5_STATE + s0, S5_STATE + s0 + slab)
        u_sl = slice(si * n_u, (si + 1) * n_u)
        bu = _dot(u_bf[:, u_sl], bblk_ref[u_sl, :])
        hs_sc[:, re_sl] = bu[:, :slab]
        hs_sc[:, im_sl] = bu[:, slab:]
        ar = jnp.broadcast_to(are_ref[:, re_sl], (bk, slab))
        ai = jnp.broadcast_to(aim_ref[:, re_sl], (bk, slab))

        def step(t, carry, re_sl=re_sl, im_sl=im_sl, ar=ar, ai=ai):
            hr, hi = carry
            rows = pl.ds(pl.multiple_of(t * bk, bk), bk)
            nr = ar * hr - ai * hi + hs_sc[rows, re_sl]
            ni = ar * hi + ai * hr + hs_sc[rows, im_sl]
            hs_sc[rows, re_sl] = nr
            hs_sc[rows, im_sl] = ni
            return nr, ni

        hr, hi = lax.fori_loop(0, tc, step, (hre_ref[:, re_sl], him_ref[:, re_sl]))
        hre_ref[:, re_sl] = hr
        him_ref[:, re_sl] = hi
        h_bf = jnp.concatenate([hs_sc[:, re_sl], hs_sc[:, im_sl]], axis=1).astype(BF16)
        y_parts.append(_dot(h_bf, cblk_ref[si * 2 * slab:(si + 1) * 2 * slab, :]))
    y = jnp.concatenate(y_parts, axis=1) + d_ref[...] * u
    g = jax.nn.gelu(y)
    gate = jax.nn.sigmoid(_dot(g.astype(BF16), wglu_ref[...]) + bglu_ref[...])
    o_ref[...] = (g * gate).astype(o_ref.dtype)


def _s5_params(lam_re, lam_im, log_dt, b_re, b_im, c_re, c_im):
    dt = jnp.exp(log_dt)[:, None]
    mag = jnp.exp(lam_re * dt)
    ab_re, ab_im = mag * jnp.cos(lam_im * dt), mag * jnp.sin(lam_im * dt)
    den = lam_re * lam_re + lam_im * lam_im
    nr, ni = ab_re - 1.0, ab_im
    coef_re = (nr * lam_re + ni * lam_im) / den
    coef_im = (ni * lam_re - nr * lam_im) / den
    bb_re = coef_re[..., None] * b_re - coef_im[..., None] * b_im
    bb_im = coef_re[..., None] * b_im + coef_im[..., None] * b_re
    gs = S5_SLAB // S5_P
    ns = S5_G // gs
    eye = jnp.eye(gs, dtype=F32)
    blk_b = lambda t: jnp.einsum("sgph,gk->sghkp", t.reshape(ns, gs, S5_P, S5_GROUP), eye).reshape(MIX_W, S5_SLAB)
    blk_c = lambda t: jnp.einsum("sghp,gk->sgpkh", t.reshape(ns, gs, S5_GROUP, S5_P), eye).reshape(
        ns, S5_SLAB, gs * S5_GROUP)
    bblk = jnp.concatenate([blk_b(bb_re), blk_b(bb_im)], axis=1).astype(BF16)
    cblk = jnp.concatenate([blk_c(c_re), blk_c(-c_im)], axis=1).astype(BF16)
    return (ab_re.reshape(1, S5_STATE), ab_im.reshape(1, S5_STATE), bblk,
            cblk.reshape(ns * 2 * S5_SLAB, gs * S5_GROUP))


def _s5(u_tb, params, d, w_glu, b_glu, h0_re, h0_im, *, bk, steps):
    a_re, a_im, bblk, cblk = params
    tc = min(steps, 64)
    rows = tc * bk
    const = lambda shape: pl.BlockSpec(shape, lambda c: (0,) * len(shape))
    return pl.pallas_call(
        functools.partial(_s5_kernel, tc=tc, bk=bk, slab=S5_SLAB),
        grid=(steps // tc,),
        in_specs=[pl.BlockSpec((rows, MIX_W), lambda c: (c, 0)),
                  const(bblk.shape), const(cblk.shape),
                  const((1, S5_STATE)), const((1, S5_STATE)), const((1, MIX_W)),
                  const((MIX_W, MIX_W)), const((1, MIX_W)),
                  const((bk, S5_STATE)), const((bk, S5_STATE))],
        out_specs=[pl.BlockSpec((rows, MIX_W), lambda c: (c, 0)),
                   const((bk, S5_STATE)), const((bk, S5_STATE))],
        out_shape=[jax.ShapeDtypeStruct((steps * bk, MIX_W), BF16),
                   jax.ShapeDtypeStruct((bk, S5_STATE), F32), jax.ShapeDtypeStruct((bk, S5_STATE), F32)],
        scratch_shapes=[pltpu.VMEM((rows, 2 * S5_STATE), F32)],
        compiler_params=_cparams("arbitrary"),
        name="s5",
    )(u_tb, bblk, cblk, a_re, a_im, d.reshape(1, MIX_W), w_glu, b_glu.reshape(1, MIX_W), h0_re, h0_im)


def _hgrn_kernel(q_ref, f_ref, i_ref, g_ref, lb_ref, ng_ref, tri_ref, s0_ref, o_ref, st_ref, *, c, sub, valid):
    n = pl.program_id(1)

    @pl.when(n == 0)
    def _():
        st_ref[...] = s0_ref[...]

    n_seq = q_ref.shape[0]
    lb = lb_ref[...]
    la = jnp.log(lb)
    lb1 = jnp.log1p(-lb)
    decay = []
    worst = jnp.zeros((1, MIX_W), F32)
    for r in range(n_seq):
        lbb = lb1 + _log_sigmoid(f_ref[r])
        lf = jnp.maximum(la, lbb) + jnp.log1p(jnp.exp(-jnp.abs(la - lbb)))
        if valid < c:
            lf = jnp.where(lax.broadcasted_iota(jnp.int32, lf.shape, 0) < valid, lf, 0.0)
        parts = _dot(tri_ref[...], jnp.concatenate(_split3(lf), axis=1))
        bc = parts[:, :MIX_W] + parts[:, MIX_W:2 * MIX_W] + parts[:, 2 * MIX_W:]
        decay.append((1.0 - jnp.exp(lf), bc))
        mid = bc[c // 2 - 1:c // 2]
        worst = jnp.maximum(worst, jnp.maximum(-mid, mid - bc[c - 1:c]))
    safe = jnp.max(worst) < HG_SAFE_RANGE

    def run(exact):
        for r in range(n_seq):
            kt_all, bc_all = decay[r]
            for h in range(HG_H):
                cols = slice(h * HG_DK, (h + 1) * HG_DK)
                o, st = _hgrn_head(q_ref[r, :, cols], kt_all[:, cols], bc_all[:, cols], i_ref[r, :, cols],
                                   st_ref[r, h], c=c, sub=sub, exact=exact)
                st_ref[r, h] = st
                o = o * lax.rsqrt(jnp.mean(o * o, axis=-1, keepdims=True) + RMS_EPS) * ng_ref[:, cols]
                o_ref[r, :, cols] = (o * jax.nn.silu(g_ref[r, :, cols])).astype(o_ref.dtype)

    pl.when(safe)(functools.partial(run, False))
    pl.when(jnp.logical_not(safe))(functools.partial(run, True))


def _hgrn_head(q, kt, bc, v, st, *, c, sub, exact):
    v_bf = v.astype(BF16)
    o = _dot_nt((q * jnp.exp(bc)).astype(BF16), st.astype(BF16))
    if not exact:
        rel = bc - bc[c // 2 - 1:c // 2]
        att = _dot_nt((q * jnp.exp(rel)).astype(BF16), (kt * jnp.exp(-rel)).astype(BF16))
        t_pos = lax.broadcasted_iota(jnp.int32, (c, c), 0)
        s_pos = lax.broadcasted_iota(jnp.int32, (c, c), 1)
        o = o + _dot(jnp.where(s_pos <= t_pos, att, 0.0).astype(BF16), v_bf)
    else:
        nb = c // sub
        rows = lax.broadcasted_iota(jnp.int32, (sub, HG_DK), 0)
        blocks = []
        for i in range(nb):
            lo_r, hi_r = i * sub, (i + 1) * sub
            bci = bc[lo_r:hi_r]
            qi = q[lo_r:hi_r]
            oi = o[lo_r:hi_r]
            if i > 0:
                e = bc[lo_r - 1:lo_r]
                qs = (qi * jnp.exp(bci - e)).astype(BF16)
                ks = (kt[:lo_r] * jnp.exp(e - bc[:lo_r])).astype(BF16)
                oi = oi + _dot(_dot_nt(qs, ks).astype(BF16), v_bf[:lo_r])
            for s in range(sub):
                r = lo_r + s
                dec = jnp.exp(jnp.where(rows >= s, bci - bc[r:r + 1], -jnp.inf))
                a = jnp.sum(qi * dec * kt[r:r + 1], axis=-1, keepdims=True)
                oi = oi + a * v[r:r + 1]
            blocks.append(oi)
        o = jnp.concatenate(blocks, axis=0) if nb > 1 else blocks[0]
    bl = bc[c - 1:c]
    st = jnp.exp(bl) * st + _dot_tn(v_bf, (kt * jnp.exp(bl - bc)).astype(BF16))
    return o, st


def _hgrn(z, col0, lb, norm_g, s0_t, *, bsz, seq_pad, valid):
    c = min(HG_CHUNK, seq_pad)
    sub = min(HG_SUB, c)
    nc = seq_pad // c
    assert col0 % MIX_W == 0 and bsz % HG_SEQS == 0
    j0 = col0 // MIX_W
    z = z.reshape(bsz, seq_pad, z.shape[-1])
    tri = jnp.asarray(np.tril(np.ones((c, c))), BF16)
    col = lambda k: pl.BlockSpec((HG_SEQS, c, MIX_W), lambda b, n: (b, n, j0 + k))
    vec = pl.BlockSpec((1, MIX_W), lambda b, n: (0, 0))
    state = pl.BlockSpec((HG_SEQS, HG_H, HG_DV, HG_DK), lambda b, n: (b, 0, 0, 0))
    o, st = pl.pallas_call(
        functools.partial(_hgrn_kernel, c=c, sub=sub, valid=valid),
        grid=(bsz // HG_SEQS, nc),
        in_specs=[col(0), col(1), col(2), col(3), vec, vec, pl.BlockSpec((c, c), lambda b, n: (0, 0)), state],
        out_specs=[pl.BlockSpec((HG_SEQS, c, MIX_W), lambda b, n: (b, n, 0)), state],
        out_shape=[jax.ShapeDtypeStruct((bsz, seq_pad, MIX_W), BF16),
                   jax.ShapeDtypeStruct((bsz, HG_H, HG_DV, HG_DK), F32)],
        compiler_params=_cparams("parallel", "arbitrary"),
        name="hgrn2",
    )(z, z, z, z, lb.reshape(1, MIX_W), norm_g.reshape(1, MIX_W), tri, s0_t)
    return o.reshape(bsz * seq_pad, MIX_W), st


def _gmlp_kernel(u_ref, v_ref, lg_ref, lbias_ref, ws_ref, bs_ref, o_ref):
    vn = _layer_norm(jax.nn.gelu(v_ref[...]), lg_ref[...], lbias_ref[...]).astype(BF16)
    u = jax.nn.gelu(u_ref[...])
    r = lax.broadcasted_iota(jnp.int32, (GM_CHUNK, GM_CHUNK), 0)
    cc = lax.broadcasted_iota(jnp.int32, (GM_CHUNK, GM_CHUNK), 1)
    chunks = [slice(c * GM_CHUNK, (c + 1) * GM_CHUNK) for c in range(u.shape[0] // GM_CHUNK)]
    for g in range(GM_GROUPS):
        sl = slice(g * GM_GW, (g + 1) * GM_GW)
        ws = jnp.where(r >= cc, ws_ref[g], 0.0).astype(BF16)
        mixed = _dot(ws, jnp.concatenate([vn[rows, sl] for rows in chunks], axis=1))
        for c, rows in enumerate(chunks):
            m_c = mixed[:, c * GM_GW:(c + 1) * GM_GW] + bs_ref[g]
            o_ref[rows, sl] = (u[rows, sl] * m_c).astype(o_ref.dtype)


def _gmlp(z, col0, ln_g, ln_b, w_s, b_s):
    m = z.shape[0]
    assert col0 % MIX_W == 0
    j0 = col0 // MIX_W
    tm = math.gcd(m, GM_STEP_CHUNKS * GM_CHUNK)
    return pl.pallas_call(
        _gmlp_kernel,
        grid=(m // tm,),
        in_specs=[pl.BlockSpec((tm, MIX_W), lambda i: (i, j0)),
                  pl.BlockSpec((tm, MIX_W), lambda i: (i, j0 + 1)),
                  pl.BlockSpec((1, MIX_W), lambda i: (0, 0)),
                  pl.BlockSpec((1, MIX_W), lambda i: (0, 0)),
                  pl.BlockSpec((GM_GROUPS, GM_CHUNK, GM_CHUNK), lambda i: (0, 0, 0)),
                  pl.BlockSpec((GM_GROUPS, GM_CHUNK, 1), lambda i: (0, 0, 0))],
        out_specs=pl.BlockSpec((tm, MIX_W), lambda i: (i, 0)),
        out_shape=jax.ShapeDtypeStruct((m, MIX_W), BF16),
        compiler_params=_cparams("parallel"),
        name="gmlp",
    )(z, z, ln_g.reshape(1, MIX_W), ln_b.reshape(1, MIX_W), w_s, b_s[:, :, None])


def _gmlp_first_kernel(u_ref, v_ref, lg_ref, lbias_ref, w00_ref, b0_ref, o_ref, vn_ref):
    vn = _layer_norm(jax.nn.gelu(v_ref[...]), lg_ref[...], lbias_ref[...])
    vn_ref[...] = vn
    o_ref[...] = (jax.nn.gelu(u_ref[...]) * (vn * w00_ref[...] + b0_ref[...])).astype(o_ref.dtype)


def _gmlp_first(z, col0, ln_g, ln_b, w_s, b_s):
    m = z.shape[0]
    assert col0 % MIX_W == 0
    j0 = col0 // MIX_W
    w00 = jnp.repeat(w_s[:, 0, 0], GM_GW).reshape(1, MIX_W)
    b0 = jnp.repeat(b_s[:, 0], GM_GW).reshape(1, MIX_W)
    vec = pl.BlockSpec((1, MIX_W), lambda i: (0, 0))
    return pl.pallas_call(
        _gmlp_first_kernel,
        grid=(1,),
        in_specs=[pl.BlockSpec((m, MIX_W), lambda i: (0, j0)), pl.BlockSpec((m, MIX_W), lambda i: (0, j0 + 1)),
                  vec, vec, vec, vec],
        out_specs=[pl.BlockSpec((m, MIX_W), lambda i: (0, 0)), pl.BlockSpec((m, MIX_W), lambda i: (0, 0))],
        out_shape=[jax.ShapeDtypeStruct((m, MIX_W), BF16), jax.ShapeDtypeStruct((m, MIX_W), F32)],
        name="gmlp_first",
    )(z, z, ln_g.reshape(1, MIX_W), ln_b.reshape(1, MIX_W), w00, b0)


def _layer_weights(l, w_in, fox_bf, s5_lambda_re, s5_lambda_im, s5_log_dt, s5_b_re, s5_b_im, s5_c_re, s5_c_im,
                   s5_w_glu, hgrn_lb, w_branch, w_out, w_ffn_up, w_ffn_down):
    n_a = 3 * MIX_W + LANES
    o_r = 3 * MIX_W + FOX_H
    lb_all = jnp.cumsum(jax.nn.softmax(hgrn_lb.astype(F32), axis=0), axis=0)
    return dict(
        w_a=_cast_cols(w_in, l, 0, n_a),
        w_r=_cast_cols(w_in, l, o_r, w_in.shape[2] - o_r),
        bf=jnp.pad(fox_bf[l], (0, LANES - FOX_H)).reshape(1, LANES),
        s5=_s5_params(s5_lambda_re[l], s5_lambda_im[l], s5_log_dt[l], s5_b_re[l], s5_b_im[l],
                      s5_c_re[l], s5_c_im[l]),
        w_glu=s5_w_glu[l].astype(BF16),
        lb=lb_all[l] - lb_all[0],
        w_branch=w_branch, w_out=w_out, w_up=w_ffn_up, w_down=w_ffn_down,
    )


def _trunk_layer(x, l, lw, P, *, bsz, seq, fox_fn, s5_h0, hg_s0):
    m = bsz * seq
    x_bf, q_bf, kv_bf, k_rows, v_rows_fox, logf = _qkv_proj(x, lw["w_a"], lw["bf"])
    logf = logf[:, :FOX_H]
    o_a = fox_fn(q_bf, kv_bf, k_rows, v_rows_fox, logf)

    z = _proj(x_bf, lw["w_r"], n=Z_GATE)[0]
    su_tb = z[:, :MIX_W].reshape(bsz, seq, MIX_W).transpose(1, 0, 2).reshape(m, MIX_W)
    o_b_tb, s5_re, s5_im = _s5(su_tb, lw["s5"], P["s5_d"][l], lw["w_glu"], P["s5_b_glu"][l],
                               s5_h0[0].reshape(bsz, S5_STATE), s5_h0[1].reshape(bsz, S5_STATE),
                               bk=bsz, steps=seq)
    o_b = o_b_tb.reshape(seq, bsz, MIX_W).transpose(1, 0, 2).reshape(m, MIX_W)

    seq_pad = seq if seq % HG_CHUNK == 0 else -(-seq // BF16_ROWS) * BF16_ROWS
    zh, zh_col0 = z, Z_HG
    if seq_pad != seq:
        zh = z[:, Z_HG:Z_GM].reshape(bsz, seq, 4 * MIX_W)
        zh, zh_col0 = jnp.pad(zh, ((0, 0), (0, seq_pad - seq), (0, 0))).reshape(-1, 4 * MIX_W), 0
    o_c, hg_t = _hgrn(zh, zh_col0, lw["lb"], P["hgrn_norm_g"][l], hg_s0.transpose(0, 1, 3, 2),
                      bsz=bsz, seq_pad=seq_pad, valid=min(seq, HG_CHUNK))
    if seq_pad != seq:
        o_c = o_c.reshape(bsz, seq_pad, MIX_W)[:, :seq].reshape(m, MIX_W)
    hg_state = hg_t.transpose(0, 1, 3, 2)

    if seq == 1:
        o_d, v_rows = _gmlp_first(z, Z_GM, P["gmlp_ln_g"][l], P["gmlp_ln_b"][l], P["gmlp_w_s"][l], P["gmlp_b_s"][l])
    else:
        o_d = _gmlp(z, Z_GM, P["gmlp_ln_g"][l], P["gmlp_ln_b"][l], P["gmlp_w_s"][l], P["gmlp_b_s"][l])
        v_rows = None

    merged = _merge(x_bf, (o_a, o_b, o_c, o_d), lw["w_r"], Z_GATE, lw["w_branch"], l)
    x1, x1_bf = _matmul_res_ln(merged, lw["w_out"], l, x, P["ln1_g"][l], P["ln1_b"][l], out_dtypes=(F32, BF16))
    hff = _ffn_up(x1_bf, lw["w_up"], l)
    x2, = _matmul_res_ln(hff, lw["w_down"], l, x1, P["ln2_g"][l], P["ln2_b"][l], out_dtypes=(F32,))

    fk = k_rows.reshape(bsz, seq, FOX_H, FOX_HD)
    fv = v_rows_fox.reshape(bsz, seq, FOX_H, FOX_HD)
    state = (fk, fv, logf.reshape(bsz, seq, FOX_H), s5_re.reshape(bsz, S5_G, S5_P),
             s5_im.reshape(bsz, S5_G, S5_P), hg_state, v_rows)
    return x2, state


def kernel(x_prompt, x_sample, cache_k, cache_v, cache_logf, page_table, state_s5_re, state_s5_im, state_hgrn,
           w_in, fox_bf, s5_lambda_re, s5_lambda_im, s5_log_dt, s5_b_re, s5_b_im, s5_c_re, s5_c_im, s5_d,
           s5_w_glu, s5_b_glu, hgrn_lb, hgrn_norm_g, gmlp_ln_g, gmlp_ln_b, gmlp_w_s, gmlp_b_s, w_branch, w_out,
           ln1_g, ln1_b, w_ffn_up, w_ffn_down, ln2_g, ln2_b):
    b_p, l_p, _ = x_prompt.shape
    b_s, l_s, _ = x_sample.shape
    assert l_s == 1, "the sample group decodes one token per sequence"
    P = dict(s5_d=s5_d, s5_b_glu=s5_b_glu, hgrn_norm_g=hgrn_norm_g, gmlp_ln_g=gmlp_ln_g, gmlp_ln_b=gmlp_ln_b,
             gmlp_w_s=gmlp_w_s, gmlp_b_s=gmlp_b_s, ln1_g=ln1_g, ln1_b=ln1_b, ln2_g=ln2_g, ln2_b=ln2_b)

    xp = x_prompt.reshape(b_p * l_p, D_MODEL)
    xs = x_sample.reshape(b_s * l_s, D_MODEL)
    s5_zero = jnp.zeros((b_p, S5_G, S5_P), F32)
    hg_zero = jnp.zeros((b_p, HG_H, HG_DK, HG_DV), F32)
    p_states, s_states = [], []
    w_branch, w_out, w_ffn_up, w_ffn_down = (w.astype(BF16) for w in (w_branch, w_out, w_ffn_up, w_ffn_down))
    for l in range(DEPTH):
        lw = _layer_weights(l, w_in, fox_bf, s5_lambda_re, s5_lambda_im, s5_log_dt, s5_b_re, s5_b_im,
                            s5_c_re, s5_c_im, s5_w_glu, hgrn_lb, w_branch, w_out, w_ffn_up, w_ffn_down)

        def fox_p(q_bf, kv_bf, k_rows, v_rows, logf):
            c = _cumsum_lanes(logf.reshape(b_p, l_p, FOX_H).transpose(0, 2, 1).reshape(b_p * FOX_H, l_p))
            return _fox_prompt(q_bf, kv_bf, c.reshape(b_p, FOX_H, l_p), b_p, l_p)

        def fox_s(q_bf, kv_bf, k_rows, v_rows, logf, l=l):
            o = _fox_decode(page_table, q_bf, k_rows.reshape(b_s, MIX_W), v_rows.reshape(b_s, MIX_W),
                            logf, cache_k, cache_v, cache_logf, l)
            return o.astype(BF16)

        xp, sp = _trunk_layer(xp, l, lw, P, bsz=b_p, seq=l_p, fox_fn=fox_p,
                              s5_h0=(s5_zero, s5_zero), hg_s0=hg_zero)
        xs, ss = _trunk_layer(xs, l, lw, P, bsz=b_s, seq=l_s, fox_fn=fox_s,
                              s5_h0=(state_s5_re[l], state_s5_im[l]), hg_s0=state_hgrn[l])
        p_states.append(sp)
        s_states.append(ss)

    stack = lambda states, i: jnp.stack([s[i] for s in states])
    return (xp.reshape(b_p, l_p, D_MODEL), xs.reshape(b_s, l_s, D_MODEL),
            stack(p_states, 0), stack(p_states, 1), stack(p_states, 2),
            stack(s_states, 0), stack(s_states, 1), stack(s_states, 2),
            stack(p_states, 3), stack(p_states, 4), stack(s_states, 3), stack(s_states, 4),
            stack(p_states, 5), stack(s_states, 5),
            jnp.stack([s[6].reshape(b_s, l_s, MIX_W) for s in s_states]))
```

```python
import functools
import math

import numpy as np
import jax
import jax.numpy as jnp
from jax import lax
from jax.experimental import pallas as pl
from jax.experimental.pallas import tpu as pltpu

F32 = jnp.float32
BF16 = jnp.bfloat16

D_MODEL = 2048
DEPTH = 2
N_BRANCH = 4
MIX_W = D_MODEL // N_BRANCH
FOX_HD = 128
FOX_H = MIX_W // FOX_HD
S5_GROUP = 16
S5_G = MIX_W // S5_GROUP
S5_P = 64
S5_STATE = S5_G * S5_P
S5_SLAB = 512
HG_DK = 128
HG_DV = 128
HG_H = MIX_W // HG_DV
HG_CHUNK = 64
HG_SUB = 16
HG_SEQS = 8
HG_SAFE_RANGE = 44.0
GM_CHUNK = 128
GM_GROUPS = 4
GM_GW = MIX_W // GM_GROUPS
GM_STEP_CHUNKS = 8
D_FF = ((8 * D_MODEL // 3 + 255) // 256) * 256
DEEPNORM_ALPHA = (2 * DEPTH) ** 0.25
LN_EPS = 1e-5
RMS_EPS = 1e-6
LANES = 128
SUBLANES = 8
BF16_ROWS = 16
FOX_DECODE_PAGES = 32
FOX_DECODE_SEQS = 1
Z_HG = MIX_W
Z_GM = Z_HG + 4 * MIX_W
Z_GATE = Z_GM + 2 * MIX_W
VMEM_LIMIT = 48 * 1024 * 1024
VMEM_LIMIT_BIG = 58 * 1024 * 1024
PROJ_TN_MAX = 1792


def _cparams(*sem):
    return pltpu.CompilerParams(dimension_semantics=sem, vmem_limit_bytes=VMEM_LIMIT)


def _log_sigmoid(x):
    return jnp.minimum(x, 0.0) - jnp.log1p(jnp.exp(-jnp.abs(x)))


def _layer_norm(x, g, b):
    mu = jnp.mean(x, axis=-1, keepdims=True)
    xc = x - mu
    var = jnp.mean(xc * xc, axis=-1, keepdims=True)
    return xc * lax.rsqrt(var + LN_EPS) * g + b


def _split3(x):
    hi = x.astype(BF16)
    r1 = x - hi.astype(F32)
    mid = r1.astype(BF16)
    lo = (r1 - mid.astype(F32)).astype(BF16)
    return hi, mid, lo


def _dot(a, b):
    return jnp.dot(a, b, preferred_element_type=F32)


def _dot_nt(a, b):
    return lax.dot_general(a, b, (((1,), (1,)), ((), ())), preferred_element_type=F32)


def _dot_tn(a, b):
    return lax.dot_general(a, b, (((0,), (0,)), ((), ())), preferred_element_type=F32)


def _cast_cols_kernel(w_ref, o_ref, *, l, depth):
    n_kt = o_ref.shape[0] // LANES
    rows_per_col = n_kt * depth
    for cb in range(o_ref.shape[1] // LANES):
        for kt in range(n_kt):
            first = cb * LANES * rows_per_col + kt * depth + l
            tile = w_ref[pl.ds(first, LANES, stride=rows_per_col), :]
            o_ref[kt * LANES:(kt + 1) * LANES, cb * LANES:(cb + 1) * LANES] = tile.T.astype(o_ref.dtype)


def _cast_cols(w, l, col0, n):
    depth, k, n_all = w.shape
    rows_per_col = (k // LANES) * depth
    view = w.reshape(depth, k // LANES, LANES, n_all).transpose(3, 1, 0, 2).reshape(n_all * rows_per_col, LANES)
    tn = max(t for t in (LANES, 2 * LANES, 4 * LANES) if n % t == 0)
    blk = tn * rows_per_col
    return pl.pallas_call(
        functools.partial(_cast_cols_kernel, l=l, depth=depth),
        grid=(n // tn,),
        in_specs=[pl.BlockSpec((pl.Element(blk), pl.Element(LANES)),
                               lambda j: (pl.multiple_of(col0 * rows_per_col + j * blk, rows_per_col), 0))],
        out_specs=pl.BlockSpec((k, tn), lambda j: (0, j)),
        out_shape=jax.ShapeDtypeStruct((k, n), BF16),
        compiler_params=_cparams("parallel"),
        name="cast_cols",
    )(view)


def _proj_kernel(x_ref, w_ref, b_ref, *out_refs, act):
    z = _dot(x_ref[...], w_ref[...])
    if act == "gelu":
        z = jax.nn.gelu(z)
    elif act == "sigmoid":
        z = jax.nn.sigmoid(z)
    elif act == "log_sigmoid_bias":
        z = _log_sigmoid(z + b_ref[...])
    for o in out_refs:
        o[...] = z.astype(o.dtype)


def _proj(x, w, *, col0=0, n=None, act=None, bias=None, out_dtypes=(F32,)):
    m, k = x.shape
    n = w.shape[1] if n is None else n
    tm = min(m, 512)
    tn = max(t for t in range(LANES, min(n, PROJ_TN_MAX) + 1, LANES) if n % t == 0 and col0 % t == 0)
    j0 = col0 // tn
    if bias is None:
        bias = jnp.zeros((1, n), F32)
    return pl.pallas_call(
        functools.partial(_proj_kernel, act=act),
        grid=(n // tn, m // tm),
        in_specs=[pl.BlockSpec((tm, k), lambda j, i: (i, 0)),
                  pl.BlockSpec((k, tn), lambda j, i: (0, j0 + j)),
                  pl.BlockSpec((1, tn), lambda j, i: (0, j))],
        out_specs=[pl.BlockSpec((tm, tn), lambda j, i: (i, j)) for _ in out_dtypes],
        out_shape=[jax.ShapeDtypeStruct((m, n), d) for d in out_dtypes],
        compiler_params=_cparams("parallel", "arbitrary"),
        name="proj_" + (act or "id"),
    )(x, w, bias)


def _qkv_kernel(x_ref, w_ref, b_ref, xbf_ref, q_ref, kv_ref, k_ref, v_ref, lf_ref):
    x = x_ref[...].astype(BF16)
    xbf_ref[...] = x
    z = _dot(x, w_ref[...])
    tm = z.shape[0]
    q_ref[...] = z[:, :MIX_W].astype(q_ref.dtype)
    kv_ref[...] = z[:, MIX_W:3 * MIX_W].astype(kv_ref.dtype)
    for h in range(FOX_H):
        k_ref[pl.ds(h, tm, stride=FOX_H), :] = z[:, MIX_W + h * FOX_HD:MIX_W + (h + 1) * FOX_HD]
        v_ref[pl.ds(h, tm, stride=FOX_H), :] = z[:, 2 * MIX_W + h * FOX_HD:2 * MIX_W + (h + 1) * FOX_HD]
    lf_ref[...] = _log_sigmoid(z[:, 3 * MIX_W:] + b_ref[...])


def _qkv_proj(x, w, bias):
    m, k = x.shape
    tm = min(m, 512)
    n = w.shape[1]
    row = lambda width: pl.BlockSpec((tm, width), lambda i: (i, 0))
    rows = pl.BlockSpec((tm * FOX_H, FOX_HD), lambda i: (i, 0))
    return pl.pallas_call(
        _qkv_kernel,
        grid=(m // tm,),
        in_specs=[row(k), pl.BlockSpec((k, n), lambda i: (0, 0), pipeline_mode=pl.Buffered(1)),
                  pl.BlockSpec((1, LANES), lambda i: (0, 0))],
        out_specs=[row(k), row(MIX_W), row(2 * MIX_W), rows, rows, row(LANES)],
        out_shape=[jax.ShapeDtypeStruct((m, k), BF16), jax.ShapeDtypeStruct((m, MIX_W), BF16),
                   jax.ShapeDtypeStruct((m, 2 * MIX_W), BF16),
                   jax.ShapeDtypeStruct((m * FOX_H, FOX_HD), F32), jax.ShapeDtypeStruct((m * FOX_H, FOX_HD), F32),
                   jax.ShapeDtypeStruct((m, LANES), F32)],
        compiler_params=_cparams("parallel"),
        name="qkv_proj",
    )(x, w, bias)


def _merge_kernel(x_ref, oa_ref, ob_ref, oc_ref, od_ref, g0_ref, g1_ref, g2_ref, g3_ref, wb_ref, o_ref):
    x = x_ref[...]
    acc = None
    for br, gr, k in ((oa_ref, g0_ref, 0), (ob_ref, g1_ref, 1), (oc_ref, g2_ref, 2), (od_ref, g3_ref, 3)):
        t = jax.nn.sigmoid(_dot(x, gr[...])) * _dot(br[...], wb_ref[k])
        acc = t if acc is None else acc + t
    o_ref[...] = acc.astype(o_ref.dtype)


def _merge(x, branches, w, gate_col0, w_branch, l):
    m, k_in = x.shape
    tm = min(m, 512)
    tn = 512
    nb = D_MODEL // tn
    assert gate_col0 % tn == 0
    j0 = gate_col0 // tn
    gate_specs = [pl.BlockSpec((k_in, tn), functools.partial(lambda j, i, k: (0, j0 + k * nb + j), k=k))
                  for k in range(N_BRANCH)]
    return pl.pallas_call(
        _merge_kernel,
        grid=(nb, m // tm),
        in_specs=[pl.BlockSpec((tm, k_in), lambda j, i: (i, 0))]
                 + [pl.BlockSpec((tm, MIX_W), lambda j, i: (i, 0)) for _ in range(N_BRANCH)] + gate_specs
                 + [pl.BlockSpec((None, N_BRANCH, MIX_W, tn), lambda j, i: (l, 0, 0, j))],
        out_specs=pl.BlockSpec((tm, tn), lambda j, i: (i, j)),
        out_shape=jax.ShapeDtypeStruct((m, D_MODEL), BF16),
        compiler_params=_cparams("parallel", "arbitrary"),
        name="merge",
    )(x, *branches, w, w, w, w, w_branch)


def _res_ln_kernel(a_ref, w_ref, x_ref, g_ref, b_ref, *y_refs):
    y = _layer_norm(DEEPNORM_ALPHA * x_ref[...] + _dot(a_ref[...], w_ref[...]), g_ref[...], b_ref[...])
    for y_ref in y_refs:
        y_ref[...] = y.astype(y_ref.dtype)


def _matmul_res_ln(a, w, l, x, g, b, *, out_dtypes):
    m, k = a.shape
    n = w.shape[2]
    tm = min(m, 512 if k <= D_MODEL else 256)
    return pl.pallas_call(
        _res_ln_kernel,
        grid=(m // tm,),
        in_specs=[pl.BlockSpec((tm, k), lambda i: (i, 0)),
                  pl.BlockSpec((None, k, n), lambda i: (l, 0, 0), pipeline_mode=pl.Buffered(1)),
                  pl.BlockSpec((tm, n), lambda i: (i, 0)),
                  pl.BlockSpec((1, n), lambda i: (0, 0)),
                  pl.BlockSpec((1, n), lambda i: (0, 0))],
        out_specs=[pl.BlockSpec((tm, n), lambda i: (i, 0)) for _ in out_dtypes],
        out_shape=[jax.ShapeDtypeStruct((m, n), d) for d in out_dtypes],
        compiler_params=pltpu.CompilerParams(dimension_semantics=("parallel",), vmem_limit_bytes=VMEM_LIMIT_BIG),
        name="matmul_res_ln",
    )(a, w, x, g.reshape(1, n), b.reshape(1, n))


def _ffn_up_kernel(x_ref, wg_ref, wu_ref, o_ref, wg_sc, wu_sc):
    @pl.when(pl.program_id(1) == 0)
    def _():
        wg_sc[...] = wg_ref[...].astype(BF16)
        wu_sc[...] = wu_ref[...].astype(BF16)

    x = x_ref[...]
    o_ref[...] = (jax.nn.silu(_dot(x, wg_sc[...])) * _dot(x, wu_sc[...])).astype(o_ref.dtype)


def _ffn_up(x, w_up, l):
    m, k = x.shape
    tm = min(m, 1024)
    tn = 512
    nb = D_FF // tn
    return pl.pallas_call(
        _ffn_up_kernel,
        grid=(nb, m // tm),
        in_specs=[pl.BlockSpec((tm, k), lambda j, i: (i, 0)),
                  pl.BlockSpec((None, k, tn), lambda j, i: (l, 0, j)),
                  pl.BlockSpec((None, k, tn), lambda j, i: (l, 0, nb + j))],
        out_specs=pl.BlockSpec((tm, tn), lambda j, i: (i, j)),
        out_shape=jax.ShapeDtypeStruct((m, D_FF), BF16),
        scratch_shapes=[pltpu.VMEM((k, tn), BF16), pltpu.VMEM((k, tn), BF16)],
        compiler_params=_cparams("parallel", "arbitrary"),
        name="ffn_up",
    )(x, w_up, w_up)


def _cumsum_lanes_kernel(x_ref, o_ref):
    x = x_ref[...]
    n = x.shape[-1]
    lane = lax.broadcasted_iota(jnp.int32, x.shape, 1)
    sh = 1
    while sh < n:
        x = x + jnp.where(lane >= sh, pltpu.roll(x, sh, axis=1), 0.0)
        sh *= 2
    o_ref[...] = x


def _cumsum_lanes(x):
    return pl.pallas_call(_cumsum_lanes_kernel, out_shape=jax.ShapeDtypeStruct(x.shape, F32),
                          name="cumsum_lanes")(x)


def _flash_kernel(qi_ref, ki_ref, q_ref, k_ref, v_ref, cq_ref, ck_ref, o_ref, qt_sc, m_sc, l_sc, acc_sc, *,
                  tq, tk, scale):
    qi = qi_ref[pl.program_id(2)]
    ki = ki_ref[pl.program_id(2)]

    @pl.when(ki == 0)
    def _():
        qt_sc[...] = q_ref[...].astype(F32).T.astype(BF16)
        m_sc[...] = jnp.full_like(m_sc, -jnp.inf)
        l_sc[...] = jnp.zeros_like(l_sc)
        acc_sc[...] = jnp.zeros_like(acc_sc)

    def update(masked):
        s = _dot(k_ref[...], qt_sc[...]) * scale
        s = s + cq_ref[0, 0] - jnp.concatenate([ck_ref[0, 0]] * (tq // LANES), axis=1)
        if masked:
            k_pos = lax.broadcasted_iota(jnp.int32, (tk, tq), 0)
            q_pos = lax.broadcasted_iota(jnp.int32, (tk, tq), 1)
            s = jnp.where(k_pos <= q_pos, s, -jnp.inf)
        m_prev = m_sc[...]
        m_new = jnp.maximum(m_prev, jnp.max(s, axis=0, keepdims=True))
        alpha = jnp.exp(m_prev - m_new)
        p = jnp.exp(s - m_new)
        l_sc[...] = alpha * l_sc[...] + jnp.sum(p, axis=0, keepdims=True)
        acc_sc[...] = alpha * acc_sc[...] + _dot_tn(v_ref[...], p.astype(BF16))
        m_sc[...] = m_new

    pl.when(ki < qi)(functools.partial(update, False))

    @pl.when(ki == qi)
    def _():
        update(True)
        o_ref[...] = (acc_sc[...] / l_sc[...]).T.astype(o_ref.dtype)


def _fox_prompt(q, kv, c, bsz, seq):
    tq = tk = min(seq, 512)
    nq = seq // tq
    nk = seq // tk
    cq = c.reshape(bsz, FOX_H, 1, seq)
    ck = jnp.broadcast_to(c[..., None], (bsz, FOX_H, seq, LANES))

    pairs = [(qi, ki) for qi in range(nq) for ki in range(qi + 1)]
    qi_tab = jnp.asarray([p[0] for p in pairs], jnp.int32)
    ki_tab = jnp.asarray([p[1] for p in pairs], jnp.int32)
    grid_spec = pltpu.PrefetchScalarGridSpec(
        num_scalar_prefetch=2,
        grid=(bsz, FOX_H, len(pairs)),
        in_specs=[pl.BlockSpec((tq, FOX_HD), lambda b, h, s, qt, kt: (b * nq + qt[s], h)),
                  pl.BlockSpec((tk, FOX_HD), lambda b, h, s, qt, kt: (b * nk + kt[s], h)),
                  pl.BlockSpec((tk, FOX_HD), lambda b, h, s, qt, kt: (b * nk + kt[s], FOX_H + h)),
                  pl.BlockSpec((1, 1, 1, tq), lambda b, h, s, qt, kt: (b, h, 0, qt[s])),
                  pl.BlockSpec((1, 1, tk, LANES), lambda b, h, s, qt, kt: (b, h, kt[s], 0))],
        out_specs=pl.BlockSpec((tq, FOX_HD), lambda b, h, s, qt, kt: (b * nq + qt[s], h)),
        scratch_shapes=[pltpu.VMEM((FOX_HD, tq), BF16), pltpu.VMEM((1, tq), F32), pltpu.VMEM((1, tq), F32),
                        pltpu.VMEM((FOX_HD, tq), F32)],
    )
    return pl.pallas_call(
        functools.partial(_flash_kernel, tq=tq, tk=tk, scale=FOX_HD ** -0.5),
        grid_spec=grid_spec,
        out_shape=jax.ShapeDtypeStruct((bsz * seq, MIX_W), BF16),
        compiler_params=_cparams("parallel", "parallel", "arbitrary"),
        name="fox_prompt",
    )(qi_tab, ki_tab, q, kv, kv, cq, ck)


def _fox_decode_kernel(pt_ref, q_ref, cnew_ref, knew_ref, vnew_ref, lf_pool_ref, *rest, scale, page, group, nseq,
                       n_pages):
    n = nseq * group
    o_ref, m_sc, l_sc, acc_sc, carry_sc = rest[2 * n:]
    b = pl.program_id(0)
    p = pl.program_id(1)

    @pl.when(p == 0)
    def _():
        m_sc[...] = jnp.full_like(m_sc, -jnp.inf)
        l_sc[...] = jnp.zeros_like(l_sc)
        acc_sc[...] = jnp.zeros_like(acc_sc)
        carry_sc[...] = jnp.zeros_like(carry_sc)

    for r in range(nseq):
        sl = slice(r * group, (r + 1) * group)
        newest = (b * nseq + r) * n_pages + (n_pages - 1) - p * group
        lf_views = [lf_pool_ref.at[pt_ref[newest - j]] for j in range(group)]
        _fox_decode_pages(q_ref.at[r], cnew_ref.at[r], rest[:n][sl], rest[n:2 * n][sl], lf_views,
                          m_sc.at[r], l_sc.at[r], acc_sc.at[r], carry_sc.at[r], scale=scale, page=page, group=group)

    @pl.when(p == pl.num_programs(1) - 1)
    def _():
        for r in range(nseq):
            _fox_decode_finish(q_ref.at[r], knew_ref.at[r], vnew_ref.at[r], o_ref.at[r], m_sc.at[r], l_sc.at[r],
                               acc_sc.at[r], scale=scale)


def _fox_decode_pages(q_ref, cnew_ref, k_refs, v_refs, lf_refs, m_sc, l_sc, acc_sc, carry_sc, *, scale, page, group):
    hcol = lax.broadcasted_iota(jnp.int32, (BF16_ROWS, 1), 0)
    later = jnp.where(lax.broadcasted_iota(jnp.int32, (page, page), 0)
                      > lax.broadcasted_iota(jnp.int32, (page, page), 1), 1.0, 0.0).astype(BF16)

    q = q_ref[...]
    carry = carry_sc[...]
    cnew = cnew_ref[...]
    parts = []
    for j in range(group):
        lf = lf_refs[j][...]
        lf16 = jnp.zeros((BF16_ROWS, page), F32)
        for h in range(FOX_H):
            lf16 = jnp.where(hcol == h, lf[h:h + 1, :], lf16)
        hi, mid, lo = _split3(lf16)
        suffix = _dot(hi, later) + _dot(mid, later) + _dot(lo, later)
        s = None
        for h in range(FOX_H):
            kh = k_refs[j][pl.ds(h, page, stride=FOX_H), :]
            t = _dot_nt(q[:, h * FOX_HD:(h + 1) * FOX_HD], kh.astype(BF16))
            s = t if s is None else s + t
        parts.append(s * scale + (suffix + carry + cnew))
        carry = carry + jnp.sum(lf16, axis=-1, keepdims=True)
    carry_sc[...] = carry
    s = jnp.concatenate(parts, axis=1)

    m_prev = m_sc[...]
    m_new = jnp.maximum(m_prev, jnp.max(s, axis=-1, keepdims=True))
    alpha = jnp.exp(m_prev - m_new)
    pr = jnp.exp(s - m_new)
    l_sc[...] = alpha * l_sc[...] + jnp.sum(pr, axis=-1, keepdims=True)
    pr = pr.astype(BF16)
    for h in range(FOX_H):
        cols = slice(h * FOX_HD, (h + 1) * FOX_HD)
        upd = None
        for j in range(group):
            vh = v_refs[j][pl.ds(h, page, stride=FOX_H), :]
            t = _dot(pr[:, j * page:(j + 1) * page], vh.astype(BF16))
            upd = t if upd is None else upd + t
        acc_sc[:, cols] = alpha * acc_sc[:, cols] + upd
    m_sc[...] = m_new


def _fox_decode_finish(q_ref, knew_ref, vnew_ref, o_ref, m_sc, l_sc, acc_sc, *, scale):
    hrow = lax.broadcasted_iota(jnp.int32, (BF16_ROWS, MIX_W), 0)
    lane = lax.broadcasted_iota(jnp.int32, (BF16_ROWS, MIX_W), 1)
    kn = knew_ref[...].astype(BF16).astype(F32)
    vn = vnew_ref[...].astype(BF16).astype(F32)
    s_new = jnp.sum(q_ref[...].astype(F32) * kn, axis=-1, keepdims=True) * scale
    m_prev = m_sc[...]
    m_fin = jnp.maximum(m_prev, s_new)
    alpha = jnp.exp(m_prev - m_fin)
    p_new = jnp.exp(s_new - m_fin)
    l_fin = alpha * l_sc[...] + p_new
    acc = alpha * acc_sc[...] + p_new.astype(BF16).astype(F32) * vn
    o = acc / l_fin
    o_ref[...] = jnp.sum(jnp.where((lane >> 7) == hrow, o, 0.0), axis=0, keepdims=True)


def _fox_decode(page_table, q, k_new, v_new, c_new, cache_k, cache_v, cache_lf, l):
    bsz, n_pages = page_table.shape
    depth, n_phys, page = cache_k.shape[:3]
    group = math.gcd(n_pages, FOX_DECODE_PAGES)
    cache_k = cache_k.reshape(depth, n_phys, page * FOX_H, FOX_HD)
    cache_v = cache_v.reshape(depth, n_phys, page * FOX_H, FOX_HD)
    cache_lf = cache_lf.transpose(0, 1, 3, 2)
    hmask = (np.arange(MIX_W)[None, :] // FOX_HD) == np.arange(BF16_ROWS)[:, None]
    q_rows = jnp.where(hmask[None], q[:, None, :], 0.0).astype(BF16)
    c_rows = jnp.pad(c_new, ((0, 0), (0, BF16_ROWS - FOX_H)))[:, :, None]
    pt = page_table.reshape(-1)

    nseq = math.gcd(bsz, FOX_DECODE_SEQS)
    slots = [(r, j) for r in range(nseq) for j in range(group)]

    def pg(r, j):
        return lambda b, p, pt_ref: pt_ref[(b * nseq + r) * n_pages + (n_pages - 1 - (p * group + j))]

    def kv_spec(r, j):
        return pl.BlockSpec((None, None, page * FOX_H, FOX_HD),
                            lambda b, p, pt_ref: (l, pg(r, j)(b, p, pt_ref), 0, 0))

    lf_pool = pl.BlockSpec((None, n_phys, FOX_H, page), lambda b, p, pt_ref: (l, 0, 0, 0),
                           pipeline_mode=pl.Buffered(1))
    per_seq = lambda *shape: pl.BlockSpec((nseq,) + shape, lambda b, p, pt_ref: (b,) + (0,) * len(shape))
    grid_spec = pltpu.PrefetchScalarGridSpec(
        num_scalar_prefetch=1,
        grid=(bsz // nseq, n_pages // group),
        in_specs=[per_seq(BF16_ROWS, MIX_W), per_seq(BF16_ROWS, 1), per_seq(1, MIX_W), per_seq(1, MIX_W), lf_pool]
                 + [kv_spec(r, j) for r, j in slots] + [kv_spec(r, j) for r, j in slots],
        out_specs=per_seq(1, MIX_W),
        scratch_shapes=[pltpu.VMEM((nseq, BF16_ROWS, 1), F32), pltpu.VMEM((nseq, BF16_ROWS, 1), F32),
                        pltpu.VMEM((nseq, BF16_ROWS, MIX_W), F32), pltpu.VMEM((nseq, BF16_ROWS, 1), F32)],
    )
    out = pl.pallas_call(
        functools.partial(_fox_decode_kernel, scale=FOX_HD ** -0.5, page=page, group=group, nseq=nseq,
                          n_pages=n_pages),
        grid_spec=grid_spec,
        out_shape=jax.ShapeDtypeStruct((bsz, 1, MIX_W), F32),
        compiler_params=pltpu.CompilerParams(dimension_semantics=("parallel", "arbitrary"),
                                             vmem_limit_bytes=VMEM_LIMIT_BIG),
        name="fox_decode",
    )(pt, q_rows, c_rows, k_new[:, None, :], v_new[:, None, :], cache_lf,
      *([cache_k] * len(slots)), *([cache_v] * len(slots)))
    return out.reshape(bsz, MIX_W)


def _s5_kernel(u_ref, bblk_ref, cblk_ref, are_ref, aim_ref, d_ref, wglu_ref, bglu_ref, h0re_ref, h0im_ref,
               o_ref, hre_ref, him_ref, hs_sc, *, tc, bk, slab):
    c = pl.program_id(0)

    @pl.when(c == 0)
    def _():
        hre_ref[...] = h0re_ref[...]
        him_ref[...] = h0im_ref[...]

    u = u_ref[...]
    u_bf = u.astype(BF16)
    n_u = slab // S5_P * S5_GROUP
    y_parts = []
    for si, s0 in enumerate(range(0, S5_STATE, slab)):
        re_sl = slice(s0, s0 + slab)
        im_sl = slice(S5_STATE + s0, S5_STATE + s0 + slab)
        u_sl = slice(si * n_u, (si + 1) * n_u)
        bu = _dot(u_bf[:, u_sl], bblk_ref[u_sl, :])
        hs_sc[:, re_sl] = bu[:, :slab]
        hs_sc[:, im_sl] = bu[:, slab:]
        ar = jnp.broadcast_to(are_ref[:, re_sl], (bk, slab))
        ai = jnp.broadcast_to(aim_ref[:, re_sl], (bk, slab))

        def step(t, carry, re_sl=re_sl, im_sl=im_sl, ar=ar, ai=ai):
            hr, hi = carry
            rows = pl.ds(pl.multiple_of(t * bk, bk), bk)
            nr = ar * hr - ai * hi + hs_sc[rows, re_sl]
            ni = ar * hi + ai * hr + hs_sc[rows, im_sl]
            hs_sc[rows, re_sl] = nr
            hs_sc[rows, im_sl] = ni
            return nr, ni

        hr, hi = lax.fori_loop(0, tc, step, (hre_ref[:, re_sl], him_ref[:, re_sl]))
        hre_ref[:, re_sl] = hr
        him_ref[:, re_sl] = hi
        h_bf = jnp.concatenate([hs_sc[:, re_sl], hs_sc[:, im_sl]], axis=1).astype(BF16)
        y_parts.append(_dot(h_bf, cblk_ref[si * 2 * slab:(si + 1) * 2 * slab, :]))
    y = jnp.concatenate(y_parts, axis=1) + d_ref[...] * u
    g = jax.nn.gelu(y)
    gate = jax.nn.sigmoid(_dot(g.astype(BF16), wglu_ref[...]) + bglu_ref[...])
    o_ref[...] = (g * gate).astype(o_ref.dtype)


def _s5_params(lam_re, lam_im, log_dt, b_re, b_im, c_re, c_im):
    dt = jnp.exp(log_dt)[:, None]
    mag = jnp.exp(lam_re * dt)
    ab_re, ab_im = mag * jnp.cos(lam_im * dt), mag * jnp.sin(lam_im * dt)
    den = lam_re * lam_re + lam_im * lam_im
    nr, ni = ab_re - 1.0, ab_im
    coef_re = (nr * lam_re + ni * lam_im) / den
    coef_im = (ni * lam_re - nr * lam_im) / den
    bb_re = coef_re[..., None] * b_re - coef_im[..., None] * b_im
    bb_im = coef_re[..., None] * b_im + coef_im[..., None] * b_re
    gs = S5_SLAB // S5_P
    ns = S5_G // gs
    eye = jnp.eye(gs, dtype=F32)
    blk_b = lambda t: jnp.einsum("sgph,gk->sghkp", t.reshape(ns, gs, S5_P, S5_GROUP), eye).reshape(MIX_W, S5_SLAB)
    blk_c = lambda t: jnp.einsum("sghp,gk->sgpkh", t.reshape(ns, gs, S5_GROUP, S5_P), eye).reshape(
        ns, S5_SLAB, gs * S5_GROUP)
    bblk = jnp.concatenate([blk_b(bb_re), blk_b(bb_im)], axis=1).astype(BF16)
    cblk = jnp.concatenate([blk_c(c_re), blk_c(-c_im)], axis=1).astype(BF16)
    return (ab_re.reshape(1, S5_STATE), ab_im.reshape(1, S5_STATE), bblk,
            cblk.reshape(ns * 2 * S5_SLAB, gs * S5_GROUP))


def _s5(u_tb, params, d, w_glu, b_glu, h0_re, h0_im, *, bk, steps):
    a_re, a_im, bblk, cblk = params
    tc = min(steps, 64)
    rows = tc * bk
    const = lambda shape: pl.BlockSpec(shape, lambda c: (0,) * len(shape))
    return pl.pallas_call(
        functools.partial(_s5_kernel, tc=tc, bk=bk, slab=S5_SLAB),
        grid=(steps // tc,),
        in_specs=[pl.BlockSpec((rows, MIX_W), lambda c: (c, 0)),
                  const(bblk.shape), const(cblk.shape),
                  const((1, S5_STATE)), const((1, S5_STATE)), const((1, MIX_W)),
                  const((MIX_W, MIX_W)), const((1, MIX_W)),
                  const((bk, S5_STATE)), const((bk, S5_STATE))],
        out_specs=[pl.BlockSpec((rows, MIX_W), lambda c: (c, 0)),
                   const((bk, S5_STATE)), const((bk, S5_STATE))],
        out_shape=[jax.ShapeDtypeStruct((steps * bk, MIX_W), BF16),
                   jax.ShapeDtypeStruct((bk, S5_STATE), F32), jax.ShapeDtypeStruct((bk, S5_STATE), F32)],
        scratch_shapes=[pltpu.VMEM((rows, 2 * S5_STATE), F32)],
        compiler_params=_cparams("arbitrary"),
        name="s5",
    )(u_tb, bblk, cblk, a_re, a_im, d.reshape(1, MIX_W), w_glu, b_glu.reshape(1, MIX_W), h0_re, h0_im)


def _hgrn_kernel(q_ref, f_ref, i_ref, g_ref, lb_ref, ng_ref, tri_ref, s0_ref, o_ref, st_ref, *, c, sub, valid):
    n = pl.program_id(1)

    @pl.when(n == 0)
    def _():
        st_ref[...] = s0_ref[...]

    n_seq = q_ref.shape[0]
    lb = lb_ref[...]
    la = jnp.log(lb)
    lb1 = jnp.log1p(-lb)
    decay = []
    worst = jnp.zeros((1, MIX_W), F32)
    for r in range(n_seq):
        lbb = lb1 + _log_sigmoid(f_ref[r])
        lf = jnp.maximum(la, lbb) + jnp.log1p(jnp.exp(-jnp.abs(la - lbb)))
        if valid < c:
            lf = jnp.where(lax.broadcasted_iota(jnp.int32, lf.shape, 0) < valid, lf, 0.0)
        parts = _dot(tri_ref[...], jnp.concatenate(_split3(lf), axis=1))
        bc = parts[:, :MIX_W] + parts[:, MIX_W:2 * MIX_W] + parts[:, 2 * MIX_W:]
        decay.append((1.0 - jnp.exp(lf), bc))
        mid = bc[c // 2 - 1:c // 2]
        worst = jnp.maximum(worst, jnp.maximum(-mid, mid - bc[c - 1:c]))
    safe = jnp.max(worst) < HG_SAFE_RANGE

    def run(exact):
        for r in range(n_seq):
            kt_all, bc_all = decay[r]
            for h in range(HG_H):
                cols = slice(h * HG_DK, (h + 1) * HG_DK)
                o, st = _hgrn_head(q_ref[r, :, cols], kt_all[:, cols], bc_all[:, cols], i_ref[r, :, cols],
                                   st_ref[r, h], c=c, sub=sub, exact=exact)
                st_ref[r, h] = st
                o = o * lax.rsqrt(jnp.mean(o * o, axis=-1, keepdims=True) + RMS_EPS) * ng_ref[:, cols]
                o_ref[r, :, cols] = (o * jax.nn.silu(g_ref[r, :, cols])).astype(o_ref.dtype)

    pl.when(safe)(functools.partial(run, False))
    pl.when(jnp.logical_not(safe))(functools.partial(run, True))


def _hgrn_head(q, kt, bc, v, st, *, c, sub, exact):
    v_bf = v.astype(BF16)
    o = _dot_nt((q * jnp.exp(bc)).astype(BF16), st.astype(BF16))
    if not exact:
        rel = bc - bc[c // 2 - 1:c // 2]
        att = _dot_nt((q * jnp.exp(rel)).astype(BF16), (kt * jnp.exp(-rel)).astype(BF16))
        t_pos = lax.broadcasted_iota(jnp.int32, (c, c), 0)
        s_pos = lax.broadcasted_iota(jnp.int32, (c, c), 1)
        o = o + _dot(jnp.where(s_pos <= t_pos, att, 0.0).astype(BF16), v_bf)
    else:
        nb = c // sub
        rows = lax.broadcasted_iota(jnp.int32, (sub, HG_DK), 0)
        blocks = []
        for i in range(nb):
            lo_r, hi_r = i * sub, (i + 1) * sub
            bci = bc[lo_r:hi_r]
            qi = q[lo_r:hi_r]
            oi = o[lo_r:hi_r]
            if i > 0:
                e = bc[lo_r - 1:lo_r]
                qs = (qi * jnp.exp(bci - e)).astype(BF16)
                ks = (kt[:lo_r] * jnp.exp(e - bc[:lo_r])).astype(BF16)
                oi = oi + _dot(_dot_nt(qs, ks).astype(BF16), v_bf[:lo_r])
            for s in range(sub):
                r = lo_r + s
                dec = jnp.exp(jnp.where(rows >= s, bci - bc[r:r + 1], -jnp.inf))
                a = jnp.sum(qi * dec * kt[r:r + 1], axis=-1, keepdims=True)
                oi = oi + a * v[r:r + 1]
            blocks.append(oi)
        o = jnp.concatenate(blocks, axis=0) if nb > 1 else blocks[0]
    bl = bc[c - 1:c]
    st = jnp.exp(bl) * st + _dot_tn(v_bf, (kt * jnp.exp(bl - bc)).astype(BF16))
    return o, st


def _hgrn(z, col0, lb, norm_g, s0_t, *, bsz, seq_pad, valid):
    c = min(HG_CHUNK, seq_pad)
    sub = min(HG_SUB, c)
    nc = seq_pad // c
    assert col0 % MIX_W == 0 and bsz % HG_SEQS == 0
    j0 = col0 // MIX_W
    z = z.reshape(bsz, seq_pad, z.shape[-1])
    tri = jnp.asarray(np.tril(np.ones((c, c))), BF16)
    col = lambda k: pl.BlockSpec((HG_SEQS, c, MIX_W), lambda b, n: (b, n, j0 + k))
    vec = pl.BlockSpec((1, MIX_W), lambda b, n: (0, 0))
    state = pl.BlockSpec((HG_SEQS, HG_H, HG_DV, HG_DK), lambda b, n: (b, 0, 0, 0))
    o, st = pl.pallas_call(
        functools.partial(_hgrn_kernel, c=c, sub=sub, valid=valid),
        grid=(bsz // HG_SEQS, nc),
        in_specs=[col(0), col(1), col(2), col(3), vec, vec, pl.BlockSpec((c, c), lambda b, n: (0, 0)), state],
        out_specs=[pl.BlockSpec((HG_SEQS, c, MIX_W), lambda b, n: (b, n, 0)), state],
        out_shape=[jax.ShapeDtypeStruct((bsz, seq_pad, MIX_W), BF16),
                   jax.ShapeDtypeStruct((bsz, HG_H, HG_DV, HG_DK), F32)],
        compiler_params=_cparams("parallel", "arbitrary"),
        name="hgrn2",
    )(z, z, z, z, lb.reshape(1, MIX_W), norm_g.reshape(1, MIX_W), tri, s0_t)
    return o.reshape(bsz * seq_pad, MIX_W), st


def _gmlp_kernel(u_ref, v_ref, lg_ref, lbias_ref, ws_ref, bs_ref, o_ref):
    vn = _layer_norm(jax.nn.gelu(v_ref[...]), lg_ref[...], lbias_ref[...]).astype(BF16)
    u = jax.nn.gelu(u_ref[...])
    r = lax.broadcasted_iota(jnp.int32, (GM_CHUNK, GM_CHUNK), 0)
    cc = lax.broadcasted_iota(jnp.int32, (GM_CHUNK, GM_CHUNK), 1)
    chunks = [slice(c * GM_CHUNK, (c + 1) * GM_CHUNK) for c in range(u.shape[0] // GM_CHUNK)]
    for g in range(GM_GROUPS):
        sl = slice(g * GM_GW, (g + 1) * GM_GW)
        ws = jnp.where(r >= cc, ws_ref[g], 0.0).astype(BF16)
        mixed = _dot(ws, jnp.concatenate([vn[rows, sl] for rows in chunks], axis=1))
        for c, rows in enumerate(chunks):
            m_c = mixed[:, c * GM_GW:(c + 1) * GM_GW] + bs_ref[g]
            o_ref[rows, sl] = (u[rows, sl] * m_c).astype(o_ref.dtype)


def _gmlp(z, col0, ln_g, ln_b, w_s, b_s):
    m = z.shape[0]
    assert col0 % MIX_W == 0
    j0 = col0 // MIX_W
    tm = math.gcd(m, GM_STEP_CHUNKS * GM_CHUNK)
    return pl.pallas_call(
        _gmlp_kernel,
        grid=(m // tm,),
        in_specs=[pl.BlockSpec((tm, MIX_W), lambda i: (i, j0)),
                  pl.BlockSpec((tm, MIX_W), lambda i: (i, j0 + 1)),
                  pl.BlockSpec((1, MIX_W), lambda i: (0, 0)),
                  pl.BlockSpec((1, MIX_W), lambda i: (0, 0)),
                  pl.BlockSpec((GM_GROUPS, GM_CHUNK, GM_CHUNK), lambda i: (0, 0, 0)),
                  pl.BlockSpec((GM_GROUPS, GM_CHUNK, 1), lambda i: (0, 0, 0))],
        out_specs=pl.BlockSpec((tm, MIX_W), lambda i: (i, 0)),
        out_shape=jax.ShapeDtypeStruct((m, MIX_W), BF16),
        compiler_params=_cparams("parallel"),
        name="gmlp",
    )(z, z, ln_g.reshape(1, MIX_W), ln_b.reshape(1, MIX_W), w_s, b_s[:, :, None])


def _gmlp_first_kernel(u_ref, v_ref, lg_ref, lbias_ref, w00_ref, b0_ref, o_ref, vn_ref):
    vn = _layer_norm(jax.nn.gelu(v_ref[...]), lg_ref[...], lbias_ref[...])
    vn_ref[...] = vn
    o_ref[...] = (jax.nn.gelu(u_ref[...]) * (vn * w00_ref[...] + b0_ref[...])).astype(o_ref.dtype)


def _gmlp_first(z, col0, ln_g, ln_b, w_s, b_s):
    m = z.shape[0]
    assert col0 % MIX_W == 0
    j0 = col0 // MIX_W
    w00 = jnp.repeat(w_s[:, 0, 0], GM_GW).reshape(1, MIX_W)
    b0 = jnp.repeat(b_s[:, 0], GM_GW).reshape(1, MIX_W)
    vec = pl.BlockSpec((1, MIX_W), lambda i: (0, 0))
    return pl.pallas_call(
        _gmlp_first_kernel,
        grid=(1,),
        in_specs=[pl.BlockSpec((m, MIX_W), lambda i: (0, j0)), pl.BlockSpec((m, MIX_W), lambda i: (0, j0 + 1)),
                  vec, vec, vec, vec],
        out_specs=[pl.BlockSpec((m, MIX_W), lambda i: (0, 0)), pl.BlockSpec((m, MIX_W), lambda i: (0, 0))],
        out_shape=[jax.ShapeDtypeStruct((m, MIX_W), BF16), jax.ShapeDtypeStruct((m, MIX_W), F32)],
        name="gmlp_first",
    )(z, z, ln_g.reshape(1, MIX_W), ln_b.reshape(1, MIX_W), w00, b0)


def _layer_weights(l, w_in, fox_bf, s5_lambda_re, s5_lambda_im, s5_log_dt, s5_b_re, s5_b_im, s5_c_re, s5_c_im,
                   s5_w_glu, hgrn_lb, w_branch, w_out, w_ffn_up, w_ffn_down):
    n_a = 3 * MIX_W + LANES
    o_r = 3 * MIX_W + FOX_H
    lb_all = jnp.cumsum(jax.nn.softmax(hgrn_lb.astype(F32), axis=0), axis=0)
    return dict(
        w_a=_cast_cols(w_in, l, 0, n_a),
        w_r=_cast_cols(w_in, l, o_r, w_in.shape[2] - o_r),
        bf=jnp.pad(fox_bf[l], (0, LANES - FOX_H)).reshape(1, LANES),
        s5=_s5_params(s5_lambda_re[l], s5_lambda_im[l], s5_log_dt[l], s5_b_re[l], s5_b_im[l],
                      s5_c_re[l], s5_c_im[l]),
        w_glu=s5_w_glu[l].astype(BF16),
        lb=lb_all[l] - lb_all[0],
        w_branch=w_branch, w_out=w_out, w_up=w_ffn_up, w_down=w_ffn_down,
    )


def _trunk_layer(x, l, lw, P, *, bsz, seq, fox_fn, s5_h0, hg_s0):
    m = bsz * seq
    x_bf, q_bf, kv_bf, k_rows, v_rows_fox, logf = _qkv_proj(x, lw["w_a"], lw["bf"])
    logf = logf[:, :FOX_H]
    o_a = fox_fn(q_bf, kv_bf, k_rows, v_rows_fox, logf)

    z = _proj(x_bf, lw["w_r"], n=Z_GATE)[0]
    su_tb = z[:, :MIX_W].reshape(bsz, seq, MIX_W).transpose(1, 0, 2).reshape(m, MIX_W)
    o_b_tb, s5_re, s5_im = _s5(su_tb, lw["s5"], P["s5_d"][l], lw["w_glu"], P["s5_b_glu"][l],
                               s5_h0[0].reshape(bsz, S5_STATE), s5_h0[1].reshape(bsz, S5_STATE),
                               bk=bsz, steps=seq)
    o_b = o_b_tb.reshape(seq, bsz, MIX_W).transpose(1, 0, 2).reshape(m, MIX_W)

    seq_pad = seq if seq % HG_CHUNK == 0 else -(-seq // BF16_ROWS) * BF16_ROWS
    zh, zh_col0 = z, Z_HG
    if seq_pad != seq:
        zh = z[:, Z_HG:Z_GM].reshape(bsz, seq, 4 * MIX_W)
        zh, zh_col0 = jnp.pad(zh, ((0, 0), (0, seq_pad - seq), (0, 0))).reshape(-1, 4 * MIX_W), 0
    o_c, hg_t = _hgrn(zh, zh_col0, lw["lb"], P["hgrn_norm_g"][l], hg_s0.transpose(0, 1, 3, 2),
                      bsz=bsz, seq_pad=seq_pad, valid=min(seq, HG_CHUNK))
    if seq_pad != seq:
        o_c = o_c.reshape(bsz, seq_pad, MIX_W)[:, :seq].reshape(m, MIX_W)
    hg_state = hg_t.transpose(0, 1, 3, 2)

    if seq == 1:
        o_d, v_rows = _gmlp_first(z, Z_GM, P["gmlp_ln_g"][l], P["gmlp_ln_b"][l], P["gmlp_w_s"][l], P["gmlp_b_s"][l])
    else:
        o_d = _gmlp(z, Z_GM, P["gmlp_ln_g"][l], P["gmlp_ln_b"][l], P["gmlp_w_s"][l], P["gmlp_b_s"][l])
        v_rows = None

    merged = _merge(x_bf, (o_a, o_b, o_c, o_d), lw["w_r"], Z_GATE, lw["w_branch"], l)
    x1, x1_bf = _matmul_res_ln(merged, lw["w_out"], l, x, P["ln1_g"][l], P["ln1_b"][l], out_dtypes=(F32, BF16))
    hff = _ffn_up(x1_bf, lw["w_up"], l)
    x2, = _matmul_res_ln(hff, lw["w_down"], l, x1, P["ln2_g"][l], P["ln2_b"][l], out_dtypes=(F32,))

    fk = k_rows.reshape(bsz, seq, FOX_H, FOX_HD)
    fv = v_rows_fox.reshape(bsz, seq, FOX_H, FOX_HD)
    state = (fk, fv, logf.reshape(bsz, seq, FOX_H), s5_re.reshape(bsz, S5_G, S5_P),
             s5_im.reshape(bsz, S5_G, S5_P), hg_state, v_rows)
    return x2, state


def kernel(x_prompt, x_sample, cache_k, cache_v, cache_logf, page_table, state_s5_re, state_s5_im, state_hgrn,
           w_in, fox_bf, s5_lambda_re, s5_lambda_im, s5_log_dt, s5_b_re, s5_b_im, s5_c_re, s5_c_im, s5_d,
           s5_w_glu, s5_b_glu, hgrn_lb, hgrn_norm_g, gmlp_ln_g, gmlp_ln_b, gmlp_w_s, gmlp_b_s, w_branch, w_out,
           ln1_g, ln1_b, w_ffn_up, w_ffn_down, ln2_g, ln2_b):
    b_p, l_p, _ = x_prompt.shape
    b_s, l_s, _ = x_sample.shape
    assert l_s == 1, "the sample group decodes one token per sequence"
    P = dict(s5_d=s5_d, s5_b_glu=s5_b_glu, hgrn_norm_g=hgrn_norm_g, gmlp_ln_g=gmlp_ln_g, gmlp_ln_b=gmlp_ln_b,
             gmlp_w_s=gmlp_w_s, gmlp_b_s=gmlp_b_s, ln1_g=ln1_g, ln1_b=ln1_b, ln2_g=ln2_g, ln2_b=ln2_b)

    xp = x_prompt.reshape(b_p * l_p, D_MODEL)
    xs = x_sample.reshape(b_s * l_s, D_MODEL)
    s5_zero = jnp.zeros((b_p, S5_G, S5_P), F32)
    hg_zero = jnp.zeros((b_p, HG_H, HG_DK, HG_DV), F32)
    p_states, s_states = [], []
    w_branch, w_out, w_ffn_down = (w.astype(BF16) for w in (w_branch, w_out, w_ffn_down))
    for l in range(DEPTH):
        lw = _layer_weights(l, w_in, fox_bf, s5_lambda_re, s5_lambda_im, s5_log_dt, s5_b_re, s5_b_im,
                            s5_c_re, s5_c_im, s5_w_glu, hgrn_lb, w_branch, w_out, w_ffn_up, w_ffn_down)

        def fox_p(q_bf, kv_bf, k_rows, v_rows, logf):
            c = _cumsum_lanes(logf.reshape(b_p, l_p, FOX_H).transpose(0, 2, 1).reshape(b_p * FOX_H, l_p))
            return _fox_prompt(q_bf, kv_bf, c.reshape(b_p, FOX_H, l_p), b_p, l_p)

        def fox_s(q_bf, kv_bf, k_rows, v_rows, logf, l=l):
            o = _fox_decode(page_table, q_bf, k_rows.reshape(b_s, MIX_W), v_rows.reshape(b_s, MIX_W),
                            logf, cache_k, cache_v, cache_logf, l)
            return o.astype(BF16)

        xp, sp = _trunk_layer(xp, l, lw, P, bsz=b_p, seq=l_p, fox_fn=fox_p,
                              s5_h0=(s5_zero, s5_zero), hg_s0=hg_zero)
        xs, ss = _trunk_layer(xs, l, lw, P, bsz=b_s, seq=l_s, fox_fn=fox_s,
                              s5_h0=(state_s5_re[l], state_s5_im[l]), hg_s0=state_hgrn[l])
        p_states.append(sp)
        s_states.append(ss)

    stack = lambda states, i: jnp.stack([s[i] for s in states])
    return (xp.reshape(b_p, l_p, D_MODEL), xs.reshape(b_s, l_s, D_MODEL),
            stack(p_states, 0), stack(p_states, 1), stack(p_states, 2),
            stack(s_states, 0), stack(s_states, 1), stack(s_states, 2),
            stack(p_states, 3), stack(p_states, 4), stack(s_states, 3), stack(s_states, 4),
            stack(p_states, 5), stack(s_states, 5),
            jnp.stack([s[6].reshape(b_s, l_s, MIX_W) for s in s_states]))
```

```python
import functools
import math

import numpy as np
import jax
import jax.numpy as jnp
from jax import lax
from jax.experimental import pallas as pl
from jax.experimental.pallas import tpu as pltpu

F32 = jnp.float32
BF16 = jnp.bfloat16

D_MODEL = 2048
DEPTH = 2
N_BRANCH = 4
MIX_W = D_MODEL // N_BRANCH
FOX_HD = 128
FOX_H = MIX_W // FOX_HD
S5_GROUP = 16
S5_G = MIX_W // S5_GROUP
S5_P = 64
S5_STATE = S5_G * S5_P
S5_SLAB = 512
HG_DK = 128
HG_DV = 128
HG_H = MIX_W // HG_DV
HG_CHUNK = 64
HG_SUB = 16
HG_SEQS = 8
HG_SAFE_RANGE = 44.0
GM_CHUNK = 128
GM_GROUPS = 4
GM_GW = MIX_W // GM_GROUPS
GM_STEP_CHUNKS = 8
D_FF = ((8 * D_MODEL // 3 + 255) // 256) * 256
DEEPNORM_ALPHA = (2 * DEPTH) ** 0.25
LN_EPS = 1e-5
RMS_EPS = 1e-6
LANES = 128
SUBLANES = 8
BF16_ROWS = 16
FOX_STEP_HEADS = 4
FOX_DECODE_PAGES = 32
FOX_DECODE_SEQS = 1
Z_HG = MIX_W
Z_GM = Z_HG + 4 * MIX_W
Z_GATE = Z_GM + 2 * MIX_W
VMEM_LIMIT = 48 * 1024 * 1024
VMEM_LIMIT_BIG = 58 * 1024 * 1024
PROJ_TN_MAX = 1792


def _cparams(*sem):
    return pltpu.CompilerParams(dimension_semantics=sem, vmem_limit_bytes=VMEM_LIMIT)


def _log_sigmoid(x):
    return jnp.minimum(x, 0.0) - jnp.log1p(jnp.exp(-jnp.abs(x)))


def _layer_norm(x, g, b):
    mu = jnp.mean(x, axis=-1, keepdims=True)
    xc = x - mu
    var = jnp.mean(xc * xc, axis=-1, keepdims=True)
    return xc * lax.rsqrt(var + LN_EPS) * g + b


def _split3(x):
    hi = x.astype(BF16)
    r1 = x - hi.astype(F32)
    mid = r1.astype(BF16)
    lo = (r1 - mid.astype(F32)).astype(BF16)
    return hi, mid, lo


def _dot(a, b):
    return jnp.dot(a, b, preferred_element_type=F32)


def _dot_nt(a, b):
    return lax.dot_general(a, b, (((1,), (1,)), ((), ())), preferred_element_type=F32)


def _dot_tn(a, b):
    return lax.dot_general(a, b, (((0,), (0,)), ((), ())), preferred_element_type=F32)


def _cast_cols_kernel(w_ref, o_ref, *, l, depth):
    n_kt = o_ref.shape[0] // LANES
    rows_per_col = n_kt * depth
    for cb in range(o_ref.shape[1] // LANES):
        for kt in range(n_kt):
            first = cb * LANES * rows_per_col + kt * depth + l
            tile = w_ref[pl.ds(first, LANES, stride=rows_per_col), :]
            o_ref[kt * LANES:(kt + 1) * LANES, cb * LANES:(cb + 1) * LANES] = tile.T.astype(o_ref.dtype)


def _cast_cols(w, l, col0, n):
    depth, k, n_all = w.shape
    rows_per_col = (k // LANES) * depth
    view = w.reshape(depth, k // LANES, LANES, n_all).transpose(3, 1, 0, 2).reshape(n_all * rows_per_col, LANES)
    tn = max(t for t in (LANES, 2 * LANES, 4 * LANES) if n % t == 0)
    blk = tn * rows_per_col
    return pl.pallas_call(
        functools.partial(_cast_cols_kernel, l=l, depth=depth),
        grid=(n // tn,),
        in_specs=[pl.BlockSpec((pl.Element(blk), pl.Element(LANES)),
                               lambda j: (pl.multiple_of(col0 * rows_per_col + j * blk, rows_per_col), 0))],
        out_specs=pl.BlockSpec((k, tn), lambda j: (0, j)),
        out_shape=jax.ShapeDtypeStruct((k, n), BF16),
        compiler_params=_cparams("parallel"),
        name="cast_cols",
    )(view)


def _proj_kernel(x_ref, w_ref, b_ref, *out_refs, act):
    z = _dot(x_ref[...], w_ref[...])
    if act == "gelu":
        z = jax.nn.gelu(z)
    elif act == "sigmoid":
        z = jax.nn.sigmoid(z)
    elif act == "log_sigmoid_bias":
        z = _log_sigmoid(z + b_ref[...])
    for o in out_refs:
        o[...] = z.astype(o.dtype)


def _proj(x, w, *, col0=0, n=None, act=None, bias=None, out_dtypes=(F32,)):
    m, k = x.shape
    n = w.shape[1] if n is None else n
    tm = min(m, 512)
    tn = max(t for t in range(LANES, min(n, PROJ_TN_MAX) + 1, LANES) if n % t == 0 and col0 % t == 0)
    j0 = col0 // tn
    if bias is None:
        bias = jnp.zeros((1, n), F32)
    return pl.pallas_call(
        functools.partial(_proj_kernel, act=act),
        grid=(n // tn, m // tm),
        in_specs=[pl.BlockSpec((tm, k), lambda j, i: (i, 0)),
                  pl.BlockSpec((k, tn), lambda j, i: (0, j0 + j)),
                  pl.BlockSpec((1, tn), lambda j, i: (0, j))],
        out_specs=[pl.BlockSpec((tm, tn), lambda j, i: (i, j)) for _ in out_dtypes],
        out_shape=[jax.ShapeDtypeStruct((m, n), d) for d in out_dtypes],
        compiler_params=_cparams("parallel", "arbitrary"),
        name="proj_" + (act or "id"),
    )(x, w, bias)


def _qkv_kernel(x_ref, w_ref, b_ref, xbf_ref, q_ref, kv_ref, k_ref, v_ref, lf_ref):
    x = x_ref[...].astype(BF16)
    xbf_ref[...] = x
    z = _dot(x, w_ref[...])
    tm = z.shape[0]
    q_ref[...] = z[:, :MIX_W].astype(q_ref.dtype)
    kv_ref[...] = z[:, MIX_W:3 * MIX_W].astype(kv_ref.dtype)
    for h in range(FOX_H):
        k_ref[pl.ds(h, tm, stride=FOX_H), :] = z[:, MIX_W + h * FOX_HD:MIX_W + (h + 1) * FOX_HD]
        v_ref[pl.ds(h, tm, stride=FOX_H), :] = z[:, 2 * MIX_W + h * FOX_HD:2 * MIX_W + (h + 1) * FOX_HD]
    lf_ref[...] = _log_sigmoid(z[:, 3 * MIX_W:] + b_ref[...])


def _qkv_proj(x, w, bias):
    m, k = x.shape
    tm = min(m, 512)
    n = w.shape[1]
    row = lambda width: pl.BlockSpec((tm, width), lambda i: (i, 0))
    rows = pl.BlockSpec((tm * FOX_H, FOX_HD), lambda i: (i, 0))
    return pl.pallas_call(
        _qkv_kernel,
        grid=(m // tm,),
        in_specs=[row(k), pl.BlockSpec((k, n), lambda i: (0, 0), pipeline_mode=pl.Buffered(1)),
                  pl.BlockSpec((1, LANES), lambda i: (0, 0))],
        out_specs=[row(k), row(MIX_W), row(2 * MIX_W), rows, rows, row(LANES)],
        out_shape=[jax.ShapeDtypeStruct((m, k), BF16), jax.ShapeDtypeStruct((m, MIX_W), BF16),
                   jax.ShapeDtypeStruct((m, 2 * MIX_W), BF16),
                   jax.ShapeDtypeStruct((m * FOX_H, FOX_HD), F32), jax.ShapeDtypeStruct((m * FOX_H, FOX_HD), F32),
                   jax.ShapeDtypeStruct((m, LANES), F32)],
        compiler_params=_cparams("parallel"),
        name="qkv_proj",
    )(x, w, bias)


def _merge_kernel(x_ref, oa_ref, ob_ref, oc_ref, od_ref, g0_ref, g1_ref, g2_ref, g3_ref, wb_ref, o_ref):
    x = x_ref[...]
    acc = None
    for br, gr, k in ((oa_ref, g0_ref, 0), (ob_ref, g1_ref, 1), (oc_ref, g2_ref, 2), (od_ref, g3_ref, 3)):
        t = jax.nn.sigmoid(_dot(x, gr[...])) * _dot(br[...], wb_ref[k])
        acc = t if acc is None else acc + t
    o_ref[...] = acc.astype(o_ref.dtype)


def _merge(x, branches, w, gate_col0, w_branch, l):
    m, k_in = x.shape
    tm = min(m, 512)
    tn = 512
    nb = D_MODEL // tn
    assert gate_col0 % tn == 0
    j0 = gate_col0 // tn
    gate_specs = [pl.BlockSpec((k_in, tn), functools.partial(lambda j, i, k: (0, j0 + k * nb + j), k=k))
                  for k in range(N_BRANCH)]
    return pl.pallas_call(
        _merge_kernel,
        grid=(nb, m // tm),
        in_specs=[pl.BlockSpec((tm, k_in), lambda j, i: (i, 0))]
                 + [pl.BlockSpec((tm, MIX_W), lambda j, i: (i, 0)) for _ in range(N_BRANCH)] + gate_specs
                 + [pl.BlockSpec((None, N_BRANCH, MIX_W, tn), lambda j, i: (l, 0, 0, j))],
        out_specs=pl.BlockSpec((tm, tn), lambda j, i: (i, j)),
        out_shape=jax.ShapeDtypeStruct((m, D_MODEL), BF16),
        compiler_params=_cparams("parallel", "arbitrary"),
        name="merge",
    )(x, *branches, w, w, w, w, w_branch)


def _res_ln_kernel(a_ref, w_ref, x_ref, g_ref, b_ref, *y_refs):
    y = _layer_norm(DEEPNORM_ALPHA * x_ref[...] + _dot(a_ref[...], w_ref[...]), g_ref[...], b_ref[...])
    for y_ref in y_refs:
        y_ref[...] = y.astype(y_ref.dtype)


def _matmul_res_ln(a, w, l, x, g, b, *, out_dtypes):
    m, k = a.shape
    n = w.shape[2]
    tm = min(m, 512 if k <= D_MODEL else 256)
    return pl.pallas_call(
        _res_ln_kernel,
        grid=(m // tm,),
        in_specs=[pl.BlockSpec((tm, k), lambda i: (i, 0)),
                  pl.BlockSpec((None, k, n), lambda i: (l, 0, 0), pipeline_mode=pl.Buffered(1)),
                  pl.BlockSpec((tm, n), lambda i: (i, 0)),
                  pl.BlockSpec((1, n), lambda i: (0, 0)),
                  pl.BlockSpec((1, n), lambda i: (0, 0))],
        out_specs=[pl.BlockSpec((tm, n), lambda i: (i, 0)) for _ in out_dtypes],
        out_shape=[jax.ShapeDtypeStruct((m, n), d) for d in out_dtypes],
        compiler_params=pltpu.CompilerParams(dimension_semantics=("parallel",), vmem_limit_bytes=VMEM_LIMIT_BIG),
        name="matmul_res_ln",
    )(a, w, x, g.reshape(1, n), b.reshape(1, n))


def _ffn_up_kernel(x_ref, wg_ref, wu_ref, o_ref, wg_sc, wu_sc):
    @pl.when(pl.program_id(1) == 0)
    def _():
        wg_sc[...] = wg_ref[...].astype(BF16)
        wu_sc[...] = wu_ref[...].astype(BF16)

    x = x_ref[...]
    o_ref[...] = (jax.nn.silu(_dot(x, wg_sc[...])) * _dot(x, wu_sc[...])).astype(o_ref.dtype)


def _ffn_up(x, w_up, l):
    m, k = x.shape
    tm = min(m, 1024)
    tn = 512
    nb = D_FF // tn
    return pl.pallas_call(
        _ffn_up_kernel,
        grid=(nb, m // tm),
        in_specs=[pl.BlockSpec((tm, k), lambda j, i: (i, 0)),
                  pl.BlockSpec((None, k, tn), lambda j, i: (l, 0, j)),
                  pl.BlockSpec((None, k, tn), lambda j, i: (l, 0, nb + j))],
        out_specs=pl.BlockSpec((tm, tn), lambda j, i: (i, j)),
        out_shape=jax.ShapeDtypeStruct((m, D_FF), BF16),
        scratch_shapes=[pltpu.VMEM((k, tn), BF16), pltpu.VMEM((k, tn), BF16)],
        compiler_params=_cparams("parallel", "arbitrary"),
        name="ffn_up",
    )(x, w_up, w_up)


def _cumsum_lanes_kernel(x_ref, o_ref):
    x = x_ref[...]
    n = x.shape[-1]
    lane = lax.broadcasted_iota(jnp.int32, x.shape, 1)
    sh = 1
    while sh < n:
        x = x + jnp.where(lane >= sh, pltpu.roll(x, sh, axis=1), 0.0)
        sh *= 2
    o_ref[...] = x


def _cumsum_lanes(x):
    return pl.pallas_call(_cumsum_lanes_kernel, out_shape=jax.ShapeDtypeStruct(x.shape, F32),
                          name="cumsum_lanes")(x)


def _flash_kernel(qi_ref, ki_ref, q_ref, k_ref, v_ref, cq_ref, ck_ref, o_ref, qt_sc, m_sc, l_sc, acc_sc, *,
                  tq, tk, scale):
    qi = qi_ref[pl.program_id(2)]
    ki = ki_ref[pl.program_id(2)]

    heads = [(h, slice(h * FOX_HD, (h + 1) * FOX_HD)) for h in range(m_sc.shape[0])]

    @pl.when(ki == 0)
    def _():
        for h, cols in heads:
            qt_sc[h] = q_ref[:, cols].astype(F32).T.astype(BF16)
        m_sc[...] = jnp.full_like(m_sc, -jnp.inf)
        l_sc[...] = jnp.zeros_like(l_sc)
        acc_sc[...] = jnp.zeros_like(acc_sc)

    def update(masked):
        for h, cols in heads:
            s = _dot(k_ref[:, cols], qt_sc[h]) * scale
            s = s + cq_ref[0, h] - jnp.concatenate([ck_ref[0, h]] * (tq // LANES), axis=1)
            if masked:
                k_pos = lax.broadcasted_iota(jnp.int32, (tk, tq), 0)
                q_pos = lax.broadcasted_iota(jnp.int32, (tk, tq), 1)
                s = jnp.where(k_pos <= q_pos, s, -jnp.inf)
            m_prev = m_sc[h]
            m_new = jnp.maximum(m_prev, jnp.max(s, axis=0, keepdims=True))
            alpha = jnp.exp(m_prev - m_new)
            p = jnp.exp(s - m_new)
            l_sc[h] = alpha * l_sc[h] + jnp.sum(p, axis=0, keepdims=True)
            acc_sc[h] = alpha * acc_sc[h] + _dot_tn(v_ref[:, cols], p.astype(BF16))
            m_sc[h] = m_new

    pl.when(ki < qi)(functools.partial(update, False))

    @pl.when(ki == qi)
    def _():
        update(True)
        for h, cols in heads:
            o_ref[:, cols] = (acc_sc[h] / l_sc[h]).T.astype(o_ref.dtype)


def _fox_prompt(q, kv, c, bsz, seq):
    tq = tk = min(seq, 512)
    nq = seq // tq
    nk = seq // tk
    cq = c.reshape(bsz, FOX_H, 1, seq)
    ck = jnp.broadcast_to(c[..., None], (bsz, FOX_H, seq, LANES))

    pairs = [(qi, ki) for qi in range(nq) for ki in range(qi + 1)]
    qi_tab = jnp.asarray([p[0] for p in pairs], jnp.int32)
    ki_tab = jnp.asarray([p[1] for p in pairs], jnp.int32)
    hs = FOX_STEP_HEADS
    ng = FOX_H // hs
    wide = hs * FOX_HD
    grid_spec = pltpu.PrefetchScalarGridSpec(
        num_scalar_prefetch=2,
        grid=(bsz, ng, len(pairs)),
        in_specs=[pl.BlockSpec((tq, wide), lambda b, g, s, qt, kt: (b * nq + qt[s], g)),
                  pl.BlockSpec((tk, wide), lambda b, g, s, qt, kt: (b * nk + kt[s], g)),
                  pl.BlockSpec((tk, wide), lambda b, g, s, qt, kt: (b * nk + kt[s], ng + g)),
                  pl.BlockSpec((1, hs, 1, tq), lambda b, g, s, qt, kt: (b, g, 0, qt[s])),
                  pl.BlockSpec((1, hs, tk, LANES), lambda b, g, s, qt, kt: (b, g, kt[s], 0))],
        out_specs=pl.BlockSpec((tq, wide), lambda b, g, s, qt, kt: (b * nq + qt[s], g)),
        scratch_shapes=[pltpu.VMEM((hs, FOX_HD, tq), BF16), pltpu.VMEM((hs, 1, tq), F32),
                        pltpu.VMEM((hs, 1, tq), F32), pltpu.VMEM((hs, FOX_HD, tq), F32)],
    )
    return pl.pallas_call(
        functools.partial(_flash_kernel, tq=tq, tk=tk, scale=FOX_HD ** -0.5),
        grid_spec=grid_spec,
        out_shape=jax.ShapeDtypeStruct((bsz * seq, MIX_W), BF16),
        compiler_params=_cparams("parallel", "parallel", "arbitrary"),
        name="fox_prompt",
    )(qi_tab, ki_tab, q, kv, kv, cq, ck)


def _fox_decode_kernel(pt_ref, q_ref, cnew_ref, knew_ref, vnew_ref, lf_pool_ref, *rest, scale, page, group, nseq,
                       n_pages):
    n = nseq * group
    o_ref, m_sc, l_sc, acc_sc, carry_sc = rest[2 * n:]
    b = pl.program_id(0)
    p = pl.program_id(1)

    @pl.when(p == 0)
    def _():
        m_sc[...] = jnp.full_like(m_sc, -jnp.inf)
        l_sc[...] = jnp.zeros_like(l_sc)
        acc_sc[...] = jnp.zeros_like(acc_sc)
        carry_sc[...] = jnp.zeros_like(carry_sc)

    for r in range(nseq):
        sl = slice(r * group, (r + 1) * group)
        newest = (b * nseq + r) * n_pages + (n_pages - 1) - p * group
        lf_views = [lf_pool_ref.at[pt_ref[newest - j]] for j in range(group)]
        _fox_decode_pages(q_ref.at[r], cnew_ref.at[r], rest[:n][sl], rest[n:2 * n][sl], lf_views,
                          m_sc.at[r], l_sc.at[r], acc_sc.at[r], carry_sc.at[r], scale=scale, page=page, group=group)

    @pl.when(p == pl.num_programs(1) - 1)
    def _():
        for r in range(nseq):
            _fox_decode_finish(q_ref.at[r], knew_ref.at[r], vnew_ref.at[r], o_ref.at[r], m_sc.at[r], l_sc.at[r],
                               acc_sc.at[r], scale=scale)


def _fox_decode_pages(q_ref, cnew_ref, k_refs, v_refs, lf_refs, m_sc, l_sc, acc_sc, carry_sc, *, scale, page, group):
    hcol = lax.broadcasted_iota(jnp.int32, (BF16_ROWS, 1), 0)
    later = jnp.where(lax.broadcasted_iota(jnp.int32, (page, page), 0)
                      > lax.broadcasted_iota(jnp.int32, (page, page), 1), 1.0, 0.0).astype(BF16)

    q = q_ref[...]
    carry = carry_sc[...]
    cnew = cnew_ref[...]
    parts = []
    for j in range(group):
        lf = lf_refs[j][...]
        lf16 = jnp.zeros((BF16_ROWS, page), F32)
        for h in range(FOX_H):
            lf16 = jnp.where(hcol == h, lf[h:h + 1, :], lf16)
        hi, mid, lo = _split3(lf16)
        suffix = _dot(hi, later) + _dot(mid, later) + _dot(lo, later)
        s = None
        for h in range(FOX_H):
            kh = k_refs[j][pl.ds(h, page, stride=FOX_H), :]
            t = _dot_nt(q[:, h * FOX_HD:(h + 1) * FOX_HD], kh.astype(BF16))
            s = t if s is None else s + t
        parts.append(s * scale + (suffix + carry + cnew))
        carry = carry + jnp.sum(lf16, axis=-1, keepdims=True)
    carry_sc[...] = carry
    s = jnp.concatenate(parts, axis=1)

    m_prev = m_sc[...]
    m_new = jnp.maximum(m_prev, jnp.max(s, axis=-1, keepdims=True))
    alpha = jnp.exp(m_prev - m_new)
    pr = jnp.exp(s - m_new)
    l_sc[...] = alpha * l_sc[...] + jnp.sum(pr, axis=-1, keepdims=True)
    pr = pr.astype(BF16)
    for h in range(FOX_H):
        cols = slice(h * FOX_HD, (h + 1) * FOX_HD)
        upd = None
        for j in range(group):
            vh = v_refs[j][pl.ds(h, page, stride=FOX_H), :]
            t = _dot(pr[:, j * page:(j + 1) * page], vh.astype(BF16))
            upd = t if upd is None else upd + t
        acc_sc[:, cols] = alpha * acc_sc[:, cols] + upd
    m_sc[...] = m_new


def _fox_decode_finish(q_ref, knew_ref, vnew_ref, o_ref, m_sc, l_sc, acc_sc, *, scale):
    hrow = lax.broadcasted_iota(jnp.int32, (BF16_ROWS, MIX_W), 0)
    lane = lax.broadcasted_iota(jnp.int32, (BF16_ROWS, MIX_W), 1)
    kn = knew_ref[...].astype(BF16).astype(F32)
    vn = vnew_ref[...].astype(BF16).astype(F32)
    s_new = jnp.sum(q_ref[...].astype(F32) * kn, axis=-1, keepdims=True) * scale
    m_prev = m_sc[...]
    m_fin = jnp.maximum(m_prev, s_new)
    alpha = jnp.exp(m_prev - m_fin)
    p_new = jnp.exp(s_new - m_fin)
    l_fin = alpha * l_sc[...] + p_new
    acc = alpha * acc_sc[...] + p_new.astype(BF16).astype(F32) * vn
    o = acc / l_fin
    o_ref[...] = jnp.sum(jnp.where((lane >> 7) == hrow, o, 0.0), axis=0, keepdims=True)


def _fox_decode(page_table, q, k_new, v_new, c_new, cache_k, cache_v, cache_lf, l):
    bsz, n_pages = page_table.shape
    depth, n_phys, page = cache_k.shape[:3]
    group = math.gcd(n_pages, FOX_DECODE_PAGES)
    cache_k = cache_k.reshape(depth, n_phys, page * FOX_H, FOX_HD)
    cache_v = cache_v.reshape(depth, n_phys, page * FOX_H, FOX_HD)
    cache_lf = cache_lf.transpose(0, 1, 3, 2)
    hmask = (np.arange(MIX_W)[None, :] // FOX_HD) == np.arange(BF16_ROWS)[:, None]
    q_rows = jnp.where(hmask[None], q[:, None, :], 0.0).astype(BF16)
    c_rows = jnp.pad(c_new, ((0, 0), (0, BF16_ROWS - FOX_H)))[:, :, None]
    pt = page_table.reshape(-1)

    nseq = math.gcd(bsz, FOX_DECODE_SEQS)
    slots = [(r, j) for r in range(nseq) for j in range(group)]

    def pg(r, j):
        return lambda b, p, pt_ref: pt_ref[(b * nseq + r) * n_pages + (n_pages - 1 - (p * group + j))]

    def kv_spec(r, j):
        return pl.BlockSpec((None, None, page * FOX_H, FOX_HD),
                            lambda b, p, pt_ref: (l, pg(r, j)(b, p, pt_ref), 0, 0))

    lf_pool = pl.BlockSpec((None, n_phys, FOX_H, page), lambda b, p, pt_ref: (l, 0, 0, 0),
                           pipeline_mode=pl.Buffered(1))
    per_seq = lambda *shape: pl.BlockSpec((nseq,) + shape, lambda b, p, pt_ref: (b,) + (0,) * len(shape))
    grid_spec = pltpu.PrefetchScalarGridSpec(
        num_scalar_prefetch=1,
        grid=(bsz // nseq, n_pages // group),
        in_specs=[per_seq(BF16_ROWS, MIX_W), per_seq(BF16_ROWS, 1), per_seq(1, MIX_W), per_seq(1, MIX_W), lf_pool]
                 + [kv_spec(r, j) for r, j in slots] + [kv_spec(r, j) for r, j in slots],
        out_specs=per_seq(1, MIX_W),
        scratch_shapes=[pltpu.VMEM((nseq, BF16_ROWS, 1), F32), pltpu.VMEM((nseq, BF16_ROWS, 1), F32),
                        pltpu.VMEM((nseq, BF16_ROWS, MIX_W), F32), pltpu.VMEM((nseq, BF16_ROWS, 1), F32)],
    )
    out = pl.pallas_call(
        functools.partial(_fox_decode_kernel, scale=FOX_HD ** -0.5, page=page, group=group, nseq=nseq,
                          n_pages=n_pages),
        grid_spec=grid_spec,
        out_shape=jax.ShapeDtypeStruct((bsz, 1, MIX_W), F32),
        compiler_params=pltpu.CompilerParams(dimension_semantics=("parallel", "arbitrary"),
                                             vmem_limit_bytes=VMEM_LIMIT_BIG),
        name="fox_decode",
    )(pt, q_rows, c_rows, k_new[:, None, :], v_new[:, None, :], cache_lf,
      *([cache_k] * len(slots)), *([cache_v] * len(slots)))
    return out.reshape(bsz, MIX_W)


def _s5_kernel(u_ref, bblk_ref, cblk_ref, are_ref, aim_ref, d_ref, wglu_ref, bglu_ref, h0re_ref, h0im_ref,
               o_ref, hre_ref, him_ref, hs_sc, *, tc, bk, slab):
    c = pl.program_id(0)

    @pl.when(c == 0)
    def _():
        hre_ref[...] = h0re_ref[...]
        him_ref[...] = h0im_ref[...]

    u = u_ref[...]
    u_bf = u.astype(BF16)
    n_u = slab // S5_P * S5_GROUP
    y_parts = []
    for si, s0 in enumerate(range(0, S5_STATE, slab)):
        re_sl = slice(s0, s0 + slab)
        im_sl = slice(S5_STATE + s0, S5_STATE + s0 + slab)
        u_sl = slice(si * n_u, (si + 1) * n_u)
        bu = _dot(u_bf[:, u_sl], bblk_ref[u_sl, :])
        hs_sc[:, re_sl] = bu[:, :slab]
        hs_sc[:, im_sl] = bu[:, slab:]
        ar = jnp.broadcast_to(are_ref[:, re_sl], (bk, slab))
        ai = jnp.broadcast_to(aim_ref[:, re_sl], (bk, slab))

        def step(t, carry, re_sl=re_sl, im_sl=im_sl, ar=ar, ai=ai):
            hr, hi = carry
            rows = pl.ds(pl.multiple_of(t * bk, bk), bk)
            nr = ar * hr - ai * hi + hs_sc[rows, re_sl]
            ni = ar * hi + ai * hr + hs_sc[rows, im_sl]
            hs_sc[rows, re_sl] = nr
            hs_sc[rows, im_sl] = ni
            return nr, ni

        hr, hi = lax.fori_loop(0, tc, step, (hre_ref[:, re_sl], him_ref[:, re_sl]))
        hre_ref[:, re_sl] = hr
        him_ref[:, re_sl] = hi
        h_bf = jnp.concatenate([hs_sc[:, re_sl], hs_sc[:, im_sl]], axis=1).astype(BF16)
        y_parts.append(_dot(h_bf, cblk_ref[si * 2 * slab:(si + 1) * 2 * slab, :]))
    y = jnp.concatenate(y_parts, axis=1) + d_ref[...] * u
    g = jax.nn.gelu(y)
    gate = jax.nn.sigmoid(_dot(g.astype(BF16), wglu_ref[...]) + bglu_ref[...])
    o_ref[...] = (g * gate).astype(o_ref.dtype)


def _s5_params(lam_re, lam_im, log_dt, b_re, b_im, c_re, c_im):
    dt = jnp.exp(log_dt)[:, None]
    mag = jnp.exp(lam_re * dt)
    ab_re, ab_im = mag * jnp.cos(lam_im * dt), mag * jnp.sin(lam_im * dt)
    den = lam_re * lam_re + lam_im * lam_im
    nr, ni = ab_re - 1.0, ab_im
    coef_re = (nr * lam_re + ni * lam_im) / den
    coef_im = (ni * lam_re - nr * lam_im) / den
    bb_re = coef_re[..., None] * b_re - coef_im[..., None] * b_im
    bb_im = coef_re[..., None] * b_im + coef_im[..., None] * b_re
    gs = S5_SLAB // S5_P
    ns = S5_G // gs
    eye = jnp.eye(gs, dtype=F32)
    blk_b = lambda t: jnp.einsum("sgph,gk->sghkp", t.reshape(ns, gs, S5_P, S5_GROUP), eye).reshape(MIX_W, S5_SLAB)
    blk_c = lambda t: jnp.einsum("sghp,gk->sgpkh", t.reshape(ns, gs, S5_GROUP, S5_P), eye).reshape(
        ns, S5_SLAB, gs * S5_GROUP)
    bblk = jnp.concatenate([blk_b(bb_re), blk_b(bb_im)], axis=1).astype(BF16)
    cblk = jnp.concatenate([blk_c(c_re), blk_c(-c_im)], axis=1).astype(BF16)
    return (ab_re.reshape(1, S5_STATE), ab_im.reshape(1, S5_STATE), bblk,
            cblk.reshape(ns * 2 * S5_SLAB, gs * S5_GROUP))


def _s5(u_tb, params, d, w_glu, b_glu, h0_re, h0_im, *, bk, steps):
    a_re, a_im, bblk, cblk = params
    tc = min(steps, 64)
    rows = tc * bk
    const = lambda shape: pl.BlockSpec(shape, lambda c: (0,) * len(shape))
    return pl.pallas_call(
        functools.partial(_s5_kernel, tc=tc, bk=bk, slab=S5_SLAB),
        grid=(steps // tc,),
        in_specs=[pl.BlockSpec((rows, MIX_W), lambda c: (c, 0)),
                  const(bblk.shape), const(cblk.shape),
                  const((1, S5_STATE)), const((1, S5_STATE)), const((1, MIX_W)),
                  const((MIX_W, MIX_W)), const((1, MIX_W)),
                  const((bk, S5_STATE)), const((bk, S5_STATE))],
        out_specs=[pl.BlockSpec((rows, MIX_W), lambda c: (c, 0)),
                   const((bk, S5_STATE)), const((bk, S5_STATE))],
        out_shape=[jax.ShapeDtypeStruct((steps * bk, MIX_W), BF16),
                   jax.ShapeDtypeStruct((bk, S5_STATE), F32), jax.ShapeDtypeStruct((bk, S5_STATE), F32)],
        scratch_shapes=[pltpu.VMEM((rows, 2 * S5_STATE), F32)],
        compiler_params=_cparams("arbitrary"),
        name="s5",
    )(u_tb, bblk, cblk, a_re, a_im, d.reshape(1, MIX_W), w_glu, b_glu.reshape(1, MIX_W), h0_re, h0_im)


def _hgrn_kernel(q_ref, f_ref, i_ref, g_ref, lb_ref, ng_ref, tri_ref, s0_ref, o_ref, st_ref, *, c, sub, valid):
    n = pl.program_id(1)

    @pl.when(n == 0)
    def _():
        st_ref[...] = s0_ref[...]

    n_seq = q_ref.shape[0]
    lb = lb_ref[...]
    la = jnp.log(lb)
    lb1 = jnp.log1p(-lb)
    decay = []
    worst = jnp.zeros((1, MIX_W), F32)
    for r in range(n_seq):
        lbb = lb1 + _log_sigmoid(f_ref[r])
        lf = jnp.maximum(la, lbb) + jnp.log1p(jnp.exp(-jnp.abs(la - lbb)))
        if valid < c:
            lf = jnp.where(lax.broadcasted_iota(jnp.int32, lf.shape, 0) < valid, lf, 0.0)
        parts = _dot(tri_ref[...], jnp.concatenate(_split3(lf), axis=1))
        bc = parts[:, :MIX_W] + parts[:, MIX_W:2 * MIX_W] + parts[:, 2 * MIX_W:]
        decay.append((1.0 - jnp.exp(lf), bc))
        mid = bc[c // 2 - 1:c // 2]
        worst = jnp.maximum(worst, jnp.maximum(-mid, mid - bc[c - 1:c]))
    safe = jnp.max(worst) < HG_SAFE_RANGE

    def run(exact):
        for r in range(n_seq):
            kt_all, bc_all = decay[r]
            for h in range(HG_H):
                cols = slice(h * HG_DK, (h + 1) * HG_DK)
                o, st = _hgrn_head(q_ref[r, :, cols], kt_all[:, cols], bc_all[:, cols], i_ref[r, :, cols],
                                   st_ref[r, h], c=c, sub=sub, exact=exact)
                st_ref[r, h] = st
                o = o * lax.rsqrt(jnp.mean(o * o, axis=-1, keepdims=True) + RMS_EPS) * ng_ref[:, cols]
                o_ref[r, :, cols] = (o * jax.nn.silu(g_ref[r, :, cols])).astype(o_ref.dtype)

    pl.when(safe)(functools.partial(run, False))
    pl.when(jnp.logical_not(safe))(functools.partial(run, True))


def _hgrn_head(q, kt, bc, v, st, *, c, sub, exact):
    v_bf = v.astype(BF16)
    o = _dot_nt((q * jnp.exp(bc)).astype(BF16), st.astype(BF16))
    if not exact:
        rel = bc - bc[c // 2 - 1:c // 2]
        att = _dot_nt((q * jnp.exp(rel)).astype(BF16), (kt * jnp.exp(-rel)).astype(BF16))
        t_pos = lax.broadcasted_iota(jnp.int32, (c, c), 0)
        s_pos = lax.broadcasted_iota(jnp.int32, (c, c), 1)
        o = o + _dot(jnp.where(s_pos <= t_pos, att, 0.0).astype(BF16), v_bf)
    else:
        nb = c // sub
        rows = lax.broadcasted_iota(jnp.int32, (sub, HG_DK), 0)
        blocks = []
        for i in range(nb):
            lo_r, hi_r = i * sub, (i + 1) * sub
            bci = bc[lo_r:hi_r]
            qi = q[lo_r:hi_r]
            oi = o[lo_r:hi_r]
            if i > 0:
                e = bc[lo_r - 1:lo_r]
                qs = (qi * jnp.exp(bci - e)).astype(BF16)
                ks = (kt[:lo_r] * jnp.exp(e - bc[:lo_r])).astype(BF16)
                oi = oi + _dot(_dot_nt(qs, ks).astype(BF16), v_bf[:lo_r])
            for s in range(sub):
                r = lo_r + s
                dec = jnp.exp(jnp.where(rows >= s, bci - bc[r:r + 1], -jnp.inf))
                a = jnp.sum(qi * dec * kt[r:r + 1], axis=-1, keepdims=True)
                oi = oi + a * v[r:r + 1]
            blocks.append(oi)
        o = jnp.concatenate(blocks, axis=0) if nb > 1 else blocks[0]
    bl = bc[c - 1:c]
    st = jnp.exp(bl) * st + _dot_tn(v_bf, (kt * jnp.exp(bl - bc)).astype(BF16))
    return o, st


def _hgrn(z, col0, lb, norm_g, s0_t, *, bsz, seq_pad, valid):
    c = min(HG_CHUNK, seq_pad)
    sub = min(HG_SUB, c)
    nc = seq_pad // c
    assert col0 % MIX_W == 0 and bsz % HG_SEQS == 0
    j0 = col0 // MIX_W
    z = z.reshape(bsz, seq_pad, z.shape[-1])
    tri = jnp.asarray(np.tril(np.ones((c, c))), BF16)
    col = lambda k: pl.BlockSpec((HG_SEQS, c, MIX_W), lambda b, n: (b, n, j0 + k))
    vec = pl.BlockSpec((1, MIX_W), lambda b, n: (0, 0))
    state = pl.BlockSpec((HG_SEQS, HG_H, HG_DV, HG_DK), lambda b, n: (b, 0, 0, 0))
    o, st = pl.pallas_call(
        functools.partial(_hgrn_kernel, c=c, sub=sub, valid=valid),
        grid=(bsz // HG_SEQS, nc),
        in_specs=[col(0), col(1), col(2), col(3), vec, vec, pl.BlockSpec((c, c), lambda b, n: (0, 0)), state],
        out_specs=[pl.BlockSpec((HG_SEQS, c, MIX_W), lambda b, n: (b, n, 0)), state],
        out_shape=[jax.ShapeDtypeStruct((bsz, seq_pad, MIX_W), BF16),
                   jax.ShapeDtypeStruct((bsz, HG_H, HG_DV, HG_DK), F32)],
        compiler_params=_cparams("parallel", "arbitrary"),
        name="hgrn2",
    )(z, z, z, z, lb.reshape(1, MIX_W), norm_g.reshape(1, MIX_W), tri, s0_t)
    return o.reshape(bsz * seq_pad, MIX_W), st


def _gmlp_kernel(u_ref, v_ref, lg_ref, lbias_ref, ws_ref, bs_ref, o_ref):
    vn = _layer_norm(jax.nn.gelu(v_ref[...]), lg_ref[...], lbias_ref[...]).astype(BF16)
    u = jax.nn.gelu(u_ref[...])
    r = lax.broadcasted_iota(jnp.int32, (GM_CHUNK, GM_CHUNK), 0)
    cc = lax.broadcasted_iota(jnp.int32, (GM_CHUNK, GM_CHUNK), 1)
    chunks = [slice(c * GM_CHUNK, (c + 1) * GM_CHUNK) for c in range(u.shape[0] // GM_CHUNK)]
    for g in range(GM_GROUPS):
        sl = slice(g * GM_GW, (g + 1) * GM_GW)
        ws = jnp.where(r >= cc, ws_ref[g], 0.0).astype(BF16)
        mixed = _dot(ws, jnp.concatenate([vn[rows, sl] for rows in chunks], axis=1))
        for c, rows in enumerate(chunks):
            m_c = mixed[:, c * GM_GW:(c + 1) * GM_GW] + bs_ref[g]
            o_ref[rows, sl] = (u[rows, sl] * m_c).astype(o_ref.dtype)


def _gmlp(z, col0, ln_g, ln_b, w_s, b_s):
    m = z.shape[0]
    assert col0 % MIX_W == 0
    j0 = col0 // MIX_W
    tm = math.gcd(m, GM_STEP_CHUNKS * GM_CHUNK)
    return pl.pallas_call(
        _gmlp_kernel,
        grid=(m // tm,),
        in_specs=[pl.BlockSpec((tm, MIX_W), lambda i: (i, j0)),
                  pl.BlockSpec((tm, MIX_W), lambda i: (i, j0 + 1)),
                  pl.BlockSpec((1, MIX_W), lambda i: (0, 0)),
                  pl.BlockSpec((1, MIX_W), lambda i: (0, 0)),
                  pl.BlockSpec((GM_GROUPS, GM_CHUNK, GM_CHUNK), lambda i: (0, 0, 0)),
                  pl.BlockSpec((GM_GROUPS, GM_CHUNK, 1), lambda i: (0, 0, 0))],
        out_specs=pl.BlockSpec((tm, MIX_W), lambda i: (i, 0)),
        out_shape=jax.ShapeDtypeStruct((m, MIX_W), BF16),
        compiler_params=_cparams("parallel"),
        name="gmlp",
    )(z, z, ln_g.reshape(1, MIX_W), ln_b.reshape(1, MIX_W), w_s, b_s[:, :, None])


def _gmlp_first_kernel(u_ref, v_ref, lg_ref, lbias_ref, w00_ref, b0_ref, o_ref, vn_ref):
    vn = _layer_norm(jax.nn.gelu(v_ref[...]), lg_ref[...], lbias_ref[...])
    vn_ref[...] = vn
    o_ref[...] = (jax.nn.gelu(u_ref[...]) * (vn * w00_ref[...] + b0_ref[...])).astype(o_ref.dtype)


def _gmlp_first(z, col0, ln_g, ln_b, w_s, b_s):
    m = z.shape[0]
    assert col0 % MIX_W == 0
    j0 = col0 // MIX_W
    w00 = jnp.repeat(w_s[:, 0, 0], GM_GW).reshape(1, MIX_W)
    b0 = jnp.repeat(b_s[:, 0], GM_GW).reshape(1, MIX_W)
    vec = pl.BlockSpec((1, MIX_W), lambda i: (0, 0))
    return pl.pallas_call(
        _gmlp_first_kernel,
        grid=(1,),
        in_specs=[pl.BlockSpec((m, MIX_W), lambda i: (0, j0)), pl.BlockSpec((m, MIX_W), lambda i: (0, j0 + 1)),
                  vec, vec, vec, vec],
        out_specs=[pl.BlockSpec((m, MIX_W), lambda i: (0, 0)), pl.BlockSpec((m, MIX_W), lambda i: (0, 0))],
        out_shape=[jax.ShapeDtypeStruct((m, MIX_W), BF16), jax.ShapeDtypeStruct((m, MIX_W), F32)],
        name="gmlp_first",
    )(z, z, ln_g.reshape(1, MIX_W), ln_b.reshape(1, MIX_W), w00, b0)


def _layer_weights(l, w_in, fox_bf, s5_lambda_re, s5_lambda_im, s5_log_dt, s5_b_re, s5_b_im, s5_c_re, s5_c_im,
                   s5_w_glu, hgrn_lb, w_branch, w_out, w_ffn_up, w_ffn_down):
    n_a = 3 * MIX_W + LANES
    o_r = 3 * MIX_W + FOX_H
    lb_all = jnp.cumsum(jax.nn.softmax(hgrn_lb.astype(F32), axis=0), axis=0)
    return dict(
        w_a=_cast_cols(w_in, l, 0, n_a),
        w_r=_cast_cols(w_in, l, o_r, w_in.shape[2] - o_r),
        bf=jnp.pad(fox_bf[l], (0, LANES - FOX_H)).reshape(1, LANES),
        s5=_s5_params(s5_lambda_re[l], s5_lambda_im[l], s5_log_dt[l], s5_b_re[l], s5_b_im[l],
                      s5_c_re[l], s5_c_im[l]),
        w_glu=s5_w_glu[l].astype(BF16),
        lb=lb_all[l] - lb_all[0],
        w_branch=w_branch, w_out=w_out, w_up=w_ffn_up, w_down=w_ffn_down,
    )


def _trunk_layer(x, l, lw, P, *, bsz, seq, fox_fn, s5_h0, hg_s0):
    m = bsz * seq
    x_bf, q_bf, kv_bf, k_rows, v_rows_fox, logf = _qkv_proj(x, lw["w_a"], lw["bf"])
    logf = logf[:, :FOX_H]
    o_a = fox_fn(q_bf, kv_bf, k_rows, v_rows_fox, logf)

    z = _proj(x_bf, lw["w_r"], n=Z_GATE)[0]
    su_tb = z[:, :MIX_W].reshape(bsz, seq, MIX_W).transpose(1, 0, 2).reshape(m, MIX_W)
    o_b_tb, s5_re, s5_im = _s5(su_tb, lw["s5"], P["s5_d"][l], lw["w_glu"], P["s5_b_glu"][l],
                               s5_h0[0].reshape(bsz, S5_STATE), s5_h0[1].reshape(bsz, S5_STATE),
                               bk=bsz, steps=seq)
    o_b = o_b_tb.reshape(seq, bsz, MIX_W).transpose(1, 0, 2).reshape(m, MIX_W)

    seq_pad = seq if seq % HG_CHUNK == 0 else -(-seq // BF16_ROWS) * BF16_ROWS
    zh, zh_col0 = z, Z_HG
    if seq_pad != seq:
        zh = z[:, Z_HG:Z_GM].reshape(bsz, seq, 4 * MIX_W)
        zh, zh_col0 = jnp.pad(zh, ((0, 0), (0, seq_pad - seq), (0, 0))).reshape(-1, 4 * MIX_W), 0
    o_c, hg_t = _hgrn(zh, zh_col0, lw["lb"], P["hgrn_norm_g"][l], hg_s0.transpose(0, 1, 3, 2),
                      bsz=bsz, seq_pad=seq_pad, valid=min(seq, HG_CHUNK))
    if seq_pad != seq:
        o_c = o_c.reshape(bsz, seq_pad, MIX_W)[:, :seq].reshape(m, MIX_W)
    hg_state = hg_t.transpose(0, 1, 3, 2)

    if seq == 1:
        o_d, v_rows = _gmlp_first(z, Z_GM, P["gmlp_ln_g"][l], P["gmlp_ln_b"][l], P["gmlp_w_s"][l], P["gmlp_b_s"][l])
    else:
        o_d = _gmlp(z, Z_GM, P["gmlp_ln_g"][l], P["gmlp_ln_b"][l], P["gmlp_w_s"][l], P["gmlp_b_s"][l])
        v_rows = None

    merged = _merge(x_bf, (o_a, o_b, o_c, o_d), lw["w_r"], Z_GATE, lw["w_branch"], l)
    x1, x1_bf = _matmul_res_ln(merged, lw["w_out"], l, x, P["ln1_g"][l], P["ln1_b"][l], out_dtypes=(F32, BF16))
    hff = _ffn_up(x1_bf, lw["w_up"], l)
    x2, = _matmul_res_ln(hff, lw["w_down"], l, x1, P["ln2_g"][l], P["ln2_b"][l], out_dtypes=(F32,))

    fk = k_rows.reshape(bsz, seq, FOX_H, FOX_HD)
    fv = v_rows_fox.reshape(bsz, seq, FOX_H, FOX_HD)
    state = (fk, fv, logf.reshape(bsz, seq, FOX_H), s5_re.reshape(bsz, S5_G, S5_P),
             s5_im.reshape(bsz, S5_G, S5_P), hg_state, v_rows)
    return x2, state


def kernel(x_prompt, x_sample, cache_k, cache_v, cache_logf, page_table, state_s5_re, state_s5_im, state_hgrn,
           w_in, fox_bf, s5_lambda_re, s5_lambda_im, s5_log_dt, s5_b_re, s5_b_im, s5_c_re, s5_c_im, s5_d,
           s5_w_glu, s5_b_glu, hgrn_lb, hgrn_norm_g, gmlp_ln_g, gmlp_ln_b, gmlp_w_s, gmlp_b_s, w_branch, w_out,
           ln1_g, ln1_b, w_ffn_up, w_ffn_down, ln2_g, ln2_b):
    b_p, l_p, _ = x_prompt.shape
    b_s, l_s, _ = x_sample.shape
    assert l_s == 1, "the sample group decodes one token per sequence"
    P = dict(s5_d=s5_d, s5_b_glu=s5_b_glu, hgrn_norm_g=hgrn_norm_g, gmlp_ln_g=gmlp_ln_g, gmlp_ln_b=gmlp_ln_b,
             gmlp_w_s=gmlp_w_s, gmlp_b_s=gmlp_b_s, ln1_g=ln1_g, ln1_b=ln1_b, ln2_g=ln2_g, ln2_b=ln2_b)

    xp = x_prompt.reshape(b_p * l_p, D_MODEL)
    xs = x_sample.reshape(b_s * l_s, D_MODEL)
    s5_zero = jnp.zeros((b_p, S5_G, S5_P), F32)
    hg_zero = jnp.zeros((b_p, HG_H, HG_DK, HG_DV), F32)
    p_states, s_states = [], []
    w_branch, w_out, w_ffn_down = (w.astype(BF16) for w in (w_branch, w_out, w_ffn_down))
    for l in range(DEPTH):
        lw = _layer_weights(l, w_in, fox_bf, s5_lambda_re, s5_lambda_im, s5_log_dt, s5_b_re, s5_b_im,
                            s5_c_re, s5_c_im, s5_w_glu, hgrn_lb, w_branch, w_out, w_ffn_up, w_ffn_down)

        def fox_p(q_bf, kv_bf, k_rows, v_rows, logf):
            c = _cumsum_lanes(logf.reshape(b_p, l_p, FOX_H).transpose(0, 2, 1).reshape(b_p * FOX_H, l_p))
            return _fox_prompt(q_bf, kv_bf, c.reshape(b_p, FOX_H, l_p), b_p, l_p)

        def fox_s(q_bf, kv_bf, k_rows, v_rows, logf, l=l):
            o = _fox_decode(page_table, q_bf, k_rows.reshape(b_s, MIX_W), v_rows.reshape(b_s, MIX_W),
                            logf, cache_k, cache_v, cache_logf, l)
            return o.astype(BF16)

        xp, sp = _trunk_layer(xp, l, lw, P, bsz=b_p, seq=l_p, fox_fn=fox_p,
                              s5_h0=(s5_zero, s5_zero), hg_s0=hg_zero)
        xs, ss = _trunk_layer(xs, l, lw, P, bsz=b_s, seq=l_s, fox_fn=fox_s,
                              s5_h0=(state_s5_re[l], state_s5_im[l]), hg_s0=state_hgrn[l])
        p_states.append(sp)
        s_states.append(ss)

    stack = lambda states, i: jnp.stack([s[i] for s in states])
    return (xp.reshape(b_p, l_p, D_MODEL), xs.reshape(b_s, l_s, D_MODEL),
            stack(p_states, 0), stack(p_states, 1), stack(p_states, 2),
            stack(s_states, 0), stack(s_states, 1), stack(s_states, 2),
            stack(p_states, 3), stack(p_states, 4), stack(s_states, 3), stack(s_states, 4),
            stack(p_states, 5), stack(s_states, 5),
            jnp.stack([s[6].reshape(b_s, l_s, MIX_W) for s in s_states]))
```

```python
import functools
import math

import numpy as np
import jax
import jax.numpy as jnp
from jax import lax
from jax.experimental import pallas as pl
from jax.experimental.pallas import tpu as pltpu

F32 = jnp.float32
BF16 = jnp.bfloat16

D_MODEL = 2048
DEPTH = 2
N_BRANCH = 4
MIX_W = D_MODEL // N_BRANCH
FOX_HD = 128
FOX_H = MIX_W // FOX_HD
S5_GROUP = 16
S5_G = MIX_W // S5_GROUP
S5_P = 64
S5_STATE = S5_G * S5_P
S5_SLAB = 512
HG_DK = 128
HG_DV = 128
HG_H = MIX_W // HG_DV
HG_CHUNK = 64
HG_SUB = 16
HG_SEQS = 8
HG_SAFE_RANGE = 44.0
GM_CHUNK = 128
GM_GROUPS = 4
GM_GW = MIX_W // GM_GROUPS
GM_STEP_CHUNKS = 8
D_FF = ((8 * D_MODEL // 3 + 255) // 256) * 256
DEEPNORM_ALPHA = (2 * DEPTH) ** 0.25
LN_EPS = 1e-5
RMS_EPS = 1e-6
LANES = 128
SUBLANES = 8
BF16_ROWS = 16
FOX_STEP_HEADS = 4
FOX_DECODE_PAGES = 32
FOX_DECODE_SEQS = 1
Z_HG = MIX_W
Z_GM = Z_HG + 4 * MIX_W
Z_GATE = Z_GM + 2 * MIX_W
VMEM_LIMIT = 48 * 1024 * 1024
VMEM_LIMIT_BIG = 58 * 1024 * 1024
PROJ_TN_MAX = 1792


def _cparams(*sem):
    return pltpu.CompilerParams(dimension_semantics=sem, vmem_limit_bytes=VMEM_LIMIT)


def _log_sigmoid(x):
    return jnp.minimum(x, 0.0) - jnp.log1p(jnp.exp(-jnp.abs(x)))


def _layer_norm(x, g, b):
    mu = jnp.mean(x, axis=-1, keepdims=True)
    xc = x - mu
    var = jnp.mean(xc * xc, axis=-1, keepdims=True)
    return xc * lax.rsqrt(var + LN_EPS) * g + b


def _split3(x):
    hi = x.astype(BF16)
    r1 = x - hi.astype(F32)
    mid = r1.astype(BF16)
    lo = (r1 - mid.astype(F32)).astype(BF16)
    return hi, mid, lo


def _dot(a, b):
    return jnp.dot(a, b, preferred_element_type=F32)


def _dot_nt(a, b):
    return lax.dot_general(a, b, (((1,), (1,)), ((), ())), preferred_element_type=F32)


def _dot_tn(a, b):
    return lax.dot_general(a, b, (((0,), (0,)), ((), ())), preferred_element_type=F32)


def _cast_cols_kernel(w_ref, o_ref, *, l, depth):
    n_kt = o_ref.shape[0] // LANES
    rows_per_col = n_kt * depth
    for cb in range(o_ref.shape[1] // LANES):
        for kt in range(n_kt):
            first = cb * LANES * rows_per_col + kt * depth + l
            tile = w_ref[pl.ds(first, LANES, stride=rows_per_col), :]
            o_ref[kt * LANES:(kt + 1) * LANES, cb * LANES:(cb + 1) * LANES] = tile.T.astype(o_ref.dtype)


def _cast_cols(w, l, col0, n):
    depth, k, n_all = w.shape
    rows_per_col = (k // LANES) * depth
    view = w.reshape(depth, k // LANES, LANES, n_all).transpose(3, 1, 0, 2).reshape(n_all * rows_per_col, LANES)
    tn = max(t for t in (LANES, 2 * LANES, 4 * LANES) if n % t == 0)
    blk = tn * rows_per_col
    return pl.pallas_call(
        functools.partial(_cast_cols_kernel, l=l, depth=depth),
        grid=(n // tn,),
        in_specs=[pl.BlockSpec((pl.Element(blk), pl.Element(LANES)),
                               lambda j: (pl.multiple_of(col0 * rows_per_col + j * blk, rows_per_col), 0))],
        out_specs=pl.BlockSpec((k, tn), lambda j: (0, j)),
        out_shape=jax.ShapeDtypeStruct((k, n), BF16),
        compiler_params=_cparams("parallel"),
        name="cast_cols",
    )(view)


def _proj_kernel(x_ref, w_ref, b_ref, *out_refs, act):
    z = _dot(x_ref[...], w_ref[...])
    if act == "gelu":
        z = jax.nn.gelu(z)
    elif act == "sigmoid":
        z = jax.nn.sigmoid(z)
    elif act == "log_sigmoid_bias":
        z = _log_sigmoid(z + b_ref[...])
    for o in out_refs:
        o[...] = z.astype(o.dtype)


def _proj(x, w, *, col0=0, n=None, act=None, bias=None, out_dtypes=(F32,)):
    m, k = x.shape
    n = w.shape[1] if n is None else n
    tm = min(m, 512)
    tn = max(t for t in range(LANES, min(n, PROJ_TN_MAX) + 1, LANES) if n % t == 0 and col0 % t == 0)
    j0 = col0 // tn
    if bias is None:
        bias = jnp.zeros((1, n), F32)
    return pl.pallas_call(
        functools.partial(_proj_kernel, act=act),
        grid=(n // tn, m // tm),
        in_specs=[pl.BlockSpec((tm, k), lambda j, i: (i, 0)),
                  pl.BlockSpec((k, tn), lambda j, i: (0, j0 + j)),
                  pl.BlockSpec((1, tn), lambda j, i: (0, j))],
        out_specs=[pl.BlockSpec((tm, tn), lambda j, i: (i, j)) for _ in out_dtypes],
        out_shape=[jax.ShapeDtypeStruct((m, n), d) for d in out_dtypes],
        compiler_params=_cparams("parallel", "arbitrary"),
        name="proj_" + (act or "id"),
    )(x, w, bias)


def _qkv_kernel(x_ref, w_ref, b_ref, xbf_ref, q_ref, kv_ref, k_ref, v_ref, lf_ref):
    x = x_ref[...].astype(BF16)
    xbf_ref[...] = x
    z = _dot(x, w_ref[...])
    tm = z.shape[0]
    q_ref[...] = z[:, :MIX_W].astype(q_ref.dtype)
    kv_ref[...] = z[:, MIX_W:3 * MIX_W].astype(kv_ref.dtype)
    for h in range(FOX_H):
        k_ref[pl.ds(h, tm, stride=FOX_H), :] = z[:, MIX_W + h * FOX_HD:MIX_W + (h + 1) * FOX_HD]
        v_ref[pl.ds(h, tm, stride=FOX_H), :] = z[:, 2 * MIX_W + h * FOX_HD:2 * MIX_W + (h + 1) * FOX_HD]
    lf_ref[...] = _log_sigmoid(z[:, 3 * MIX_W:] + b_ref[...])


def _qkv_proj(x, w, bias):
    m, k = x.shape
    tm = min(m, 512)
    n = w.shape[1]
    row = lambda width: pl.BlockSpec((tm, width), lambda i: (i, 0))
    rows = pl.BlockSpec((tm * FOX_H, FOX_HD), lambda i: (i, 0))
    return pl.pallas_call(
        _qkv_kernel,
        grid=(m // tm,),
        in_specs=[row(k), pl.BlockSpec((k, n), lambda i: (0, 0), pipeline_mode=pl.Buffered(1)),
                  pl.BlockSpec((1, LANES), lambda i: (0, 0))],
        out_specs=[row(k), row(MIX_W), row(2 * MIX_W), rows, rows, row(LANES)],
        out_shape=[jax.ShapeDtypeStruct((m, k), BF16), jax.ShapeDtypeStruct((m, MIX_W), BF16),
                   jax.ShapeDtypeStruct((m, 2 * MIX_W), BF16),
                   jax.ShapeDtypeStruct((m * FOX_H, FOX_HD), F32), jax.ShapeDtypeStruct((m * FOX_H, FOX_HD), F32),
                   jax.ShapeDtypeStruct((m, LANES), F32)],
        compiler_params=_cparams("parallel"),
        name="qkv_proj",
    )(x, w, bias)


def _merge_kernel(x_ref, oa_ref, ob_ref, oc_ref, od_ref, g0_ref, g1_ref, g2_ref, g3_ref, wb_ref, o_ref):
    x = x_ref[...]
    acc = None
    for br, gr, k in ((oa_ref, g0_ref, 0), (ob_ref, g1_ref, 1), (oc_ref, g2_ref, 2), (od_ref, g3_ref, 3)):
        t = jax.nn.sigmoid(_dot(x, gr[...])) * _dot(br[...], wb_ref[k])
        acc = t if acc is None else acc + t
    o_ref[...] = acc.astype(o_ref.dtype)


def _merge(x, branches, w, gate_col0, w_branch, l):
    m, k_in = x.shape
    tm = min(m, 512)
    tn = 512
    nb = D_MODEL // tn
    assert gate_col0 % tn == 0
    j0 = gate_col0 // tn
    gate_specs = [pl.BlockSpec((k_in, tn), functools.partial(lambda j, i, k: (0, j0 + k * nb + j), k=k))
                  for k in range(N_BRANCH)]
    return pl.pallas_call(
        _merge_kernel,
        grid=(nb, m // tm),
        in_specs=[pl.BlockSpec((tm, k_in), lambda j, i: (i, 0))]
                 + [pl.BlockSpec((tm, MIX_W), lambda j, i: (i, 0)) for _ in range(N_BRANCH)] + gate_specs
                 + [pl.BlockSpec((None, N_BRANCH, MIX_W, tn), lambda j, i: (l, 0, 0, j))],
        out_specs=pl.BlockSpec((tm, tn), lambda j, i: (i, j)),
        out_shape=jax.ShapeDtypeStruct((m, D_MODEL), BF16),
        compiler_params=_cparams("parallel", "arbitrary"),
        name="merge",
    )(x, *branches, w, w, w, w, w_branch)


def _res_ln_kernel(a_ref, w_ref, x_ref, g_ref, b_ref, *y_refs):
    y = _layer_norm(DEEPNORM_ALPHA * x_ref[...] + _dot(a_ref[...], w_ref[...]), g_ref[...], b_ref[...])
    for y_ref in y_refs:
        y_ref[...] = y.astype(y_ref.dtype)


def _matmul_res_ln(a, w, l, x, g, b, *, out_dtypes):
    m, k = a.shape
    n = w.shape[2]
    tm = min(m, 512 if k <= D_MODEL else 256)
    return pl.pallas_call(
        _res_ln_kernel,
        grid=(m // tm,),
        in_specs=[pl.BlockSpec((tm, k), lambda i: (i, 0)),
                  pl.BlockSpec((None, k, n), lambda i: (l, 0, 0), pipeline_mode=pl.Buffered(1)),
                  pl.BlockSpec((tm, n), lambda i: (i, 0)),
                  pl.BlockSpec((1, n), lambda i: (0, 0)),
                  pl.BlockSpec((1, n), lambda i: (0, 0))],
        out_specs=[pl.BlockSpec((tm, n), lambda i: (i, 0)) for _ in out_dtypes],
        out_shape=[jax.ShapeDtypeStruct((m, n), d) for d in out_dtypes],
        compiler_params=pltpu.CompilerParams(dimension_semantics=("parallel",), vmem_limit_bytes=VMEM_LIMIT_BIG),
        name="matmul_res_ln",
    )(a, w, x, g.reshape(1, n), b.reshape(1, n))


def _ffn_up_kernel(x_ref, wg_ref, wu_ref, o_ref, wg_sc, wu_sc):
    @pl.when(pl.program_id(1) == 0)
    def _():
        wg_sc[...] = wg_ref[...].astype(BF16)
        wu_sc[...] = wu_ref[...].astype(BF16)

    x = x_ref[...]
    o_ref[...] = (jax.nn.silu(_dot(x, wg_sc[...])) * _dot(x, wu_sc[...])).astype(o_ref.dtype)


def _ffn_up(x, w_up, l):
    m, k = x.shape
    tm = min(m, 1024)
    tn = 512
    nb = D_FF // tn
    return pl.pallas_call(
        _ffn_up_kernel,
        grid=(nb, m // tm),
        in_specs=[pl.BlockSpec((tm, k), lambda j, i: (i, 0)),
                  pl.BlockSpec((None, k, tn), lambda j, i: (l, 0, j)),
                  pl.BlockSpec((None, k, tn), lambda j, i: (l, 0, nb + j))],
        out_specs=pl.BlockSpec((tm, tn), lambda j, i: (i, j)),
        out_shape=jax.ShapeDtypeStruct((m, D_FF), BF16),
        scratch_shapes=[pltpu.VMEM((k, tn), BF16), pltpu.VMEM((k, tn), BF16)],
        compiler_params=_cparams("parallel", "arbitrary"),
        name="ffn_up",
    )(x, w_up, w_up)


def _cumsum_lanes_kernel(x_ref, o_ref):
    x = x_ref[...]
    n = x.shape[-1]
    lane = lax.broadcasted_iota(jnp.int32, x.shape, 1)
    sh = 1
    while sh < n:
        x = x + jnp.where(lane >= sh, pltpu.roll(x, sh, axis=1), 0.0)
        sh *= 2
    o_ref[...] = x


def _cumsum_lanes(x):
    return pl.pallas_call(_cumsum_lanes_kernel, out_shape=jax.ShapeDtypeStruct(x.shape, F32),
                          name="cumsum_lanes")(x)


def _flash_kernel(qi_ref, ki_ref, q_ref, k_ref, v_ref, cq_ref, ck_ref, o_ref, qt_sc, m_sc, l_sc, acc_sc, *,
                  tq, tk, scale):
    qi = qi_ref[pl.program_id(2)]
    ki = ki_ref[pl.program_id(2)]

    heads = [(h, slice(h * FOX_HD, (h + 1) * FOX_HD)) for h in range(m_sc.shape[0])]

    @pl.when(ki == 0)
    def _():
        for h, cols in heads:
            qt_sc[h] = q_ref[:, cols].astype(F32).T.astype(BF16)
        m_sc[...] = jnp.full_like(m_sc, -jnp.inf)
        l_sc[...] = jnp.zeros_like(l_sc)
        acc_sc[...] = jnp.zeros_like(acc_sc)

    def update(masked):
        for h, cols in heads:
            s = _dot(k_ref[:, cols], qt_sc[h]) * scale
            s = s + cq_ref[0, h] - jnp.concatenate([ck_ref[0, h]] * (tq // LANES), axis=1)
            if masked:
                k_pos = lax.broadcasted_iota(jnp.int32, (tk, tq), 0)
                q_pos = lax.broadcasted_iota(jnp.int32, (tk, tq), 1)
                s = jnp.where(k_pos <= q_pos, s, -jnp.inf)
            m_prev = m_sc[h]
            m_new = jnp.maximum(m_prev, jnp.max(s, axis=0, keepdims=True))
            alpha = jnp.exp(m_prev - m_new)
            p = jnp.exp(s - m_new)
            l_sc[h] = alpha * l_sc[h] + jnp.sum(p, axis=0, keepdims=True)
            acc_sc[h] = alpha * acc_sc[h] + _dot_tn(v_ref[:, cols], p.astype(BF16))
            m_sc[h] = m_new

    pl.when(ki < qi)(functools.partial(update, False))

    @pl.when(ki == qi)
    def _():
        update(True)
        for h, cols in heads:
            o_ref[:, cols] = (acc_sc[h] / l_sc[h]).T.astype(o_ref.dtype)


def _fox_prompt(q, kv, c, bsz, seq):
    tq = tk = min(seq, 512)
    nq = seq // tq
    nk = seq // tk
    cq = c.reshape(bsz, FOX_H, 1, seq)
    ck = jnp.broadcast_to(c[..., None], (bsz, FOX_H, seq, LANES))

    pairs = [(qi, ki) for qi in range(nq) for ki in range(qi + 1)]
    qi_tab = jnp.asarray([p[0] for p in pairs], jnp.int32)
    ki_tab = jnp.asarray([p[1] for p in pairs], jnp.int32)
    hs = FOX_STEP_HEADS
    ng = FOX_H // hs
    wide = hs * FOX_HD
    grid_spec = pltpu.PrefetchScalarGridSpec(
        num_scalar_prefetch=2,
        grid=(bsz, ng, len(pairs)),
        in_specs=[pl.BlockSpec((tq, wide), lambda b, g, s, qt, kt: (b * nq + qt[s], g)),
                  pl.BlockSpec((tk, wide), lambda b, g, s, qt, kt: (b * nk + kt[s], g)),
                  pl.BlockSpec((tk, wide), lambda b, g, s, qt, kt: (b * nk + kt[s], ng + g)),
                  pl.BlockSpec((1, hs, 1, tq), lambda b, g, s, qt, kt: (b, g, 0, qt[s])),
                  pl.BlockSpec((1, hs, tk, LANES), lambda b, g, s, qt, kt: (b, g, kt[s], 0))],
        out_specs=pl.BlockSpec((tq, wide), lambda b, g, s, qt, kt: (b * nq + qt[s], g)),
        scratch_shapes=[pltpu.VMEM((hs, FOX_HD, tq), BF16), pltpu.VMEM((hs, 1, tq), F32),
                        pltpu.VMEM((hs, 1, tq), F32), pltpu.VMEM((hs, FOX_HD, tq), F32)],
    )
    return pl.pallas_call(
        functools.partial(_flash_kernel, tq=tq, tk=tk, scale=FOX_HD ** -0.5),
        grid_spec=grid_spec,
        out_shape=jax.ShapeDtypeStruct((bsz * seq, MIX_W), BF16),
        compiler_params=_cparams("parallel", "parallel", "arbitrary"),
        name="fox_prompt",
    )(qi_tab, ki_tab, q, kv, kv, cq, ck)


def _fox_decode_kernel(pt_ref, q_ref, cnew_ref, knew_ref, vnew_ref, lf_pool_ref, *rest, scale, page, group, nseq,
                       n_pages):
    n = nseq * group
    o_ref, m_sc, l_sc, acc_sc, carry_sc = rest[2 * n:]
    b = pl.program_id(0)
    p = pl.program_id(1)

    @pl.when(p == 0)
    def _():
        m_sc[...] = jnp.full_like(m_sc, -jnp.inf)
        l_sc[...] = jnp.zeros_like(l_sc)
        acc_sc[...] = jnp.zeros_like(acc_sc)
        carry_sc[...] = jnp.zeros_like(carry_sc)

    for r in range(nseq):
        sl = slice(r * group, (r + 1) * group)
        newest = (b * nseq + r) * n_pages + (n_pages - 1) - p * group
        lf_views = [lf_pool_ref.at[pt_ref[newest - j]] for j in range(group)]
        _fox_decode_pages(q_ref.at[r], cnew_ref.at[r], rest[:n][sl], rest[n:2 * n][sl], lf_views,
                          m_sc.at[r], l_sc.at[r], acc_sc.at[r], carry_sc.at[r], scale=scale, page=page, group=group)

    @pl.when(p == pl.num_programs(1) - 1)
    def _():
        for r in range(nseq):
            _fox_decode_finish(q_ref.at[r], knew_ref.at[r], vnew_ref.at[r], o_ref.at[r], m_sc.at[r], l_sc.at[r],
                               acc_sc.at[r], scale=scale)


def _fox_decode_pages(q_ref, cnew_ref, k_refs, v_refs, lf_refs, m_sc, l_sc, acc_sc, carry_sc, *, scale, page, group):
    hcol = lax.broadcasted_iota(jnp.int32, (BF16_ROWS, 1), 0)
    later = jnp.where(lax.broadcasted_iota(jnp.int32, (page, page), 0)
                      > lax.broadcasted_iota(jnp.int32, (page, page), 1), 1.0, 0.0).astype(BF16)

    q = q_ref[...]
    carry = carry_sc[...]
    cnew = cnew_ref[...]
    parts = []
    for j in range(group):
        lf = lf_refs[j][...]
        lf16 = jnp.zeros((BF16_ROWS, page), F32)
        for h in range(FOX_H):
            lf16 = jnp.where(hcol == h, lf[h:h + 1, :], lf16)
        hi, mid, lo = _split3(lf16)
        suffix = _dot(hi, later) + _dot(mid, later) + _dot(lo, later)
        s = None
        for h in range(FOX_H):
            kh = k_refs[j][pl.ds(h, page, stride=FOX_H), :]
            t = _dot_nt(q[:, h * FOX_HD:(h + 1) * FOX_HD], kh.astype(BF16))
            s = t if s is None else s + t
        parts.append(s * scale + (suffix + carry + cnew))
        carry = carry + jnp.sum(lf16, axis=-1, keepdims=True)
    carry_sc[...] = carry
    s = jnp.concatenate(parts, axis=1)

    m_prev = m_sc[...]
    m_new = jnp.maximum(m_prev, jnp.max(s, axis=-1, keepdims=True))
    alpha = jnp.exp(m_prev - m_new)
    pr = jnp.exp(s - m_new)
    l_sc[...] = alpha * l_sc[...] + jnp.sum(pr, axis=-1, keepdims=True)
    pr = pr.astype(BF16)
    for h in range(FOX_H):
        cols = slice(h * FOX_HD, (h + 1) * FOX_HD)
        upd = None
        for j in range(group):
            vh = v_refs[j][pl.ds(h, page, stride=FOX_H), :]
            t = _dot(pr[:, j * page:(j + 1) * page], vh.astype(BF16))
            upd = t if upd is None else upd + t
        acc_sc[:, cols] = alpha * acc_sc[:, cols] + upd
    m_sc[...] = m_new


def _fox_decode_finish(q_ref, knew_ref, vnew_ref, o_ref, m_sc, l_sc, acc_sc, *, scale):
    hrow = lax.broadcasted_iota(jnp.int32, (BF16_ROWS, MIX_W), 0)
    lane = lax.broadcasted_iota(jnp.int32, (BF16_ROWS, MIX_W), 1)
    kn = knew_ref[...].astype(BF16).astype(F32)
    vn = vnew_ref[...].astype(BF16).astype(F32)
    s_new = jnp.sum(q_ref[...].astype(F32) * kn, axis=-1, keepdims=True) * scale
    m_prev = m_sc[...]
    m_fin = jnp.maximum(m_prev, s_new)
    alpha = jnp.exp(m_prev - m_fin)
    p_new = jnp.exp(s_new - m_fin)
    l_fin = alpha * l_sc[...] + p_new
    acc = alpha * acc_sc[...] + p_new.astype(BF16).astype(F32) * vn
    o = acc / l_fin
    o_ref[...] = jnp.sum(jnp.where((lane >> 7) == hrow, o, 0.0), axis=0, keepdims=True)


def _fox_decode(page_table, q, k_new, v_new, c_new, cache_k, cache_v, cache_lf, l):
    bsz, n_pages = page_table.shape
    depth, n_phys, page = cache_k.shape[:3]
    group = math.gcd(n_pages, FOX_DECODE_PAGES)
    cache_k = cache_k.reshape(depth, n_phys, page * FOX_H, FOX_HD)
    cache_v = cache_v.reshape(depth, n_phys, page * FOX_H, FOX_HD)
    cache_lf = cache_lf.transpose(0, 1, 3, 2)
    hmask = (np.arange(MIX_W)[None, :] // FOX_HD) == np.arange(BF16_ROWS)[:, None]
    q_rows = jnp.where(hmask[None], q[:, None, :], 0.0).astype(BF16)
    c_rows = jnp.pad(c_new, ((0, 0), (0, BF16_ROWS - FOX_H)))[:, :, None]
    pt = page_table.reshape(-1)

    nseq = math.gcd(bsz, FOX_DECODE_SEQS)
    slots = [(r, j) for r in range(nseq) for j in range(group)]

    def pg(r, j):
        return lambda b, p, pt_ref: pt_ref[(b * nseq + r) * n_pages + (n_pages - 1 - (p * group + j))]

    def kv_spec(r, j):
        return pl.BlockSpec((None, None, page * FOX_H, FOX_HD),
                            lambda b, p, pt_ref: (l, pg(r, j)(b, p, pt_ref), 0, 0))

    lf_pool = pl.BlockSpec((None, n_phys, FOX_H, page), lambda b, p, pt_ref: (l, 0, 0, 0),
                           pipeline_mode=pl.Buffered(1))
    per_seq = lambda *shape: pl.BlockSpec((nseq,) + shape, lambda b, p, pt_ref: (b,) + (0,) * len(shape))
    grid_spec = pltpu.PrefetchScalarGridSpec(
        num_scalar_prefetch=1,
        grid=(bsz // nseq, n_pages // group),
        in_specs=[per_seq(BF16_ROWS, MIX_W), per_seq(BF16_ROWS, 1), per_seq(1, MIX_W), per_seq(1, MIX_W), lf_pool]
                 + [kv_spec(r, j) for r, j in slots] + [kv_spec(r, j) for r, j in slots],
        out_specs=per_seq(1, MIX_W),
        scratch_shapes=[pltpu.VMEM((nseq, BF16_ROWS, 1), F32), pltpu.VMEM((nseq, BF16_ROWS, 1), F32),
                        pltpu.VMEM((nseq, BF16_ROWS, MIX_W), F32), pltpu.VMEM((nseq, BF16_ROWS, 1), F32)],
    )
    out = pl.pallas_call(
        functools.partial(_fox_decode_kernel, scale=FOX_HD ** -0.5, page=page, group=group, nseq=nseq,
                          n_pages=n_pages),
        grid_spec=grid_spec,
        out_shape=jax.ShapeDtypeStruct((bsz, 1, MIX_W), F32),
        compiler_params=pltpu.CompilerParams(dimension_semantics=("parallel", "arbitrary"),
                                             vmem_limit_bytes=VMEM_LIMIT_BIG),
        name="fox_decode",
    )(pt, q_rows, c_rows, k_new[:, None, :], v_new[:, None, :], cache_lf,
      *([cache_k] * len(slots)), *([cache_v] * len(slots)))
    return out.reshape(bsz, MIX_W)


def _s5_kernel(u_ref, bblk_ref, cblk_ref, are_ref, aim_ref, d_ref, wglu_ref, bglu_ref, h0re_ref, h0im_ref,
               o_ref, hre_ref, him_ref, hs_sc, *, tc, bk, slab):
    c = pl.program_id(0)

    @pl.when(c == 0)
    def _():
        hre_ref[...] = h0re_ref[...]
        him_ref[...] = h0im_ref[...]

    u = u_ref[...]
    u_bf = u.astype(BF16)
    n_u = slab // S5_P * S5_GROUP
    slabs = [(slice(s0, s0 + slab), slice(S5_STATE + s0, S5_STATE + s0 + slab)) for s0 in range(0, S5_STATE, slab)]
    for si, (re_sl, im_sl) in enumerate(slabs):
        u_sl = slice(si * n_u, (si + 1) * n_u)
        bu = _dot(u_bf[:, u_sl], bblk_ref[u_sl, :])
        hs_sc[:, re_sl] = bu[:, :slab]
        hs_sc[:, im_sl] = bu[:, slab:]

    for pair in (slabs[i:i + 2] for i in range(0, len(slabs), 2)):
        coef = [(jnp.broadcast_to(are_ref[:, re_sl], (bk, slab)), jnp.broadcast_to(aim_ref[:, re_sl], (bk, slab)))
                for re_sl, _ in pair]

        def step(t, carry, pair=pair, coef=coef):
            rows = pl.ds(pl.multiple_of(t * bk, bk), bk)
            out = []
            for (re_sl, im_sl), (ar, ai), (hr, hi) in zip(pair, coef, carry):
                nr = ar * hr - ai * hi + hs_sc[rows, re_sl]
                ni = ar * hi + ai * hr + hs_sc[rows, im_sl]
                hs_sc[rows, re_sl] = nr
                hs_sc[rows, im_sl] = ni
                out.append((nr, ni))
            return tuple(out)

        fin = lax.fori_loop(0, tc, step, tuple((hre_ref[:, re_sl], him_ref[:, re_sl]) for re_sl, _ in pair))
        for (re_sl, _), (hr, hi) in zip(pair, fin):
            hre_ref[:, re_sl] = hr
            him_ref[:, re_sl] = hi

    y_parts = []
    for si, (re_sl, im_sl) in enumerate(slabs):
        h_bf = jnp.concatenate([hs_sc[:, re_sl], hs_sc[:, im_sl]], axis=1).astype(BF16)
        y_parts.append(_dot(h_bf, cblk_ref[si * 2 * slab:(si + 1) * 2 * slab, :]))
    y = jnp.concatenate(y_parts, axis=1) + d_ref[...] * u
    g = jax.nn.gelu(y)
    gate = jax.nn.sigmoid(_dot(g.astype(BF16), wglu_ref[...]) + bglu_ref[...])
    o_ref[...] = (g * gate).astype(o_ref.dtype)


def _s5_params(lam_re, lam_im, log_dt, b_re, b_im, c_re, c_im):
    dt = jnp.exp(log_dt)[:, None]
    mag = jnp.exp(lam_re * dt)
    ab_re, ab_im = mag * jnp.cos(lam_im * dt), mag * jnp.sin(lam_im * dt)
    den = lam_re * lam_re + lam_im * lam_im
    nr, ni = ab_re - 1.0, ab_im
    coef_re = (nr * lam_re + ni * lam_im) / den
    coef_im = (ni * lam_re - nr * lam_im) / den
    bb_re = coef_re[..., None] * b_re - coef_im[..., None] * b_im
    bb_im = coef_re[..., None] * b_im + coef_im[..., None] * b_re
    gs = S5_SLAB // S5_P
    ns = S5_G // gs
    eye = jnp.eye(gs, dtype=F32)
    blk_b = lambda t: jnp.einsum("sgph,gk->sghkp", t.reshape(ns, gs, S5_P, S5_GROUP), eye).reshape(MIX_W, S5_SLAB)
    blk_c = lambda t: jnp.einsum("sghp,gk->sgpkh", t.reshape(ns, gs, S5_GROUP, S5_P), eye).reshape(
        ns, S5_SLAB, gs * S5_GROUP)
    bblk = jnp.concatenate([blk_b(bb_re), blk_b(bb_im)], axis=1).astype(BF16)
    cblk = jnp.concatenate([blk_c(c_re), blk_c(-c_im)], axis=1).astype(BF16)
    return (ab_re.reshape(1, S5_STATE), ab_im.reshape(1, S5_STATE), bblk,
            cblk.reshape(ns * 2 * S5_SLAB, gs * S5_GROUP))


def _s5(u_tb, params, d, w_glu, b_glu, h0_re, h0_im, *, bk, steps):
    a_re, a_im, bblk, cblk = params
    tc = min(steps, 64)
    rows = tc * bk
    const = lambda shape: pl.BlockSpec(shape, lambda c: (0,) * len(shape))
    return pl.pallas_call(
        functools.partial(_s5_kernel, tc=tc, bk=bk, slab=S5_SLAB),
        grid=(steps // tc,),
        in_specs=[pl.BlockSpec((rows, MIX_W), lambda c: (c, 0)),
                  const(bblk.shape), const(cblk.shape),
                  const((1, S5_STATE)), const((1, S5_STATE)), const((1, MIX_W)),
                  const((MIX_W, MIX_W)), const((1, MIX_W)),
                  const((bk, S5_STATE)), const((bk, S5_STATE))],
        out_specs=[pl.BlockSpec((rows, MIX_W), lambda c: (c, 0)),
                   const((bk, S5_STATE)), const((bk, S5_STATE))],
        out_shape=[jax.ShapeDtypeStruct((steps * bk, MIX_W), BF16),
                   jax.ShapeDtypeStruct((bk, S5_STATE), F32), jax.ShapeDtypeStruct((bk, S5_STATE), F32)],
        scratch_shapes=[pltpu.VMEM((rows, 2 * S5_STATE), F32)],
        compiler_params=_cparams("arbitrary"),
        name="s5",
    )(u_tb, bblk, cblk, a_re, a_im, d.reshape(1, MIX_W), w_glu, b_glu.reshape(1, MIX_W), h0_re, h0_im)


def _hgrn_kernel(q_ref, f_ref, i_ref, g_ref, lb_ref, ng_ref, tri_ref, s0_ref, o_ref, st_ref, *, c, sub, valid):
    n = pl.program_id(1)

    @pl.when(n == 0)
    def _():
        st_ref[...] = s0_ref[...]

    n_seq = q_ref.shape[0]
    lb = lb_ref[...]
    la = jnp.log(lb)
    lb1 = jnp.log1p(-lb)
    decay = []
    worst = jnp.zeros((1, MIX_W), F32)
    for r in range(n_seq):
        lbb = lb1 + _log_sigmoid(f_ref[r])
        lf = jnp.maximum(la, lbb) + jnp.log1p(jnp.exp(-jnp.abs(la - lbb)))
        if valid < c:
            lf = jnp.where(lax.broadcasted_iota(jnp.int32, lf.shape, 0) < valid, lf, 0.0)
        parts = _dot(tri_ref[...], jnp.concatenate(_split3(lf), axis=1))
        bc = parts[:, :MIX_W] + parts[:, MIX_W:2 * MIX_W] + parts[:, 2 * MIX_W:]
        decay.append((1.0 - jnp.exp(lf), bc))
        mid = bc[c // 2 - 1:c // 2]
        worst = jnp.maximum(worst, jnp.maximum(-mid, mid - bc[c - 1:c]))
    safe = jnp.max(worst) < HG_SAFE_RANGE

    def run(exact):
        for r in range(n_seq):
            kt_all, bc_all = decay[r]
            for h in range(HG_H):
                cols = slice(h * HG_DK, (h + 1) * HG_DK)
                o, st = _hgrn_head(q_ref[r, :, cols], kt_all[:, cols], bc_all[:, cols], i_ref[r, :, cols],
                                   st_ref[r, h], c=c, sub=sub, exact=exact)
                st_ref[r, h] = st
                o = o * lax.rsqrt(jnp.mean(o * o, axis=-1, keepdims=True) + RMS_EPS) * ng_ref[:, cols]
                o_ref[r, :, cols] = (o * jax.nn.silu(g_ref[r, :, cols])).astype(o_ref.dtype)

    pl.when(safe)(functools.partial(run, False))
    pl.when(jnp.logical_not(safe))(functools.partial(run, True))


def _hgrn_head(q, kt, bc, v, st, *, c, sub, exact):
    v_bf = v.astype(BF16)
    o = _dot_nt((q * jnp.exp(bc)).astype(BF16), st.astype(BF16))
    if not exact:
        rel = bc - bc[c // 2 - 1:c // 2]
        att = _dot_nt((q * jnp.exp(rel)).astype(BF16), (kt * jnp.exp(-rel)).astype(BF16))
        t_pos = lax.broadcasted_iota(jnp.int32, (c, c), 0)
        s_pos = lax.broadcasted_iota(jnp.int32, (c, c), 1)
        o = o + _dot(jnp.where(s_pos <= t_pos, att, 0.0).astype(BF16), v_bf)
    else:
        nb = c // sub
        rows = lax.broadcasted_iota(jnp.int32, (sub, HG_DK), 0)
        blocks = []
        for i in range(nb):
            lo_r, hi_r = i * sub, (i + 1) * sub
            bci = bc[lo_r:hi_r]
            qi = q[lo_r:hi_r]
            oi = o[lo_r:hi_r]
            if i > 0:
                e = bc[lo_r - 1:lo_r]
                qs = (qi * jnp.exp(bci - e)).astype(BF16)
                ks = (kt[:lo_r] * jnp.exp(e - bc[:lo_r])).astype(BF16)
                oi = oi + _dot(_dot_nt(qs, ks).astype(BF16), v_bf[:lo_r])
            for s in range(sub):
                r = lo_r + s
                dec = jnp.exp(jnp.where(rows >= s, bci - bc[r:r + 1], -jnp.inf))
                a = jnp.sum(qi * dec * kt[r:r + 1], axis=-1, keepdims=True)
                oi = oi + a * v[r:r + 1]
            blocks.append(oi)
        o = jnp.concatenate(blocks, axis=0) if nb > 1 else blocks[0]
    bl = bc[c - 1:c]
    st = jnp.exp(bl) * st + _dot_tn(v_bf, (kt * jnp.exp(bl - bc)).astype(BF16))
    return o, st


def _hgrn(z, col0, lb, norm_g, s0_t, *, bsz, seq_pad, valid):
    c = min(HG_CHUNK, seq_pad)
    sub = min(HG_SUB, c)
    nc = seq_pad // c
    assert col0 % MIX_W == 0 and bsz % HG_SEQS == 0
    j0 = col0 // MIX_W
    z = z.reshape(bsz, seq_pad, z.shape[-1])
    tri = jnp.asarray(np.tril(np.ones((c, c))), BF16)
    col = lambda k: pl.BlockSpec((HG_SEQS, c, MIX_W), lambda b, n: (b, n, j0 + k))
    vec = pl.BlockSpec((1, MIX_W), lambda b, n: (0, 0))
    state = pl.BlockSpec((HG_SEQS, HG_H, HG_DV, HG_DK), lambda b, n: (b, 0, 0, 0))
    o, st = pl.pallas_call(
        functools.partial(_hgrn_kernel, c=c, sub=sub, valid=valid),
        grid=(bsz // HG_SEQS, nc),
        in_specs=[col(0), col(1), col(2), col(3), vec, vec, pl.BlockSpec((c, c), lambda b, n: (0, 0)), state],
        out_specs=[pl.BlockSpec((HG_SEQS, c, MIX_W), lambda b, n: (b, n, 0)), state],
        out_shape=[jax.ShapeDtypeStruct((bsz, seq_pad, MIX_W), BF16),
                   jax.ShapeDtypeStruct((bsz, HG_H, HG_DV, HG_DK), F32)],
        compiler_params=_cparams("parallel", "arbitrary"),
        name="hgrn2",
    )(z, z, z, z, lb.reshape(1, MIX_W), norm_g.reshape(1, MIX_W), tri, s0_t)
    return o.reshape(bsz * seq_pad, MIX_W), st


def _gmlp_kernel(u_ref, v_ref, lg_ref, lbias_ref, ws_ref, bs_ref, o_ref):
    vn = _layer_norm(jax.nn.gelu(v_ref[...]), lg_ref[...], lbias_ref[...]).astype(BF16)
    u = jax.nn.gelu(u_ref[...])
    r = lax.broadcasted_iota(jnp.int32, (GM_CHUNK, GM_CHUNK), 0)
    cc = lax.broadcasted_iota(jnp.int32, (GM_CHUNK, GM_CHUNK), 1)
    chunks = [slice(c * GM_CHUNK, (c + 1) * GM_CHUNK) for c in range(u.shape[0] // GM_CHUNK)]
    for g in range(GM_GROUPS):
        sl = slice(g * GM_GW, (g + 1) * GM_GW)
        ws = jnp.where(r >= cc, ws_ref[g], 0.0).astype(BF16)
        mixed = _dot(ws, jnp.concatenate([vn[rows, sl] for rows in chunks], axis=1))
        for c, rows in enumerate(chunks):
            m_c = mixed[:, c * GM_GW:(c + 1) * GM_GW] + bs_ref[g]
            o_ref[rows, sl] = (u[rows, sl] * m_c).astype(o_ref.dtype)


def _gmlp(z, col0, ln_g, ln_b, w_s, b_s):
    m = z.shape[0]
    assert col0 % MIX_W == 0
    j0 = col0 // MIX_W
    tm = math.gcd(m, GM_STEP_CHUNKS * GM_CHUNK)
    return pl.pallas_call(
        _gmlp_kernel,
        grid=(m // tm,),
        in_specs=[pl.BlockSpec((tm, MIX_W), lambda i: (i, j0)),
                  pl.BlockSpec((tm, MIX_W), lambda i: (i, j0 + 1)),
                  pl.BlockSpec((1, MIX_W), lambda i: (0, 0)),
                  pl.BlockSpec((1, MIX_W), lambda i: (0, 0)),
                  pl.BlockSpec((GM_GROUPS, GM_CHUNK, GM_CHUNK), lambda i: (0, 0, 0)),
                  pl.BlockSpec((GM_GROUPS, GM_CHUNK, 1), lambda i: (0, 0, 0))],
        out_specs=pl.BlockSpec((tm, MIX_W), lambda i: (i, 0)),
        out_shape=jax.ShapeDtypeStruct((m, MIX_W), BF16),
        compiler_params=_cparams("parallel"),
        name="gmlp",
    )(z, z, ln_g.reshape(1, MIX_W), ln_b.reshape(1, MIX_W), w_s, b_s[:, :, None])


def _gmlp_first_kernel(u_ref, v_ref, lg_ref, lbias_ref, w00_ref, b0_ref, o_ref, vn_ref):
    vn = _layer_norm(jax.nn.gelu(v_ref[...]), lg_ref[...], lbias_ref[...])
    vn_ref[...] = vn
    o_ref[...] = (jax.nn.gelu(u_ref[...]) * (vn * w00_ref[...] + b0_ref[...])).astype(o_ref.dtype)


def _gmlp_first(z, col0, ln_g, ln_b, w_s, b_s):
    m = z.shape[0]
    assert col0 % MIX_W == 0
    j0 = col0 // MIX_W
    w00 = jnp.repeat(w_s[:, 0, 0], GM_GW).reshape(1, MIX_W)
    b0 = jnp.repeat(b_s[:, 0], GM_GW).reshape(1, MIX_W)
    vec = pl.BlockSpec((1, MIX_W), lambda i: (0, 0))
    return pl.pallas_call(
        _gmlp_first_kernel,
        grid=(1,),
        in_specs=[pl.BlockSpec((m, MIX_W), lambda i: (0, j0)), pl.BlockSpec((m, MIX_W), lambda i: (0, j0 + 1)),
                  vec, vec, vec, vec],
        out_specs=[pl.BlockSpec((m, MIX_W), lambda i: (0, 0)), pl.BlockSpec((m, MIX_W), lambda i: (0, 0))],
        out_shape=[jax.ShapeDtypeStruct((m, MIX_W), BF16), jax.ShapeDtypeStruct((m, MIX_W), F32)],
        name="gmlp_first",
    )(z, z, ln_g.reshape(1, MIX_W), ln_b.reshape(1, MIX_W), w00, b0)


def _layer_weights(l, w_in, fox_bf, s5_lambda_re, s5_lambda_im, s5_log_dt, s5_b_re, s5_b_im, s5_c_re, s5_c_im,
                   s5_w_glu, hgrn_lb, w_branch, w_out, w_ffn_up, w_ffn_down):
    n_a = 3 * MIX_W + LANES
    o_r = 3 * MIX_W + FOX_H
    lb_all = jnp.cumsum(jax.nn.softmax(hgrn_lb.astype(F32), axis=0), axis=0)
    return dict(
        w_a=_cast_cols(w_in, l, 0, n_a),
        w_r=_cast_cols(w_in, l, o_r, w_in.shape[2] - o_r),
        bf=jnp.pad(fox_bf[l], (0, LANES - FOX_H)).reshape(1, LANES),
        s5=_s5_params(s5_lambda_re[l], s5_lambda_im[l], s5_log_dt[l], s5_b_re[l], s5_b_im[l],
                      s5_c_re[l], s5_c_im[l]),
        w_glu=s5_w_glu[l].astype(BF16),
        lb=lb_all[l] - lb_all[0],
        w_branch=w_branch, w_out=w_out, w_up=w_ffn_up, w_down=w_ffn_down,
    )


def _trunk_layer(x, l, lw, P, *, bsz, seq, fox_fn, s5_h0, hg_s0):
    m = bsz * seq
    x_bf, q_bf, kv_bf, k_rows, v_rows_fox, logf = _qkv_proj(x, lw["w_a"], lw["bf"])
    logf = logf[:, :FOX_H]
    o_a = fox_fn(q_bf, kv_bf, k_rows, v_rows_fox, logf)

    z = _proj(x_bf, lw["w_r"], n=Z_GATE)[0]
    su_tb = z[:, :MIX_W].reshape(bsz, seq, MIX_W).transpose(1, 0, 2).reshape(m, MIX_W)
    o_b_tb, s5_re, s5_im = _s5(su_tb, lw["s5"], P["s5_d"][l], lw["w_glu"], P["s5_b_glu"][l],
                               s5_h0[0].reshape(bsz, S5_STATE), s5_h0[1].reshape(bsz, S5_STATE),
                               bk=bsz, steps=seq)
    o_b = o_b_tb.reshape(seq, bsz, MIX_W).transpose(1, 0, 2).reshape(m, MIX_W)

    seq_pad = seq if seq % HG_CHUNK == 0 else -(-seq // BF16_ROWS) * BF16_ROWS
    zh, zh_col0 = z, Z_HG
    if seq_pad != seq:
        zh = z[:, Z_HG:Z_GM].reshape(bsz, seq, 4 * MIX_W)
        zh, zh_col0 = jnp.pad(zh, ((0, 0), (0, seq_pad - seq), (0, 0))).reshape(-1, 4 * MIX_W), 0
    o_c, hg_t = _hgrn(zh, zh_col0, lw["lb"], P["hgrn_norm_g"][l], hg_s0.transpose(0, 1, 3, 2),
                      bsz=bsz, seq_pad=seq_pad, valid=min(seq, HG_CHUNK))
    if seq_pad != seq:
        o_c = o_c.reshape(bsz, seq_pad, MIX_W)[:, :seq].reshape(m, MIX_W)
    hg_state = hg_t.transpose(0, 1, 3, 2)

    if seq == 1:
        o_d, v_rows = _gmlp_first(z, Z_GM, P["gmlp_ln_g"][l], P["gmlp_ln_b"][l], P["gmlp_w_s"][l], P["gmlp_b_s"][l])
    else:
        o_d = _gmlp(z, Z_GM, P["gmlp_ln_g"][l], P["gmlp_ln_b"][l], P["gmlp_w_s"][l], P["gmlp_b_s"][l])
        v_rows = None

    merged = _merge(x_bf, (o_a, o_b, o_c, o_d), lw["w_r"], Z_GATE, lw["w_branch"], l)
    x1, x1_bf = _matmul_res_ln(merged, lw["w_out"], l, x, P["ln1_g"][l], P["ln1_b"][l], out_dtypes=(F32, BF16))
    hff = _ffn_up(x1_bf, lw["w_up"], l)
    x2, = _matmul_res_ln(hff, lw["w_down"], l, x1, P["ln2_g"][l], P["ln2_b"][l], out_dtypes=(F32,))

    fk = k_rows.reshape(bsz, seq, FOX_H, FOX_HD)
    fv = v_rows_fox.reshape(bsz, seq, FOX_H, FOX_HD)
    state = (fk, fv, logf.reshape(bsz, seq, FOX_H), s5_re.reshape(bsz, S5_G, S5_P),
             s5_im.reshape(bsz, S5_G, S5_P), hg_state, v_rows)
    return x2, state


def kernel(x_prompt, x_sample, cache_k, cache_v, cache_logf, page_table, state_s5_re, state_s5_im, state_hgrn,
           w_in, fox_bf, s5_lambda_re, s5_lambda_im, s5_log_dt, s5_b_re, s5_b_im, s5_c_re, s5_c_im, s5_d,
           s5_w_glu, s5_b_glu, hgrn_lb, hgrn_norm_g, gmlp_ln_g, gmlp_ln_b, gmlp_w_s, gmlp_b_s, w_branch, w_out,
           ln1_g, ln1_b, w_ffn_up, w_ffn_down, ln2_g, ln2_b):
    b_p, l_p, _ = x_prompt.shape
    b_s, l_s, _ = x_sample.shape
    assert l_s == 1, "the sample group decodes one token per sequence"
    P = dict(s5_d=s5_d, s5_b_glu=s5_b_glu, hgrn_norm_g=hgrn_norm_g, gmlp_ln_g=gmlp_ln_g, gmlp_ln_b=gmlp_ln_b,
             gmlp_w_s=gmlp_w_s, gmlp_b_s=gmlp_b_s, ln1_g=ln1_g, ln1_b=ln1_b, ln2_g=ln2_g, ln2_b=ln2_b)

    xp = x_prompt.reshape(b_p * l_p, D_MODEL)
    xs = x_sample.reshape(b_s * l_s, D_MODEL)
    s5_zero = jnp.zeros((b_p, S5_G, S5_P), F32)
    hg_zero = jnp.zeros((b_p, HG_H, HG_DK, HG_DV), F32)
    p_states, s_states = [], []
    w_branch, w_out, w_ffn_down = (w.astype(BF16) for w in (w_branch, w_out, w_ffn_down))
    for l in range(DEPTH):
        lw = _layer_weights(l, w_in, fox_bf, s5_lambda_re, s5_lambda_im, s5_log_dt, s5_b_re, s5_b_im,
                            s5_c_re, s5_c_im, s5_w_glu, hgrn_lb, w_branch, w_out, w_ffn_up, w_ffn_down)

        def fox_p(q_bf, kv_bf, k_rows, v_rows, logf):
            c = _cumsum_lanes(logf.reshape(b_p, l_p, FOX_H).transpose(0, 2, 1).reshape(b_p * FOX_H, l_p))
            return _fox_prompt(q_bf, kv_bf, c.reshape(b_p, FOX_H, l_p), b_p, l_p)

        def fox_s(q_bf, kv_bf, k_rows, v_rows, logf, l=l):
            o = _fox_decode(page_table, q_bf, k_rows.reshape(b_s, MIX_W), v_rows.reshape(b_s, MIX_W),
                            logf, cache_k, cache_v, cache_logf, l)
            return o.astype(BF16)

        xp, sp = _trunk_layer(xp, l, lw, P, bsz=b_p, seq=l_p, fox_fn=fox_p,
                              s5_h0=(s5_zero, s5_zero), hg_s0=hg_zero)
        xs, ss = _trunk_layer(xs, l, lw, P, bsz=b_s, seq=l_s, fox_fn=fox_s,
                              s5_h0=(state_s5_re[l], state_s5_im[l]), hg_s0=state_hgrn[l])
        p_states.append(sp)
        s_states.append(ss)

    stack = lambda states, i: jnp.stack([s[i] for s in states])
    return (xp.reshape(b_p, l_p, D_MODEL), xs.reshape(b_s, l_s, D_MODEL),
            stack(p_states, 0), stack(p_states, 1), stack(p_states, 2),
            stack(s_states, 0), stack(s_states, 1), stack(s_states, 2),
            stack(p_states, 3), stack(p_states, 4), stack(s_states, 3), stack(s_states, 4),
            stack(p_states, 5), stack(s_states, 5),
            jnp.stack([s[6].reshape(b_s, l_s, MIX_W) for s in s_states]))
```
